```python
import math
import jax, jax.numpy as jnp
from jax import lax
import numpy as np

D_MODEL = 1024
BATCH = 8
SEQ = 2048
DEPTH = 1
DEC_BATCH = 128
DEC_SEQ = 8
PAST_LEN = 16384
PAGE_SIZE = 128

D_MIX = 2 * D_MODEL
SSD_WIDTH = D_MIX // 2
SSD_HEADDIM = 64
SSD_HEADS = SSD_WIDTH // SSD_HEADDIM
SSD_GROUPS = 2
SSD_STATE = 128
SSD_CONV_DIM = SSD_WIDTH + 2 * SSD_GROUPS * SSD_STATE
GDN_WIDTH = D_MIX - SSD_WIDTH
GDN_HEAD_DIM = 128
GDN_HEADS = GDN_WIDTH // GDN_HEAD_DIM
GDN_KEY_DIM = 128
GDN_QK_WIDTH = GDN_HEADS * GDN_KEY_DIM
GDN_CONV_DIM = 2 * GDN_QK_WIDTH + GDN_WIDTH
CONV_WIDTH = 4
CHUNK = 64
D_FF = -(-8 * D_MODEL // (3 * 256)) * 256
IN_SIZES = (SSD_WIDTH, SSD_CONV_DIM, SSD_HEADS, GDN_CONV_DIM, GDN_WIDTH, GDN_HEADS, GDN_HEADS)
IN_DIM = sum(IN_SIZES)
NORM_EPS = 1e-6
L2_EPS = 1e-6

kernel_name = "hymba_ssd_gdn_step"


def _rms_norm(x, w):
    xf = x.astype(jnp.float32)
    y = xf * lax.rsqrt(jnp.mean(xf * xf, axis=-1, keepdims=True) + NORM_EPS)
    return (y * w.astype(jnp.float32)).astype(x.dtype)


def _l2norm(x):
    xf = x.astype(jnp.float32)
    return xf * lax.rsqrt(jnp.sum(xf * xf, axis=-1, keepdims=True) + L2_EPS)


def _causal_dwconv(x, buf, w, b=None):
    xx = jnp.concatenate([buf.astype(x.dtype), x], axis=1)
    y = lax.conv_general_dilated(xx, w[:, None, :].astype(x.dtype), (1,), "VALID",
                                 dimension_numbers=("NWC", "WIO", "NWC"),
                                 feature_group_count=x.shape[-1])
    if b is not None:
        y = y + b.astype(y.dtype)
    return y, xx[:, xx.shape[1] - (CONV_WIDTH - 1):].astype(buf.dtype)


def _pad_time(a, pad):
    return jnp.pad(a, [(0, 0), (0, pad)] + [(0, 0)] * (a.ndim - 2))


def _ssd_chunked(x, dt, A, Bm, Cm, h0):
    b, L, H, P = x.shape
    G, N = Bm.shape[2], Bm.shape[3]
    R = H // G
    cs = min(CHUNK, L)
    nc = -(-L // cs)
    pad = nc * cs - L
    x, dt, Bm, Cm = (_pad_time(t, pad) for t in (x, dt, Bm, Cm))
    xc = x.reshape(b, nc, cs, G, R, P)
    dtc = dt.reshape(b, nc, cs, G, R)
    Bc = Bm.reshape(b, nc, cs, G, N)
    Cc = Cm.reshape(b, nc, cs, G, N)
    acs = jnp.cumsum(dtc * A.reshape(G, R), axis=2)
    xdt = xc * dtc[..., None]
    causal = jnp.tril(jnp.ones((cs, cs), bool))[:, :, None, None]
    seg = acs[:, :, :, None] - acs[:, :, None, :]
    decay_ls = jnp.exp(jnp.where(causal, seg, -jnp.inf))
    cb = jnp.einsum("bclgn,bcsgn->bclsg", Cc, Bc)
    y_diag = jnp.einsum("bclsgr,bcsgrp->bclgrp", cb[..., None] * decay_ls, xdt)
    decay_end = jnp.exp(acs[:, :, -1:] - acs)
    chunk_states = jnp.einsum("bclgn,bclgrp->bcgrpn", Bc, xdt * decay_end[..., None])
    chunk_decay = jnp.exp(acs[:, :, -1])

    def step(h, inp):
        st, dec = inp
        return h * dec[..., None, None] + st, h

    h_last, h_in = lax.scan(step, h0.reshape(b, G, R, P, N),
                            (jnp.moveaxis(chunk_states, 1, 0), jnp.moveaxis(chunk_decay, 1, 0)))
    h_in = jnp.moveaxis(h_in, 0, 1)
    y_off = jnp.einsum("bclgn,bcgrpn->bclgrp", Cc, h_in) * jnp.exp(acs)[..., None]
    y = (y_diag + y_off).reshape(b, nc * cs, H, P)[:, :L]
    return y, h_last.reshape(b, H, P, N)


def _gdn_chunked(q, k, v, g, beta, S0):
    b, L, H, Dk = q.shape
    cs = min(CHUNK, L)
    nc = -(-L // cs)
    pad = nc * cs - L
    q, k, v, g, beta = (_pad_time(t, pad) for t in (q, k, v, g, beta))

    def heads_first(t):
        return jnp.moveaxis(t.reshape((b, nc, cs) + t.shape[2:]), 3, 1)

    q, k, v, g, beta = (heads_first(t) for t in (q, k, v, g, beta))
    gc = jnp.cumsum(g, axis=-1)
    incl = jnp.tril(jnp.ones((cs, cs), bool))
    strict = jnp.tril(jnp.ones((cs, cs), bool), k=-1)
    decay = jnp.exp(jnp.where(incl, gc[..., :, None] - gc[..., None, :], -jnp.inf))
    kb = k * beta[..., None]
    lower = jnp.einsum("bhcld,bhcsd->bhcls", kb, k) * jnp.where(strict, decay, 0.0)
    tri = lower + jnp.eye(cs, dtype=lower.dtype)
    U = lax.linalg.triangular_solve(tri, v * beta[..., None], left_side=True, lower=True, unit_diagonal=True)
    W = lax.linalg.triangular_solve(tri, kb * jnp.exp(gc)[..., None], left_side=True, lower=True, unit_diagonal=True)
    attn = jnp.einsum("bhcld,bhcsd->bhcls", q, k) * decay
    q_in = q * jnp.exp(gc)[..., None]
    k_out = k * jnp.exp(gc[..., -1:] - gc)[..., None]
    g_end = jnp.exp(gc[..., -1])

    def step(S, inp):
        Uc, Wc, attn_c, q_c, k_c, ge = inp
        v_new = Uc - jnp.einsum("bhld,bhde->bhle", Wc, S)
        o = jnp.einsum("bhld,bhde->bhle", q_c, S) + jnp.einsum("bhls,bhse->bhle", attn_c, v_new)
        S = S * ge[..., None, None] + jnp.einsum("bhld,bhle->bhde", k_c, v_new)
        return S, o

    xs = tuple(jnp.moveaxis(t, 2, 0) for t in (U, W, attn, q_in, k_out, g_end))
    S_last, o = lax.scan(step, S0, xs)
    o = jnp.moveaxis(o, 0, 2).reshape(b, H, nc * cs, -1)
    o = jnp.moveaxis(o, 1, 2)[:, :L]
    return o, S_last


def _mixer(h, ssd_h0, ssd_conv_buf, gdn_S0, gdn_conv_buf, w_in, ssd_conv_w, ssd_conv_b, ssd_dt_bias,
           ssd_A_log, ssd_D, ssd_norm_w, gdn_conv_w, gdn_dt_bias, gdn_A_log, gdn_norm_w, w_out):
    f32 = jnp.float32
    b, L, _ = h.shape
    proj = h @ w_in
    z, xbc, dt_raw, qkv, gate, beta_raw, alpha_raw = jnp.split(proj, list(np.cumsum(IN_SIZES)[:-1]), axis=-1)
    xbc, ssd_conv_new = _causal_dwconv(xbc, ssd_conv_buf, ssd_conv_w, ssd_conv_b)
    xbc = jax.nn.silu(xbc.astype(f32))
    xs, Bm, Cm = jnp.split(xbc, [SSD_WIDTH, SSD_WIDTH + SSD_GROUPS * SSD_STATE], axis=-1)
    xs = xs.reshape(b, L, SSD_HEADS, SSD_HEADDIM)
    dt = jax.nn.softplus(dt_raw.astype(f32) + ssd_dt_bias.astype(f32))
    A = -jnp.exp(ssd_A_log.astype(f32))
    y, ssd_h = _ssd_chunked(xs, dt, A, Bm.reshape(b, L, SSD_GROUPS, SSD_STATE),
                            Cm.reshape(b, L, SSD_GROUPS, SSD_STATE), ssd_h0.astype(f32))
    y = y + ssd_D.astype(f32)[:, None] * xs
    gsz = SSD_WIDTH // SSD_GROUPS
    y = y.reshape(b, L, SSD_GROUPS, gsz) * jax.nn.silu(z.astype(f32)).reshape(b, L, SSD_GROUPS, gsz)
    y = _rms_norm(y, ssd_norm_w.reshape(SSD_GROUPS, gsz)).reshape(b, L, SSD_WIDTH)
    qkv, gdn_conv_new = _causal_dwconv(qkv, gdn_conv_buf, gdn_conv_w)
    qkv = jax.nn.silu(qkv.astype(f32))
    q, k, v = jnp.split(qkv, [GDN_QK_WIDTH, 2 * GDN_QK_WIDTH], axis=-1)
    q = _l2norm(q.reshape(b, L, GDN_HEADS, GDN_KEY_DIM)) * (GDN_KEY_DIM ** -0.5)
    k = _l2norm(k.reshape(b, L, GDN_HEADS, GDN_KEY_DIM))
    v = v.reshape(b, L, GDN_HEADS, GDN_HEAD_DIM)
    beta = jax.nn.sigmoid(beta_raw.astype(f32))
    g = -jnp.exp(gdn_A_log.astype(f32)) * jax.nn.softplus(alpha_raw.astype(f32) + gdn_dt_bias.astype(f32))
    o, gdn_S = _gdn_chunked(q, k, v, g, beta, gdn_S0.astype(f32))
    o = _rms_norm(o, gdn_norm_w) * jax.nn.silu(gate.astype(f32)).reshape(b, L, GDN_HEADS, GDN_HEAD_DIM)
    mixed = jnp.concatenate([y, o.reshape(b, L, GDN_WIDTH)], axis=-1).astype(h.dtype)
    return (mixed @ w_out, ssd_h.astype(ssd_h0.dtype), ssd_conv_new,
            gdn_S.astype(gdn_S0.dtype), gdn_conv_new)


def _layer(x, c, ssd_h0, ssd_conv_buf, gdn_S0, gdn_conv_buf, w_ada, b_ada, norm_mix_pre, norm_mix_post,
           norm_ffn_pre, norm_ffn_post, w_in, ssd_conv_w, ssd_conv_b, ssd_dt_bias, ssd_A_log, ssd_D,
           ssd_norm_w, gdn_conv_w, gdn_dt_bias, gdn_A_log, gdn_norm_w, w_out, w_gate_up, w_down):
    mod = jax.nn.silu(c) @ w_ada + b_ada
    sh1, sc1, g1, sh2, sc2, g2 = jnp.split(mod[:, None, :], 6, axis=-1)
    h = _rms_norm(x, norm_mix_pre) * (1 + sc1) + sh1
    m, ssd_h, ssd_conv, gdn_S, gdn_conv = _mixer(
        h, ssd_h0, ssd_conv_buf, gdn_S0, gdn_conv_buf, w_in, ssd_conv_w, ssd_conv_b, ssd_dt_bias,
        ssd_A_log, ssd_D, ssd_norm_w, gdn_conv_w, gdn_dt_bias, gdn_A_log, gdn_norm_w, w_out)
    x = x + g1 * _rms_norm(m, norm_mix_post)
    h = _rms_norm(x, norm_ffn_pre) * (1 + sc2) + sh2
    gu = h @ w_gate_up
    f = (jax.nn.silu(gu[..., :D_FF]) * gu[..., D_FF:]) @ w_down
    x = x + g2 * _rms_norm(f, norm_ffn_post)
    return x, ssd_h, ssd_conv, gdn_S, gdn_conv


def setup_inputs(seed: int = 0) -> dict:
    key = jax.random.key(seed)
    ks = jax.random.split(key, 32)
    f32 = jnp.float32

    def nrm(k, shape, scale):
        return jax.random.normal(k, shape, f32) * scale

    def gain(k, shape):
        return 1.0 + 0.05 * jax.random.normal(k, shape, f32)

    def dt_bias(k, n):
        dt = jnp.exp(jax.random.uniform(k, (DEPTH, n), f32) * (math.log(0.1) - math.log(0.001)) + math.log(0.001))
        return dt + jnp.log(-jnp.expm1(-dt))

    def a_log(k, n):
        return jnp.log(jax.random.uniform(k, (DEPTH, n), f32, minval=1.0, maxval=16.0))

    return {
        "x_prompt": nrm(ks[0], (BATCH, SEQ, D_MODEL), 1.0),
        "x_sample": nrm(ks[1], (DEC_BATCH, DEC_SEQ, D_MODEL), 1.0),
        "c_prompt": nrm(ks[2], (BATCH, D_MODEL), 1.0),
        "c_sample": nrm(ks[3], (DEC_BATCH, D_MODEL), 1.0),
        "state_ssd": nrm(ks[4], (DEPTH, DEC_BATCH, SSD_HEADS, SSD_HEADDIM, SSD_STATE), 0.1),
        "state_ssd_conv": nrm(ks[5], (DEPTH, DEC_BATCH, CONV_WIDTH - 1, SSD_CONV_DIM), 1.0),
        "state_gdn": nrm(ks[6], (DEPTH, DEC_BATCH, GDN_HEADS, GDN_KEY_DIM, GDN_HEAD_DIM), 0.1),
        "state_gdn_conv": nrm(ks[7], (DEPTH, DEC_BATCH, CONV_WIDTH - 1, GDN_CONV_DIM), 1.0),
        "w_ada": nrm(ks[8], (DEPTH, D_MODEL, 6 * D_MODEL), 0.5 * D_MODEL ** -0.5),
        "b_ada": nrm(ks[9], (DEPTH, 6 * D_MODEL), 0.02),
        "norm_mix_pre": gain(ks[10], (DEPTH, D_MODEL)),
        "norm_mix_post": gain(ks[11], (DEPTH, D_MODEL)),
        "norm_ffn_pre": gain(ks[12], (DEPTH, D_MODEL)),
        "norm_ffn_post": gain(ks[13], (DEPTH, D_MODEL)),
        "w_in": nrm(ks[14], (DEPTH, D_MODEL, IN_DIM), D_MODEL ** -0.5),
        "ssd_conv_w": nrm(ks[15], (DEPTH, CONV_WIDTH, SSD_CONV_DIM), CONV_WIDTH ** -0.5),
        "ssd_conv_b": nrm(ks[16], (DEPTH, SSD_CONV_DIM), 0.02),
        "ssd_dt_bias": dt_bias(ks[17], SSD_HEADS),
        "ssd_A_log": a_log(ks[18], SSD_HEADS),
        "ssd_D": gain(ks[19], (DEPTH, SSD_HEADS)),
        "ssd_norm_w": gain(ks[20], (DEPTH, SSD_WIDTH)),
        "gdn_conv_w": nrm(ks[21], (DEPTH, CONV_WIDTH, GDN_CONV_DIM), CONV_WIDTH ** -0.5),
        "gdn_dt_bias": dt_bias(ks[22], GDN_HEADS),
        "gdn_A_log": a_log(ks[23], GDN_HEADS),
        "gdn_norm_w": gain(ks[24], (DEPTH, GDN_HEAD_DIM)),
        "w_out": nrm(ks[25], (DEPTH, D_MIX, D_MODEL), D_MIX ** -0.5),
        "w_gate_up": nrm(ks[26], (DEPTH, D_MODEL, 2 * D_FF), D_MODEL ** -0.5),
        "w_down": nrm(ks[27], (DEPTH, D_FF, D_MODEL), D_FF ** -0.5),
    }


def reference(x_prompt, x_sample, c_prompt, c_sample, state_ssd, state_ssd_conv, state_gdn, state_gdn_conv,
              w_ada, b_ada, norm_mix_pre, norm_mix_post, norm_ffn_pre, norm_ffn_post, w_in, ssd_conv_w,
              ssd_conv_b, ssd_dt_bias, ssd_A_log, ssd_D, ssd_norm_w, gdn_conv_w, gdn_dt_bias, gdn_A_log,
              gdn_norm_w, w_out, w_gate_up, w_down):
    bp = x_prompt.shape[0]
    yp, ys = x_prompt, x_sample
    p_ssd, p_ssd_conv, p_gdn, p_gdn_conv = [], [], [], []
    s_ssd, s_ssd_conv, s_gdn, s_gdn_conv = [], [], [], []
    for l in range(DEPTH):
        lp = (w_ada[l], b_ada[l], norm_mix_pre[l], norm_mix_post[l], norm_ffn_pre[l], norm_ffn_post[l],
              w_in[l], ssd_conv_w[l], ssd_conv_b[l], ssd_dt_bias[l], ssd_A_log[l], ssd_D[l], ssd_norm_w[l],
              gdn_conv_w[l], gdn_dt_bias[l], gdn_A_log[l], gdn_norm_w[l], w_out[l], w_gate_up[l], w_down[l])
        yp, a0, a1, a2, a3 = _layer(
            yp, c_prompt,
            jnp.zeros((bp,) + state_ssd.shape[2:], state_ssd.dtype),
            jnp.zeros((bp,) + state_ssd_conv.shape[2:], state_ssd_conv.dtype),
            jnp.zeros((bp,) + state_gdn.shape[2:], state_gdn.dtype),
            jnp.zeros((bp,) + state_gdn_conv.shape[2:], state_gdn_conv.dtype),
            *lp)
        p_ssd.append(a0); p_ssd_conv.append(a1); p_gdn.append(a2); p_gdn_conv.append(a3)
        ys, b0, b1, b2, b3 = _layer(ys, c_sample, state_ssd[l], state_ssd_conv[l], state_gdn[l],
                                    state_gdn_conv[l], *lp)
        s_ssd.append(b0); s_ssd_conv.append(b1); s_gdn.append(b2); s_gdn_conv.append(b3)
    return (yp, ys, jnp.stack(p_ssd), jnp.stack(p_ssd_conv), jnp.stack(p_gdn), jnp.stack(p_gdn_conv),
            jnp.stack(s_ssd), jnp.stack(s_ssd_conv), jnp.stack(s_gdn), jnp.stack(s_gdn_conv))
```

```python
import functools

import numpy as np
import jax
import jax.numpy as jnp
from jax import lax
from jax.experimental import pallas as pl
from jax.experimental.pallas import tpu as pltpu

F32 = jnp.float32
BF16 = jnp.bfloat16
HIGHEST = lax.Precision.HIGHEST

D_MODEL = 1024
SSD_HEADS = 16
SSD_HEADDIM = 64
SSD_GROUPS = 2
SSD_STATE = 128
SSD_WIDTH = SSD_HEADS * SSD_HEADDIM
SSD_CONV_DIM = SSD_WIDTH + 2 * SSD_GROUPS * SSD_STATE
GDN_HEADS = 8
GDN_DIM = 128
GDN_WIDTH = GDN_HEADS * GDN_DIM
GDN_CONV_DIM = 3 * GDN_WIDTH
CONV_WIDTH = 4
CHUNK = 64
D_FF = 2816
NORM_EPS = 1e-6
L2_EPS = 1e-6
LANES = 128
SUBLANES = 8
BIG_COLS = SSD_WIDTH + SSD_CONV_DIM + GDN_CONV_DIM + GDN_WIDTH
DT_LANE0, BETA_LANE0, ALPHA_LANE0 = 0, SSD_HEADS, SSD_HEADS + GDN_HEADS
VMEM_LIMIT = 56 * 1024 * 1024
ROW_TILE = 256


def _dot(a, b):
    return jnp.dot(a, b, preferred_element_type=F32)


def _dot_hi(a, b):
    return jnp.dot(a, b, preferred_element_type=F32, precision=HIGHEST)


def _dot_nt(a, b):
    return lax.dot_general(a, b, (((1,), (1,)), ((), ())), preferred_element_type=F32)


def _dot_tn(a, b):
    return lax.dot_general(a, b, (((0,), (0,)), ((), ())), preferred_element_type=F32)


def _bf16_pieces(x):
    hi = x.astype(BF16)
    r1 = x - hi.astype(F32)
    mid = r1.astype(BF16)
    lo = (r1 - mid.astype(F32)).astype(BF16)
    return hi, mid, lo


def _split3_lanes(x):
    hi, mid, lo = _bf16_pieces(x)
    return jnp.concatenate([hi.astype(F32), mid.astype(F32), lo.astype(F32)], axis=1).astype(BF16)


def _dot_exact_rhs(a_bf16, x):
    hi, mid, lo = _bf16_pieces(x)
    return _dot(a_bf16, hi) + _dot(a_bf16, mid) + _dot(a_bf16, lo)


def _silu(x):
    return x * jax.nn.sigmoid(x)


def _softplus(x):
    return jnp.maximum(x, 0.0) + jnp.log1p(jnp.exp(-jnp.abs(x)))


def _rms(x, w):
    return x * lax.rsqrt(jnp.mean(x * x, axis=-1, keepdims=True) + NORM_EPS) * w


def _causal_conv(ext, w_ref, q):
    acc = ext[SUBLANES:SUBLANES + q] * w_ref[CONV_WIDTH - 1:CONV_WIDTH, :]
    for j in range(1, CONV_WIDTH):
        acc = acc + ext[SUBLANES - j:SUBLANES - j + q] * w_ref[CONV_WIDTH - 1 - j:CONV_WIDTH - j, :]
    return acc


def _mod_kernel(c_ref, w_ref, b_ref, o_ref):
    a = _silu(c_ref[...]).astype(BF16)
    o_ref[...] = _dot(a, w_ref[...].astype(BF16)) + b_ref[...]


def _mod_call(c_all, w_ada, b_ada):
    m, d = c_all.shape
    n = w_ada.shape[1]
    tn = 512
    return pl.pallas_call(
        _mod_kernel,
        grid=(n // tn,),
        in_specs=[pl.BlockSpec((m, d), lambda j: (0, 0)),
                  pl.BlockSpec((d, tn), lambda j: (0, j)),
                  pl.BlockSpec((1, tn), lambda j: (0, j))],
        out_specs=pl.BlockSpec((m, tn), lambda j: (0, j)),
        out_shape=jax.ShapeDtypeStruct((m, n), F32),
        compiler_params=pltpu.CompilerParams(dimension_semantics=("arbitrary",),
                                             vmem_limit_bytes=VMEM_LIMIT),
        name="adaln_mod",
    )(c_all, w_ada, b_ada.reshape(1, n))


def _inproj_kernel(x_ref, sh_ref, sc_ref, nw_ref, wbig_ref, wsm_ref,
                   z_ref, xbc_ref, qkv_ref, gate_ref, sm_ref):
    x = x_ref[...]
    h = _rms(x, nw_ref[...]) * (1.0 + sc_ref[...]) + sh_ref[...]
    h = h.reshape(x.shape[0] * x.shape[1], x.shape[2]).astype(BF16)
    c0 = 0
    for ref in (z_ref, xbc_ref, qkv_ref, gate_ref):
        w = ref.shape[1]
        ref[...] = _dot(h, wbig_ref[:, c0:c0 + w])
        c0 += w
    sm_ref[...] = _dot(h, wsm_ref[...])


def _const_spec(shape):
    nd = len(shape)
    return pl.BlockSpec(shape, lambda *_: (0,) * nd, pipeline_mode=pl.Buffered(1))


def _inproj_call(x, mod3, norm_w, w_big, w_small, bb, lb):
    b, l, d = x.shape
    nl = l // lb
    tm = bb * lb
    t = b * l
    row = lambda i, j: (i * nl + j, 0)
    widths = (SSD_WIDTH, SSD_CONV_DIM, GDN_CONV_DIM, GDN_WIDTH, LANES)
    return pl.pallas_call(
        _inproj_kernel,
        grid=(b // bb, nl),
        in_specs=[pl.BlockSpec((bb, lb, d), lambda i, j: (i, j, 0)),
                  pl.BlockSpec((bb, 1, d), lambda i, j: (i, 0, 0)),
                  pl.BlockSpec((bb, 1, d), lambda i, j: (i, 0, 1)),
                  _const_spec((1, d)),
                  _const_spec(w_big.shape),
                  _const_spec(w_small.shape)],
        out_specs=[pl.BlockSpec((tm, w), row) for w in widths],
        out_shape=[jax.ShapeDtypeStruct((t, w), F32) for w in widths],
        compiler_params=pltpu.CompilerParams(dimension_semantics=("arbitrary", "arbitrary"),
                                             vmem_limit_bytes=VMEM_LIMIT),
        name="inproj",
    )(x, mod3, mod3, norm_w, w_big, w_small)


def _ssd_constants(q):
    hq = SSD_HEADS * q
    hp = LANES // q
    w = hp * SSD_HEADDIM
    e_p = np.zeros((LANES, SSD_WIDTH), np.float32)
    e_q = np.zeros((LANES, hq), np.float32)
    for r in range(SSD_HEADS):
        e_p[DT_LANE0 + r, r * SSD_HEADDIM:(r + 1) * SSD_HEADDIM] = 1.0
        e_q[DT_LANE0 + r, r * q:(r + 1) * q] = 1.0
    tri = np.tril(np.ones((q, q), np.float32))
    gmask = np.zeros((hq, 2 * SSD_STATE), np.float32)
    half = hq // SSD_GROUPS
    gmask[:half, :SSD_STATE] = 1.0
    gmask[half:, SSD_STATE:] = 1.0
    xmask = np.kron(np.eye(hp, dtype=np.float32), np.ones((q, SSD_HEADDIM), np.float32))
    return dict(
        l3=jnp.asarray(tri, BF16),
        ones3=jnp.ones((q, q), BF16),
        e3p=jnp.asarray(np.tile(e_p, (3, 1)), BF16),
        e3q=jnp.asarray(np.tile(e_q, (3, 1)), BF16),
        dmask=jnp.asarray(np.tile(np.eye(q, dtype=np.float32), (1, SSD_HEADS))),
        causal=jnp.asarray(np.tile(tri, (1, SSD_HEADS))),
        gmask=jnp.asarray(gmask),
        xmask=jnp.asarray(xmask),
    ), hp, w


def _ssd_kernel(xbc_ref, z_ref, sm_ref, h0_ref, conv0_ref, cw_ref, cb_ref, dtb_ref, a_ref, d_ref, nw_ref,
                l3_ref, ones3_ref, e3p_ref, e3q_ref, dmask_ref, causal_ref, gmask_ref, xmask_ref,
                y_ref, hout_ref, convout_ref, ht_ref, tail_ref, *, q, hp, w):
    c = pl.program_id(1)
    nc = pl.num_programs(1)

    @pl.when(c == 0)
    def _():
        ht_ref[...] = h0_ref[0].T
        tail_ref[...] = conv0_ref[0]

    ext = jnp.concatenate([tail_ref[...], xbc_ref[...]], axis=0)
    act = _silu(_causal_conv(ext, cw_ref, q) + cb_ref[...])
    tail_ref[...] = ext[q:q + SUBLANES]
    convout_ref[0] = ext[q + SUBLANES - (CONV_WIDTH - 1):q + SUBLANES]
    xs = act[:, :SSD_WIDTH]
    bcat = act[:, SSD_WIDTH:SSD_WIDTH + 2 * SSD_STATE]
    ccat = act[:, SSD_WIDTH + 2 * SSD_STATE:]

    dt = _softplus(sm_ref[...] + dtb_ref[...])
    acs = _dot_exact_rhs(l3_ref[...], dt * a_ref[...])
    dec_end = jnp.exp(acs[q - 1:q, :] - acs)
    eacs = jnp.exp(acs)
    per_head = jnp.concatenate([dt, dt * dec_end, eacs], axis=0)
    exp3 = _dot(_split3_lanes(per_head), e3p_ref[...])
    dt_e, dtd_e, eacs_e = exp3[:q], exp3[q:2 * q], exp3[2 * q:]

    col_all = _dot(_split3_lanes(acs), e3q_ref[...])
    row_all = _dot_exact_rhs(ones3_ref[...], col_all * dmask_ref[...])
    decay = jnp.exp(jnp.minimum(col_all - row_all, 0.0)) * causal_ref[...]
    brep = (jnp.concatenate([bcat] * SSD_HEADS, axis=0) * gmask_ref[...]).astype(BF16)
    cb_all = _dot_nt(ccat.astype(BF16), brep)
    m_all = (cb_all * decay).astype(BF16)

    xdt = xs * dt_e
    y_blocks = []
    for j in range(SSD_WIDTH // w):
        xj = (jnp.concatenate([xdt[:, j * w:(j + 1) * w]] * hp, axis=0) * xmask_ref[...]).astype(BF16)
        y_blocks.append(_dot(m_all[:, j * LANES:(j + 1) * LANES], xj))
    y_diag = y_blocks[0] if len(y_blocks) == 1 else jnp.concatenate(y_blocks, axis=1)

    ht = ht_ref[...]
    ht_bf = ht.astype(BF16)
    xd = (xs * dtd_e).astype(BF16)
    gw = SSD_WIDTH // SSD_GROUPS
    y_off, upd = [], []
    for g in range(SSD_GROUPS):
        cg = ccat[:, g * SSD_STATE:(g + 1) * SSD_STATE].astype(BF16)
        bg = bcat[:, g * SSD_STATE:(g + 1) * SSD_STATE].astype(BF16)
        y_off.append(_dot(cg, ht_bf[:, g * gw:(g + 1) * gw]))
        upd.append(_dot_tn(bg, xd[:, g * gw:(g + 1) * gw]))
    y = y_diag + jnp.concatenate(y_off, axis=1) * eacs_e + d_ref[...] * xs
    ht_new = ht * eacs_e[q - 1:q, :] + jnp.concatenate(upd, axis=1)
    ht_ref[...] = ht_new

    yz = y * _silu(z_ref[...])
    nw = nw_ref[...]
    outs = [_rms(yz[:, g * gw:(g + 1) * gw], nw[:, g * gw:(g + 1) * gw]) for g in range(SSD_GROUPS)]
    y_ref[...] = jnp.concatenate(outs, axis=1).astype(y_ref.dtype)

    @pl.when(c == nc - 1)
    def _():
        hout_ref[0] = ht_new.T


def _ssd_call(xbc, z, small, h0, conv0, cw, cb, dtb, a_pad, d_e, nw, b, l, q):
    consts, hp, w = _ssd_constants(q)
    nc = l // q
    t = b * l
    row = lambda i, c: (i * nc + c, 0)
    params = [cw, cb, dtb, a_pad, d_e, nw] + [consts[k] for k in
                                              ("l3", "ones3", "e3p", "e3q", "dmask", "causal", "gmask", "xmask")]
    hp_shape = (SSD_WIDTH, SSD_STATE)
    return pl.pallas_call(
        functools.partial(_ssd_kernel, q=q, hp=hp, w=w),
        grid=(b, nc),
        in_specs=[pl.BlockSpec((q, SSD_CONV_DIM), row),
                  pl.BlockSpec((q, SSD_WIDTH), row),
                  pl.BlockSpec((q, LANES), row),
                  pl.BlockSpec((1,) + hp_shape, lambda i, c: (i, 0, 0)),
                  pl.BlockSpec((1, SUBLANES, SSD_CONV_DIM), lambda i, c: (i, 0, 0))]
                 + [_const_spec(p.shape) for p in params],
        out_specs=[pl.BlockSpec((q, SSD_WIDTH), row),
                   pl.BlockSpec((1,) + hp_shape, lambda i, c: (i, 0, 0)),
                   pl.BlockSpec((1, CONV_WIDTH - 1, SSD_CONV_DIM), lambda i, c: (i, 0, 0))],
        out_shape=[jax.ShapeDtypeStruct((t, SSD_WIDTH), F32),
                   jax.ShapeDtypeStruct((b,) + hp_shape, F32),
                   jax.ShapeDtypeStruct((b, CONV_WIDTH - 1, SSD_CONV_DIM), F32)],
        scratch_shapes=[pltpu.VMEM((SSD_STATE, SSD_WIDTH), F32),
                        pltpu.VMEM((SUBLANES, SSD_CONV_DIM), F32)],
        compiler_params=pltpu.CompilerParams(dimension_semantics=("arbitrary", "arbitrary"),
                                             vmem_limit_bytes=VMEM_LIMIT),
        name="ssd_scan",
    )(xbc, z, small, h0, conv0, *params)


def _gdn_constants(q):
    e_beta = np.zeros((LANES, GDN_WIDTH), np.float32)
    e_alpha = np.zeros((LANES, GDN_WIDTH), np.float32)
    dmask = np.zeros((q, GDN_WIDTH), np.float32)
    incl = np.zeros((q, GDN_WIDTH), np.float32)
    tri = np.tril(np.ones((q, q), np.float32))
    for h in range(GDN_HEADS):
        e_beta[BETA_LANE0 + h, h * GDN_DIM:(h + 1) * GDN_DIM] = 1.0
        e_alpha[ALPHA_LANE0 + h, h * GDN_DIM:(h + 1) * GDN_DIM] = 1.0
        dmask[:, h * GDN_DIM:h * GDN_DIM + q] = np.eye(q)
        incl[:, h * GDN_DIM:h * GDN_DIM + q] = tri
    return dict(
        l3=jnp.asarray(tri, BF16),
        ones3=jnp.ones((q, q), BF16),
        e3b=jnp.asarray(np.tile(e_beta, (3, 1)), BF16),
        e3a=jnp.asarray(np.tile(e_alpha, (3, 1)), BF16),
        dmask=jnp.asarray(dmask),
        incl=jnp.asarray(incl),
        strict=jnp.asarray(np.tril(np.ones((q, q), np.float32), k=-1)),
        eye=jnp.asarray(np.eye(q, dtype=np.float32)),
    )


def _gdn_kernel(qkv_ref, gate_ref, sm_ref, s0_ref, conv0_ref, cw_ref, dtb_ref, a_ref, nw_ref,
                l3_ref, ones3_ref, e3b_ref, e3a_ref, dmask_ref, incl_ref, strict_ref, eye_ref,
                o_ref, sout_ref, convout_ref, s_ref, tail_ref, *, q):
    c = pl.program_id(1)
    nc = pl.num_programs(1)

    @pl.when(c == 0)
    def _():
        s_ref[...] = s0_ref[0]
        tail_ref[...] = conv0_ref[0]

    ext = jnp.concatenate([tail_ref[...], qkv_ref[...]], axis=0)
    act = _silu(_causal_conv(ext, cw_ref, q))
    tail_ref[...] = ext[q:q + SUBLANES]
    convout_ref[0] = ext[q + SUBLANES - (CONV_WIDTH - 1):q + SUBLANES]

    sm = sm_ref[...]
    beta = jax.nn.sigmoid(sm)
    g = a_ref[...] * _softplus(sm + dtb_ref[...])
    gc = _dot_exact_rhs(l3_ref[...], g)
    beta_e = _dot(_split3_lanes(beta), e3b_ref[...])
    gc_e = _dot(_split3_lanes(gc), e3a_ref[...])
    row_e = _dot_exact_rhs(ones3_ref[...], gc_e * dmask_ref[...])
    decay_e = jnp.exp(jnp.minimum(gc_e - row_e, 0.0)) * incl_ref[...]
    egc_e = jnp.exp(gc_e)
    gc_last = gc_e[q - 1:q, :]
    eout_e = jnp.exp(gc_last - gc_e)
    ge_e = jnp.exp(gc_last)

    strict = strict_ref[...]
    eye = eye_ref[...]
    nw = nw_ref[...]
    gate = gate_ref[...]
    n_double = max(q.bit_length() - 2, 0)
    outs = []
    for h in range(GDN_HEADS):
        lo, hi = h * GDN_DIM, (h + 1) * GDN_DIM
        qh = act[:, lo:hi]
        kh = act[:, GDN_WIDTH + lo:GDN_WIDTH + hi]
        vh = act[:, 2 * GDN_WIDTH + lo:2 * GDN_WIDTH + hi]
        qh = qh * lax.rsqrt(jnp.sum(qh * qh, axis=-1, keepdims=True) + L2_EPS) * (GDN_DIM ** -0.5)
        kh = kh * lax.rsqrt(jnp.sum(kh * kh, axis=-1, keepdims=True) + L2_EPS)
        bh = beta_e[:, lo:hi]
        kb = kh * bh
        kh_bf = kh.astype(BF16)
        prod = _dot_nt(jnp.concatenate([kb, qh], axis=0).astype(BF16), kh_bf)
        dec = decay_e[:, lo:lo + q]
        x = -(prod[:q] * dec * strict)
        attn = prod[q:] * dec
        inv = eye + x
        ypow = x
        for _ in range(n_double):
            ypow = _dot_hi(ypow, ypow)
            inv = inv + _dot_hi(inv, ypow)
        rhs = jnp.concatenate([vh * bh, kb * egc_e[:, lo:hi]], axis=1)
        uw = _dot_hi(inv, rhs)
        u, wmat = uw[:, :GDN_DIM], uw[:, GDN_DIM:]
        s = s_ref[h]
        ws_qs = _dot(jnp.concatenate([wmat, qh * egc_e[:, lo:hi]], axis=0).astype(BF16), s.astype(BF16))
        v_new = u - ws_qs[:q]
        v_bf = v_new.astype(BF16)
        o = ws_qs[q:] + _dot(attn.astype(BF16), v_bf)
        k_out = (kh * eout_e[:, lo:hi]).astype(BF16)
        s_new = s * ge_e[:, lo:hi] + _dot_tn(k_out, v_bf)
        s_ref[h] = s_new
        outs.append(_rms(o, nw) * _silu(gate[:, lo:hi]))
    o_ref[...] = jnp.concatenate(outs, axis=1).astype(o_ref.dtype)

    @pl.when(c == nc - 1)
    def _():
        sout_ref[0] = s_ref[...]


def _gdn_call(qkv, gate, small, s0, conv0, cw, dtb, a_pad, nw, b, l, q):
    consts = _gdn_constants(q)
    nc = l // q
    t = b * l
    row = lambda i, c: (i * nc + c, 0)
    params = [cw, dtb, a_pad, nw] + [consts[k] for k in
                                     ("l3", "ones3", "e3b", "e3a", "dmask", "incl", "strict", "eye")]
    s_shape = (GDN_HEADS, GDN_DIM, GDN_DIM)
    return pl.pallas_call(
        functools.partial(_gdn_kernel, q=q),
        grid=(b, nc),
        in_specs=[pl.BlockSpec((q, GDN_CONV_DIM), row),
                  pl.BlockSpec((q, GDN_WIDTH), row),
                  pl.BlockSpec((q, LANES), row),
                  pl.BlockSpec((1,) + s_shape, lambda i, c: (i, 0, 0, 0)),
                  pl.BlockSpec((1, SUBLANES, GDN_CONV_DIM), lambda i, c: (i, 0, 0))]
                 + [_const_spec(p.shape) for p in params],
        out_specs=[pl.BlockSpec((q, GDN_WIDTH), row),
                   pl.BlockSpec((1,) + s_shape, lambda i, c: (i, 0, 0, 0)),
                   pl.BlockSpec((1, CONV_WIDTH - 1, GDN_CONV_DIM), lambda i, c: (i, 0, 0))],
        out_shape=[jax.ShapeDtypeStruct((t, GDN_WIDTH), F32),
                   jax.ShapeDtypeStruct((b,) + s_shape, F32),
                   jax.ShapeDtypeStruct((b, CONV_WIDTH - 1, GDN_CONV_DIM), F32)],
        scratch_shapes=[pltpu.VMEM(s_shape, F32),
                        pltpu.VMEM((SUBLANES, GDN_CONV_DIM), F32)],
        compiler_params=pltpu.CompilerParams(dimension_semantics=("arbitrary", "arbitrary"),
                                             vmem_limit_bytes=VMEM_LIMIT),
        name="gdn_scan",
    )(qkv, gate, small, s0, conv0, *params)


def _outffn_kernel(x_ref, y_ref, o_ref, g1_ref, sh2_ref, sc2_ref, g2_ref, nwm_ref, nwf_ref, nwp_ref,
                   wout_ref, wgu_ref, wdown_ref, out_ref, *, ff_chunk):
    x = x_ref[...]
    shape3 = x.shape
    tm = shape3[0] * shape3[1]
    m = (_dot(y_ref[...].astype(BF16), wout_ref[:SSD_WIDTH, :])
         + _dot(o_ref[...].astype(BF16), wout_ref[SSD_WIDTH:, :]))
    x1 = x + g1_ref[...] * _rms(m, nwm_ref[...]).reshape(shape3)
    h2 = _rms(x1, nwf_ref[...]) * (1.0 + sc2_ref[...]) + sh2_ref[...]
    h2 = h2.reshape(tm, shape3[2]).astype(BF16)
    f = jnp.zeros((tm, shape3[2]), F32)
    for j in range(D_FF // ff_chunk):
        gj = _dot(h2, wgu_ref[:, j * ff_chunk:(j + 1) * ff_chunk])
        uj = _dot(h2, wgu_ref[:, D_FF + j * ff_chunk:D_FF + (j + 1) * ff_chunk])
        f = f + _dot((_silu(gj) * uj).astype(BF16), wdown_ref[j * ff_chunk:(j + 1) * ff_chunk, :])
    out_ref[...] = x1 + g2_ref[...] * _rms(f, nwp_ref[...]).reshape(shape3)


def _outffn_call(x, y, o, mod3, nw_mix_post, nw_ffn_pre, nw_ffn_post, w_out, w_gu, w_down, bb, lb):
    b, l, d = x.shape
    nl = l // lb
    tm = bb * lb
    row = lambda i, j: (i * nl + j, 0)
    mod_spec = lambda k: pl.BlockSpec((bb, 1, d), lambda i, j: (i, 0, k))
    return pl.pallas_call(
        functools.partial(_outffn_kernel, ff_chunk=D_FF // 2),
        grid=(b // bb, nl),
        in_specs=[pl.BlockSpec((bb, lb, d), lambda i, j: (i, j, 0)),
                  pl.BlockSpec((tm, SSD_WIDTH), row),
                  pl.BlockSpec((tm, GDN_WIDTH), row),
                  mod_spec(2), mod_spec(3), mod_spec(4), mod_spec(5),
                  _const_spec((1, d)), _const_spec((1, d)), _const_spec((1, d)),
                  _const_spec(w_out.shape), _const_spec(w_gu.shape), _const_spec(w_down.shape)],
        out_specs=pl.BlockSpec((bb, lb, d), lambda i, j: (i, j, 0)),
        out_shape=jax.ShapeDtypeStruct((b, l, d), F32),
        compiler_params=pltpu.CompilerParams(dimension_semantics=("arbitrary", "arbitrary"),
                                             vmem_limit_bytes=VMEM_LIMIT),
        name="outproj_ffn",
    )(x, y, o, mod3, mod3, mod3, mod3, nw_mix_post, nw_ffn_pre, nw_ffn_post, w_out, w_gu, w_down)


def _pad_lanes(v, lane0):
    out = jnp.zeros((1, LANES), F32)
    return out.at[0, lane0:lane0 + v.shape[0]].set(v.astype(F32))


def _conv_hist(buf):
    return jnp.pad(buf, ((0, 0), (SUBLANES - (CONV_WIDTH - 1), 0), (0, 0)))


def _layer(x, mod, ssd_h0, ssd_conv0, gdn_s0, gdn_conv0, p, q, bb, lb):
    b, l, d = x.shape
    mod3 = mod.reshape(b, 1, 6 * d)
    z, xbc, qkv, gate, small = _inproj_call(x, mod3, p["norm_mix_pre"], p["w_big"], p["w_small"], bb, lb)
    y, ssd_h, ssd_conv = _ssd_call(
        xbc, z, small, ssd_h0.reshape(b, SSD_WIDTH, SSD_STATE), _conv_hist(ssd_conv0),
        p["ssd_conv_w"], p["ssd_conv_b"], p["ssd_dtb"], p["ssd_a"], p["ssd_d"], p["ssd_norm_w"], b, l, q)
    o, gdn_s, gdn_conv = _gdn_call(
        qkv, gate, small, gdn_s0, _conv_hist(gdn_conv0),
        p["gdn_conv_w"], p["gdn_dtb"], p["gdn_a"], p["gdn_norm_w"], b, l, q)
    out = _outffn_call(x, y, o, mod3, p["norm_mix_post"], p["norm_ffn_pre"], p["norm_ffn_post"],
                       p["w_out"], p["w_gu"], p["w_down"], bb, lb)
    return out, ssd_h.reshape(ssd_h0.shape), ssd_conv, gdn_s, gdn_conv


def _layer_params(l, w_ada, b_ada, norm_mix_pre, norm_mix_post, norm_ffn_pre, norm_ffn_post, w_in, ssd_conv_w,
                  ssd_conv_b, ssd_dt_bias, ssd_A_log, ssd_D, ssd_norm_w, gdn_conv_w, gdn_dt_bias, gdn_A_log,
                  gdn_norm_w, w_out, w_gate_up, w_down):
    w = w_in[l]
    o_z, o_xbc = 0, SSD_WIDTH
    o_dt = o_xbc + SSD_CONV_DIM
    o_qkv = o_dt + SSD_HEADS
    o_gate = o_qkv + GDN_CONV_DIM
    o_beta = o_gate + GDN_WIDTH
    o_alpha = o_beta + GDN_HEADS
    w_big = jnp.concatenate([w[:, o_z:o_dt], w[:, o_qkv:o_beta]], axis=1).astype(BF16)
    w_small = jnp.concatenate([w[:, o_dt:o_qkv], w[:, o_beta:o_alpha + GDN_HEADS],
                               jnp.zeros((w.shape[0], LANES - SSD_HEADS - 2 * GDN_HEADS), w.dtype)],
                              axis=1).astype(BF16)
    row = lambda v: v.reshape(1, -1).astype(F32)
    return dict(
        w_ada=w_ada[l], b_ada=b_ada[l],
        norm_mix_pre=row(norm_mix_pre[l]), norm_mix_post=row(norm_mix_post[l]),
        norm_ffn_pre=row(norm_ffn_pre[l]), norm_ffn_post=row(norm_ffn_post[l]),
        w_big=w_big, w_small=w_small,
        ssd_conv_w=ssd_conv_w[l], ssd_conv_b=row(ssd_conv_b[l]),
        ssd_dtb=_pad_lanes(ssd_dt_bias[l], DT_LANE0),
        ssd_a=_pad_lanes(-jnp.exp(ssd_A_log[l].astype(F32)), DT_LANE0),
        ssd_d=row(jnp.repeat(ssd_D[l], SSD_HEADDIM)),
        ssd_norm_w=row(ssd_norm_w[l]),
        gdn_conv_w=gdn_conv_w[l],
        gdn_dtb=_pad_lanes(gdn_dt_bias[l], ALPHA_LANE0),
        gdn_a=_pad_lanes(-jnp.exp(gdn_A_log[l].astype(F32)), ALPHA_LANE0),
        gdn_norm_w=row(gdn_norm_w[l]),
        w_out=w_out[l].astype(BF16), w_gu=w_gate_up[l].astype(BF16), w_down=w_down[l].astype(BF16),
    )


def kernel(x_prompt, x_sample, c_prompt, c_sample, state_ssd, state_ssd_conv, state_gdn, state_gdn_conv, w_ada, b_ada, norm_mix_pre, norm_mix_post, norm_ffn_pre, norm_ffn_post, w_in, ssd_conv_w, ssd_conv_b, ssd_dt_bias, ssd_A_log, ssd_D, ssd_norm_w, gdn_conv_w, gdn_dt_bias, gdn_A_log, gdn_norm_w, w_out, w_gate_up, w_down):
    depth = w_in.shape[0]
    bp, lp, _ = x_prompt.shape
    bs, ls, _ = x_sample.shape
    yp, ys = x_prompt, x_sample
    outs = [[] for _ in range(8)]
    for l in range(depth):
        p = _layer_params(l, w_ada, b_ada, norm_mix_pre, norm_mix_post, norm_ffn_pre, norm_ffn_post, w_in,
                          ssd_conv_w, ssd_conv_b, ssd_dt_bias, ssd_A_log, ssd_D, ssd_norm_w, gdn_conv_w,
                          gdn_dt_bias, gdn_A_log, gdn_norm_w, w_out, w_gate_up, w_down)
        mod = _mod_call(jnp.concatenate([c_prompt, c_sample], axis=0), p["w_ada"], p["b_ada"])
        zeros = lambda a: jnp.zeros((bp,) + a.shape[2:], a.dtype)
        yp, a0, a1, a2, a3 = _layer(yp, mod[:bp], zeros(state_ssd), zeros(state_ssd_conv), zeros(state_gdn),
                                    zeros(state_gdn_conv), p, q=min(CHUNK, lp), bb=1, lb=min(ROW_TILE, lp))
        ys, b0, b1, b2, b3 = _layer(ys, mod[bp:], state_ssd[l], state_ssd_conv[l], state_gdn[l],
                                    state_gdn_conv[l], p, q=min(CHUNK, ls), bb=min(ROW_TILE // ls, bs), lb=ls)
        for lst, v in zip(outs, (a0, a1, a2, a3, b0, b1, b2, b3)):
            lst.append(v)
    return (yp, ys) + tuple(jnp.stack(v) for v in outs)
```

```python
import functools

import numpy as np
import jax
import jax.numpy as jnp
from jax import lax
from jax.experimental import pallas as pl
from jax.experimental.pallas import tpu as pltpu

F32 = jnp.float32
BF16 = jnp.bfloat16
HIGHEST = lax.Precision.HIGHEST

D_MODEL = 1024
SSD_HEADS = 16
SSD_HEADDIM = 64
SSD_GROUPS = 2
SSD_STATE = 128
SSD_WIDTH = SSD_HEADS * SSD_HEADDIM
SSD_CONV_DIM = SSD_WIDTH + 2 * SSD_GROUPS * SSD_STATE
GDN_HEADS = 8
GDN_DIM = 128
GDN_WIDTH = GDN_HEADS * GDN_DIM
GDN_CONV_DIM = 3 * GDN_WIDTH
CONV_WIDTH = 4
CHUNK = 64
D_FF = 2816
NORM_EPS = 1e-6
L2_EPS = 1e-6
LANES = 128
SUBLANES = 8
BIG_COLS = SSD_WIDTH + SSD_CONV_DIM + GDN_CONV_DIM + GDN_WIDTH
DT_LANE0, BETA_LANE0, ALPHA_LANE0 = 0, SSD_HEADS, SSD_HEADS + GDN_HEADS
VMEM_LIMIT = 56 * 1024 * 1024
ROW_TILE = 256


def _dot(a, b):
    return jnp.dot(a, b, preferred_element_type=F32)


def _dot_hi(a, b):
    return jnp.dot(a, b, preferred_element_type=F32, precision=HIGHEST)


def _dot_nt(a, b):
    return lax.dot_general(a, b, (((1,), (1,)), ((), ())), preferred_element_type=F32)


def _dot_tn(a, b):
    return lax.dot_general(a, b, (((0,), (0,)), ((), ())), preferred_element_type=F32)


def _bf16_pieces(x):
    hi = x.astype(BF16)
    r1 = x - hi.astype(F32)
    mid = r1.astype(BF16)
    lo = (r1 - mid.astype(F32)).astype(BF16)
    return hi, mid, lo


def _split3_lanes(x):
    hi, mid, lo = _bf16_pieces(x)
    return jnp.concatenate([hi.astype(F32), mid.astype(F32), lo.astype(F32)], axis=1).astype(BF16)


def _dot_exact_rhs(a_bf16, x):
    hi, mid, lo = _bf16_pieces(x)
    return _dot(a_bf16, hi) + _dot(a_bf16, mid) + _dot(a_bf16, lo)


def _silu(x):
    return x * jax.nn.sigmoid(x)


def _softplus(x):
    return jnp.maximum(x, 0.0) + jnp.log1p(jnp.exp(-jnp.abs(x)))


def _rms(x, w):
    return x * lax.rsqrt(jnp.mean(x * x, axis=-1, keepdims=True) + NORM_EPS) * w


def _causal_conv(ext, w_ref, q):
    acc = ext[SUBLANES:SUBLANES + q] * w_ref[CONV_WIDTH - 1:CONV_WIDTH, :]
    for j in range(1, CONV_WIDTH):
        acc = acc + ext[SUBLANES - j:SUBLANES - j + q] * w_ref[CONV_WIDTH - 1 - j:CONV_WIDTH - j, :]
    return acc


def _mod_kernel(c_ref, w_ref, b_ref, o_ref):
    a = _silu(c_ref[...]).astype(BF16)
    o_ref[...] = _dot(a, w_ref[...].astype(BF16)) + b_ref[...]


def _mod_call(c_all, w_ada, b_ada):
    m, d = c_all.shape
    n = w_ada.shape[1]
    tn = 512
    return pl.pallas_call(
        _mod_kernel,
        grid=(n // tn,),
        in_specs=[pl.BlockSpec((m, d), lambda j: (0, 0)),
                  pl.BlockSpec((d, tn), lambda j: (0, j)),
                  pl.BlockSpec((1, tn), lambda j: (0, j))],
        out_specs=pl.BlockSpec((m, tn), lambda j: (0, j)),
        out_shape=jax.ShapeDtypeStruct((m, n), F32),
        compiler_params=pltpu.CompilerParams(dimension_semantics=("arbitrary",),
                                             vmem_limit_bytes=VMEM_LIMIT),
        name="adaln_mod",
    )(c_all, w_ada, b_ada.reshape(1, n))


def _inproj_kernel(x_ref, sh_ref, sc_ref, nw_ref, wbig_ref, wsm_ref,
                   z_ref, xbc_ref, qkv_ref, gate_ref, sm_ref):
    x = x_ref[...]
    h = _rms(x, nw_ref[...]) * (1.0 + sc_ref[...]) + sh_ref[...]
    h = h.reshape(x.shape[0] * x.shape[1], x.shape[2]).astype(BF16)
    c0 = 0
    for ref in (z_ref, xbc_ref, qkv_ref, gate_ref):
        w = ref.shape[1]
        ref[...] = _dot(h, wbig_ref[:, c0:c0 + w])
        c0 += w
    sm_ref[...] = _dot(h, wsm_ref[...])


def _const_spec(shape):
    nd = len(shape)
    return pl.BlockSpec(shape, lambda *_: (0,) * nd, pipeline_mode=pl.Buffered(1))


def _inproj_call(x, mod3, norm_w, w_big, w_small, bb, lb):
    b, l, d = x.shape
    nl = l // lb
    tm = bb * lb
    t = b * l
    row = lambda i, j: (i * nl + j, 0)
    widths = (SSD_WIDTH, SSD_CONV_DIM, GDN_CONV_DIM, GDN_WIDTH, LANES)
    return pl.pallas_call(
        _inproj_kernel,
        grid=(b // bb, nl),
        in_specs=[pl.BlockSpec((bb, lb, d), lambda i, j: (i, j, 0)),
                  pl.BlockSpec((bb, 1, d), lambda i, j: (i, 0, 0)),
                  pl.BlockSpec((bb, 1, d), lambda i, j: (i, 0, 1)),
                  _const_spec((1, d)),
                  _const_spec(w_big.shape),
                  _const_spec(w_small.shape)],
        out_specs=[pl.BlockSpec((tm, w), row) for w in widths],
        out_shape=[jax.ShapeDtypeStruct((t, w), F32) for w in widths],
        compiler_params=pltpu.CompilerParams(dimension_semantics=("arbitrary", "arbitrary"),
                                             vmem_limit_bytes=VMEM_LIMIT),
        name="inproj",
    )(x, mod3, mod3, norm_w, w_big, w_small)


def _ssd_constants(q):
    hq = SSD_HEADS * q
    hp = LANES // q
    w = hp * SSD_HEADDIM
    e_p = np.zeros((LANES, SSD_WIDTH), np.float32)
    e_q = np.zeros((LANES, hq), np.float32)
    for r in range(SSD_HEADS):
        e_p[DT_LANE0 + r, r * SSD_HEADDIM:(r + 1) * SSD_HEADDIM] = 1.0
        e_q[DT_LANE0 + r, r * q:(r + 1) * q] = 1.0
    tri = np.tril(np.ones((q, q), np.float32))
    gmask = np.zeros((hq, 2 * SSD_STATE), np.float32)
    half = hq // SSD_GROUPS
    gmask[:half, :SSD_STATE] = 1.0
    gmask[half:, SSD_STATE:] = 1.0
    xmask = np.kron(np.eye(hp, dtype=np.float32), np.ones((q, SSD_HEADDIM), np.float32))
    return dict(
        l3=jnp.asarray(tri, BF16),
        ones3=jnp.ones((q, q), BF16),
        e3p=jnp.asarray(np.tile(e_p, (3, 1)), BF16),
        e3q=jnp.asarray(np.tile(e_q, (3, 1)), BF16),
        dmask=jnp.asarray(np.tile(np.eye(q, dtype=np.float32), (1, SSD_HEADS))),
        causal=jnp.asarray(np.tile(tri, (1, SSD_HEADS))),
        gmask=jnp.asarray(gmask),
        xmask=jnp.asarray(xmask),
    ), hp, w


def _ssd_kernel(xbc_ref, z_ref, sm_ref, h0_ref, conv0_ref, cw_ref, cb_ref, dtb_ref, a_ref, d_ref, nw_ref,
                l3_ref, ones3_ref, e3p_ref, e3q_ref, dmask_ref, causal_ref, gmask_ref, xmask_ref,
                y_ref, hout_ref, convout_ref, ht_ref, tail_ref, *, q, hp, w):
    c = pl.program_id(1)
    nc = pl.num_programs(1)

    @pl.when(c == 0)
    def _():
        ht_ref[...] = h0_ref[0].T
        tail_ref[...] = conv0_ref[0]

    ext = jnp.concatenate([tail_ref[...], xbc_ref[...]], axis=0)
    act = _silu(_causal_conv(ext, cw_ref, q) + cb_ref[...])
    tail_ref[...] = ext[q:q + SUBLANES]
    convout_ref[0] = ext[q + SUBLANES - (CONV_WIDTH - 1):q + SUBLANES]
    xs = act[:, :SSD_WIDTH]
    bcat = act[:, SSD_WIDTH:SSD_WIDTH + 2 * SSD_STATE]
    ccat = act[:, SSD_WIDTH + 2 * SSD_STATE:]

    dt = _softplus(sm_ref[...] + dtb_ref[...])
    acs = _dot_exact_rhs(l3_ref[...], dt * a_ref[...])
    dec_end = jnp.exp(acs[q - 1:q, :] - acs)
    eacs = jnp.exp(acs)
    per_head = jnp.concatenate([dt, dt * dec_end, eacs], axis=0)
    exp3 = _dot(_split3_lanes(per_head), e3p_ref[...])
    dt_e, dtd_e, eacs_e = exp3[:q], exp3[q:2 * q], exp3[2 * q:]

    col_all = _dot(_split3_lanes(acs), e3q_ref[...])
    row_all = _dot_exact_rhs(ones3_ref[...], col_all * dmask_ref[...])
    decay = jnp.exp(jnp.minimum(col_all - row_all, 0.0)) * causal_ref[...]
    brep = (jnp.concatenate([bcat] * SSD_HEADS, axis=0) * gmask_ref[...]).astype(BF16)
    cb_all = _dot_nt(ccat.astype(BF16), brep)
    m_all = (cb_all * decay).astype(BF16)

    xdt = xs * dt_e
    y_blocks = []
    for j in range(SSD_WIDTH // w):
        xj = (jnp.concatenate([xdt[:, j * w:(j + 1) * w]] * hp, axis=0) * xmask_ref[...]).astype(BF16)
        y_blocks.append(_dot(m_all[:, j * LANES:(j + 1) * LANES], xj))
    y_diag = y_blocks[0] if len(y_blocks) == 1 else jnp.concatenate(y_blocks, axis=1)

    ht = ht_ref[...]
    ht_bf = ht.astype(BF16)
    xd = (xs * dtd_e).astype(BF16)
    gw = SSD_WIDTH // SSD_GROUPS
    y_off, upd = [], []
    for g in range(SSD_GROUPS):
        cg = ccat[:, g * SSD_STATE:(g + 1) * SSD_STATE].astype(BF16)
        bg = bcat[:, g * SSD_STATE:(g + 1) * SSD_STATE].astype(BF16)
        y_off.append(_dot(cg, ht_bf[:, g * gw:(g + 1) * gw]))
        upd.append(_dot_tn(bg, xd[:, g * gw:(g + 1) * gw]))
    y = y_diag + jnp.concatenate(y_off, axis=1) * eacs_e + d_ref[...] * xs
    ht_new = ht * eacs_e[q - 1:q, :] + jnp.concatenate(upd, axis=1)
    ht_ref[...] = ht_new

    yz = y * _silu(z_ref[...])
    nw = nw_ref[...]
    outs = [_rms(yz[:, g * gw:(g + 1) * gw], nw[:, g * gw:(g + 1) * gw]) for g in range(SSD_GROUPS)]
    y_ref[...] = jnp.concatenate(outs, axis=1).astype(y_ref.dtype)

    @pl.when(c == nc - 1)
    def _():
        hout_ref[0] = ht_new.T


def _ssd_call(xbc, z, small, h0, conv0, cw, cb, dtb, a_pad, d_e, nw, b, l, q):
    consts, hp, w = _ssd_constants(q)
    nc = l // q
    t = b * l
    row = lambda i, c: (i * nc + c, 0)
    params = [cw, cb, dtb, a_pad, d_e, nw] + [consts[k] for k in
                                              ("l3", "ones3", "e3p", "e3q", "dmask", "causal", "gmask", "xmask")]
    hp_shape = (SSD_WIDTH, SSD_STATE)
    return pl.pallas_call(
        functools.partial(_ssd_kernel, q=q, hp=hp, w=w),
        grid=(b, nc),
        in_specs=[pl.BlockSpec((q, SSD_CONV_DIM), row),
                  pl.BlockSpec((q, SSD_WIDTH), row),
                  pl.BlockSpec((q, LANES), row),
                  pl.BlockSpec((1,) + hp_shape, lambda i, c: (i, 0, 0)),
                  pl.BlockSpec((1, SUBLANES, SSD_CONV_DIM), lambda i, c: (i, 0, 0))]
                 + [_const_spec(p.shape) for p in params],
        out_specs=[pl.BlockSpec((q, SSD_WIDTH), row),
                   pl.BlockSpec((1,) + hp_shape, lambda i, c: (i, 0, 0)),
                   pl.BlockSpec((1, CONV_WIDTH - 1, SSD_CONV_DIM), lambda i, c: (i, 0, 0))],
        out_shape=[jax.ShapeDtypeStruct((t, SSD_WIDTH), F32),
                   jax.ShapeDtypeStruct((b,) + hp_shape, F32),
                   jax.ShapeDtypeStruct((b, CONV_WIDTH - 1, SSD_CONV_DIM), F32)],
        scratch_shapes=[pltpu.VMEM((SSD_STATE, SSD_WIDTH), F32),
                        pltpu.VMEM((SUBLANES, SSD_CONV_DIM), F32)],
        compiler_params=pltpu.CompilerParams(dimension_semantics=("arbitrary", "arbitrary"),
                                             vmem_limit_bytes=VMEM_LIMIT),
        name="ssd_scan",
    )(xbc, z, small, h0, conv0, *params)


MXU_DIM = 256


def _gdn_constants(q, nseq, hpb):
    assert nseq * hpb * q == MXU_DIM
    nhb = GDN_HEADS // hpb
    rows = nseq * q
    e_beta = np.zeros((LANES, GDN_WIDTH), np.float32)
    e_alpha = np.zeros((LANES, GDN_WIDTH), np.float32)
    e_c = np.zeros((LANES, nhb * MXU_DIM), np.float32)
    for h in range(GDN_HEADS):
        e_beta[BETA_LANE0 + h, h * GDN_DIM:(h + 1) * GDN_DIM] = 1.0
        e_alpha[ALPHA_LANE0 + h, h * GDN_DIM:(h + 1) * GDN_DIM] = 1.0
    jh, seq_c, hh_c, s_c = np.unravel_index(np.arange(nhb * MXU_DIM), (nhb, nseq, hpb, q))
    e_c[ALPHA_LANE0 + jh * hpb + hh_c, np.arange(nhb * MXU_DIM)] = 1.0
    seq_r, l_r = np.unravel_index(np.arange(rows), (nseq, q))
    same_seq = seq_r[:, None] == seq_c[None, :]
    dmask = same_seq & (l_r[:, None] == s_c[None, :])
    incl = same_seq & (l_r[:, None] >= s_c[None, :])
    strict = same_seq & (l_r[:, None] > s_c[None, :])
    seq_s, hh_s, _ = np.unravel_index(np.arange(MXU_DIM), (nseq, hpb, q))
    bdmask = (seq_s[:, None] == seq_c[None, :MXU_DIM]) & (hh_s[:, None] == hh_c[None, :MXU_DIM])
    kmask = hh_s[:, None] == (np.arange(hpb * GDN_DIM) // GDN_DIM)[None, :]
    f = lambda m: jnp.asarray(m.astype(np.float32))
    return dict(
        l3=jnp.asarray(np.kron(np.eye(nseq), np.tril(np.ones((q, q)))), BF16),
        ones3=jnp.ones((rows, rows), BF16),
        e3b=jnp.asarray(np.tile(e_beta, (3, 1)), BF16),
        e3a=jnp.asarray(np.tile(e_alpha, (3, 1)), BF16),
        e3c=jnp.asarray(np.tile(e_c, (3, 1)), BF16),
        dmask=f(dmask), incl=f(incl), strict=f(strict), eye=f(dmask[:, :MXU_DIM]),
        bdmask=f(bdmask), kmask=f(kmask),
    )


def _gdn_kernel(qkv_ref, gate_ref, sm_ref, s0_ref, conv0_ref, cw_ref, dtb_ref, a_ref, nw_ref,
                l3_ref, ones3_ref, e3b_ref, e3a_ref, e3c_ref, dmask_ref, incl_ref, strict_ref, eye_ref,
                bdmask_ref, kmask_ref,
                o_ref, sout_ref, convout_ref, s_ref, tail_ref, *, q, nseq, hpb):
    c = pl.program_id(1)
    nc = pl.num_programs(1)
    nhb = GDN_HEADS // hpb
    rows = nseq * q
    bw = hpb * GDN_DIM

    @pl.when(c == 0)
    def _():
        s_ref[...] = s0_ref[...]
        tail_ref[...] = conv0_ref[...]

    xin = qkv_ref[...]
    acts = []
    for sq in range(nseq):
        ext = jnp.concatenate([tail_ref[sq], xin[sq * q:(sq + 1) * q]], axis=0)
        acts.append(_silu(_causal_conv(ext, cw_ref, q)))
        tail_ref[sq] = ext[q:q + SUBLANES]
        convout_ref[sq] = ext[q + SUBLANES - (CONV_WIDTH - 1):q + SUBLANES]
    act = acts[0] if nseq == 1 else jnp.concatenate(acts, axis=0)

    def l2n(x):
        parts = []
        for h in range(GDN_HEADS):
            xh = x[:, h * GDN_DIM:(h + 1) * GDN_DIM]
            parts.append(xh * lax.rsqrt(jnp.sum(xh * xh, axis=-1, keepdims=True) + L2_EPS))
        return jnp.concatenate(parts, axis=1)

    qn = l2n(act[:, :GDN_WIDTH]) * (GDN_DIM ** -0.5)
    kn = l2n(act[:, GDN_WIDTH:2 * GDN_WIDTH])
    v = act[:, 2 * GDN_WIDTH:]

    sm = sm_ref[...]
    beta = jax.nn.sigmoid(sm)
    g = a_ref[...] * _softplus(sm + dtb_ref[...])
    gc = _dot_exact_rhs(l3_ref[...], g)
    gc3 = _split3_lanes(gc)
    beta_e = _dot(_split3_lanes(beta), e3b_ref[...])
    gc_e = _dot(gc3, e3a_ref[...])
    col_c = _dot(gc3, e3c_ref[...])
    row_c = _dot_exact_rhs(ones3_ref[...], col_c * dmask_ref[...])
    decay_c = jnp.exp(jnp.minimum(col_c - row_c, 0.0)) * incl_ref[...]
    egc_e = jnp.exp(gc_e)
    lasts = [gc_e[(sq + 1) * q - 1:(sq + 1) * q, :] for sq in range(nseq)]
    eouts = [jnp.exp(lasts[sq] - gc_e[sq * q:(sq + 1) * q]) for sq in range(nseq)]
    eout_e = eouts[0] if nseq == 1 else jnp.concatenate(eouts, axis=0)
    kb = kn * beta_e
    vb = v * beta_e
    kbg = kb * egc_e
    qin = qn * egc_e
    kout = (kn * eout_e).astype(BF16)

    bdmask = bdmask_ref[...]
    kmask = kmask_ref[...]

    def stack_rows(x):
        return jnp.concatenate([x[sq * q:(sq + 1) * q] for sq in range(nseq) for _ in range(hpb)], axis=0)

    def block_diag_pieces(m):
        hi = m.astype(BF16).astype(F32)
        return (stack_rows(hi) * bdmask).astype(BF16), (stack_rows(m - hi) * bdmask).astype(BF16)

    def lhs_pieces(m):
        hi = m.astype(BF16)
        return hi, (m - hi.astype(F32)).astype(BF16)

    def head_lanes(h):
        return slice(h * GDN_DIM, (h + 1) * GDN_DIM)

    def seq_rows(sq):
        return slice(sq * q, (sq + 1) * q)

    n_sq = q.bit_length() - 1
    assert q == 1 << n_sq and n_sq >= 2
    o_blocks = []
    for jb in range(nhb):
        nl = slice(jb * bw, (jb + 1) * bw)
        cl = slice(jb * MXU_DIM, (jb + 1) * MXU_DIM)
        k_bd = (stack_rows(kn[:, nl]) * kmask).astype(BF16)
        prod = _dot_nt(jnp.concatenate([kb[:, nl], qn[:, nl]], axis=0).astype(BF16), k_bd)
        dec = decay_c[:, cl]
        x = -(prod[:rows] * dec * strict_ref[:, cl])
        attn_c = prod[rows:] * dec
        inv = eye_ref[...] + x
        ypow = x
        for k in range(n_sq):
            first, last = k == 0, k == n_sq - 1
            b_hi, b_lo = block_diag_pieces(ypow)
            lhs = ypow if first else (inv if last else jnp.concatenate([inv, ypow], axis=0))
            l_hi, l_lo = lhs_pieces(lhs)
            res = _dot(jnp.concatenate([l_hi, l_hi, l_lo], axis=1), jnp.concatenate([b_hi, b_lo, b_hi], axis=0))
            if first:
                ypow = res
            elif last:
                inv = inv + res
            else:
                inv, ypow = inv + res[:rows], res[rows:]
        i_hi, i_lo = block_diag_pieces(inv)
        segs = [(sq, jb * hpb + hh) for sq in range(nseq) for hh in range(hpb)]
        rhs = jnp.concatenate([jnp.concatenate([vb[seq_rows(sq), head_lanes(h)], kbg[seq_rows(sq), head_lanes(h)]],
                                               axis=1) for sq, h in segs], axis=0).astype(BF16)
        uw = _dot(jnp.concatenate([i_hi, i_lo], axis=1), jnp.concatenate([rhs, rhs], axis=0))
        ws, qs, states = [], [], []
        for i, (sq, h) in enumerate(segs):
            s = s_ref[sq, h]
            states.append(s)
            wq = jnp.concatenate([uw[i * q:(i + 1) * q, GDN_DIM:], qin[seq_rows(sq), head_lanes(h)]], axis=0)
            r = _dot(wq.astype(BF16), s.astype(BF16))
            ws.append(r[:q])
            qs.append(r[q:])
        v_new = uw[:, :GDN_DIM] - jnp.concatenate(ws, axis=0)
        v_rep = (jnp.concatenate([v_new] * hpb, axis=1) * kmask).astype(BF16)
        qs_rows = [jnp.concatenate(qs[sq * hpb:(sq + 1) * hpb], axis=1) for sq in range(nseq)]
        qs_nat = qs_rows[0] if nseq == 1 else jnp.concatenate(qs_rows, axis=0)
        o_blocks.append(qs_nat + _dot(attn_c.astype(BF16), v_rep))
        v_bf = v_new.astype(BF16)
        for i, (sq, h) in enumerate(segs):
            ge = jnp.exp(lasts[sq][:, head_lanes(h)])
            s_ref[sq, h] = states[i] * ge + _dot_tn(kout[seq_rows(sq), head_lanes(h)], v_bf[i * q:(i + 1) * q])

    o = o_blocks[0] if nhb == 1 else jnp.concatenate(o_blocks, axis=1)
    gate = gate_ref[...]
    nw = nw_ref[...]
    outs = [_rms(o[:, head_lanes(h)], nw) * _silu(gate[:, head_lanes(h)]) for h in range(GDN_HEADS)]
    o_ref[...] = jnp.concatenate(outs, axis=1).astype(o_ref.dtype)

    @pl.when(c == nc - 1)
    def _():
        sout_ref[...] = s_ref[...]


def _gdn_call(qkv, gate, small, s0, conv0, cw, dtb, a_pad, nw, b, l, q):
    nseq = max(1, MXU_DIM // (GDN_HEADS * q))
    hpb = MXU_DIM // (nseq * q)
    consts = _gdn_constants(q, nseq, hpb)
    nc = l // q
    assert b % nseq == 0 and (nseq == 1 or nc == 1)
    t = b * l
    rows = nseq * q
    row = lambda i, c: (i * nc + c, 0)
    params = [cw, dtb, a_pad, nw] + [consts[k] for k in
                                     ("l3", "ones3", "e3b", "e3a", "e3c", "dmask", "incl", "strict", "eye",
                                      "bdmask", "kmask")]
    s_shape = (GDN_HEADS, GDN_DIM, GDN_DIM)
    return pl.pallas_call(
        functools.partial(_gdn_kernel, q=q, nseq=nseq, hpb=hpb),
        grid=(b // nseq, nc),
        in_specs=[pl.BlockSpec((rows, GDN_CONV_DIM), row),
                  pl.BlockSpec((rows, GDN_WIDTH), row),
                  pl.BlockSpec((rows, LANES), row),
                  pl.BlockSpec((nseq,) + s_shape, lambda i, c: (i, 0, 0, 0)),
                  pl.BlockSpec((nseq, SUBLANES, GDN_CONV_DIM), lambda i, c: (i, 0, 0))]
                 + [_const_spec(p.shape) for p in params],
        out_specs=[pl.BlockSpec((rows, GDN_WIDTH), row),
                   pl.BlockSpec((nseq,) + s_shape, lambda i, c: (i, 0, 0, 0)),
                   pl.BlockSpec((nseq, CONV_WIDTH - 1, GDN_CONV_DIM), lambda i, c: (i, 0, 0))],
        out_shape=[jax.ShapeDtypeStruct((t, GDN_WIDTH), F32),
                   jax.ShapeDtypeStruct((b,) + s_shape, F32),
                   jax.ShapeDtypeStruct((b, CONV_WIDTH - 1, GDN_CONV_DIM), F32)],
        scratch_shapes=[pltpu.VMEM((nseq,) + s_shape, F32),
                        pltpu.VMEM((nseq, SUBLANES, GDN_CONV_DIM), F32)],
        compiler_params=pltpu.CompilerParams(dimension_semantics=("arbitrary", "arbitrary"),
                                             vmem_limit_bytes=VMEM_LIMIT),
        name="gdn_scan",
    )(qkv, gate, small, s0, conv0, *params)


def _outffn_kernel(x_ref, y_ref, o_ref, g1_ref, sh2_ref, sc2_ref, g2_ref, nwm_ref, nwf_ref, nwp_ref,
                   wout_ref, wgu_ref, wdown_ref, out_ref, *, ff_chunk):
    x = x_ref[...]
    shape3 = x.shape
    tm = shape3[0] * shape3[1]
    m = (_dot(y_ref[...].astype(BF16), wout_ref[:SSD_WIDTH, :])
         + _dot(o_ref[...].astype(BF16), wout_ref[SSD_WIDTH:, :]))
    x1 = x + g1_ref[...] * _rms(m, nwm_ref[...]).reshape(shape3)
    h2 = _rms(x1, nwf_ref[...]) * (1.0 + sc2_ref[...]) + sh2_ref[...]
    h2 = h2.reshape(tm, shape3[2]).astype(BF16)
    f = jnp.zeros((tm, shape3[2]), F32)
    for j in range(D_FF // ff_chunk):
        gj = _dot(h2, wgu_ref[:, j * ff_chunk:(j + 1) * ff_chunk])
        uj = _dot(h2, wgu_ref[:, D_FF + j * ff_chunk:D_FF + (j + 1) * ff_chunk])
        f = f + _dot((_silu(gj) * uj).astype(BF16), wdown_ref[j * ff_chunk:(j + 1) * ff_chunk, :])
    out_ref[...] = x1 + g2_ref[...] * _rms(f, nwp_ref[...]).reshape(shape3)


def _outffn_call(x, y, o, mod3, nw_mix_post, nw_ffn_pre, nw_ffn_post, w_out, w_gu, w_down, bb, lb):
    b, l, d = x.shape
    nl = l // lb
    tm = bb * lb
    row = lambda i, j: (i * nl + j, 0)
    mod_spec = lambda k: pl.BlockSpec((bb, 1, d), lambda i, j: (i, 0, k))
    return pl.pallas_call(
        functools.partial(_outffn_kernel, ff_chunk=D_FF // 2),
        grid=(b // bb, nl),
        in_specs=[pl.BlockSpec((bb, lb, d), lambda i, j: (i, j, 0)),
                  pl.BlockSpec((tm, SSD_WIDTH), row),
                  pl.BlockSpec((tm, GDN_WIDTH), row),
                  mod_spec(2), mod_spec(3), mod_spec(4), mod_spec(5),
                  _const_spec((1, d)), _const_spec((1, d)), _const_spec((1, d)),
                  _const_spec(w_out.shape), _const_spec(w_gu.shape), _const_spec(w_down.shape)],
        out_specs=pl.BlockSpec((bb, lb, d), lambda i, j: (i, j, 0)),
        out_shape=jax.ShapeDtypeStruct((b, l, d), F32),
        compiler_params=pltpu.CompilerParams(dimension_semantics=("arbitrary", "arbitrary"),
                                             vmem_limit_bytes=VMEM_LIMIT),
        name="outproj_ffn",
    )(x, y, o, mod3, mod3, mod3, mod3, nw_mix_post, nw_ffn_pre, nw_ffn_post, w_out, w_gu, w_down)


def _pad_lanes(v, lane0):
    out = jnp.zeros((1, LANES), F32)
    return out.at[0, lane0:lane0 + v.shape[0]].set(v.astype(F32))


def _conv_hist(buf):
    return jnp.pad(buf, ((0, 0), (SUBLANES - (CONV_WIDTH - 1), 0), (0, 0)))


def _layer(x, mod, ssd_h0, ssd_conv0, gdn_s0, gdn_conv0, p, q, bb, lb):
    b, l, d = x.shape
    mod3 = mod.reshape(b, 1, 6 * d)
    z, xbc, qkv, gate, small = _inproj_call(x, mod3, p["norm_mix_pre"], p["w_big"], p["w_small"], bb, lb)
    y, ssd_h, ssd_conv = _ssd_call(
        xbc, z, small, ssd_h0.reshape(b, SSD_WIDTH, SSD_STATE), _conv_hist(ssd_conv0),
        p["ssd_conv_w"], p["ssd_conv_b"], p["ssd_dtb"], p["ssd_a"], p["ssd_d"], p["ssd_norm_w"], b, l, q)
    o, gdn_s, gdn_conv = _gdn_call(
        qkv, gate, small, gdn_s0, _conv_hist(gdn_conv0),
        p["gdn_conv_w"], p["gdn_dtb"], p["gdn_a"], p["gdn_norm_w"], b, l, q)
    out = _outffn_call(x, y, o, mod3, p["norm_mix_post"], p["norm_ffn_pre"], p["norm_ffn_post"],
                       p["w_out"], p["w_gu"], p["w_down"], bb, lb)
    return out, ssd_h.reshape(ssd_h0.shape), ssd_conv, gdn_s, gdn_conv


def _layer_params(l, w_ada, b_ada, norm_mix_pre, norm_mix_post, norm_ffn_pre, norm_ffn_post, w_in, ssd_conv_w,
                  ssd_conv_b, ssd_dt_bias, ssd_A_log, ssd_D, ssd_norm_w, gdn_conv_w, gdn_dt_bias, gdn_A_log,
                  gdn_norm_w, w_out, w_gate_up, w_down):
    w = w_in[l]
    o_z, o_xbc = 0, SSD_WIDTH
    o_dt = o_xbc + SSD_CONV_DIM
    o_qkv = o_dt + SSD_HEADS
    o_gate = o_qkv + GDN_CONV_DIM
    o_beta = o_gate + GDN_WIDTH
    o_alpha = o_beta + GDN_HEADS
    w_big = jnp.concatenate([w[:, o_z:o_dt], w[:, o_qkv:o_beta]], axis=1).astype(BF16)
    w_small = jnp.concatenate([w[:, o_dt:o_qkv], w[:, o_beta:o_alpha + GDN_HEADS],
                               jnp.zeros((w.shape[0], LANES - SSD_HEADS - 2 * GDN_HEADS), w.dtype)],
                              axis=1).astype(BF16)
    row = lambda v: v.reshape(1, -1).astype(F32)
    return dict(
        w_ada=w_ada[l], b_ada=b_ada[l],
        norm_mix_pre=row(norm_mix_pre[l]), norm_mix_post=row(norm_mix_post[l]),
        norm_ffn_pre=row(norm_ffn_pre[l]), norm_ffn_post=row(norm_ffn_post[l]),
        w_big=w_big, w_small=w_small,
        ssd_conv_w=ssd_conv_w[l], ssd_conv_b=row(ssd_conv_b[l]),
        ssd_dtb=_pad_lanes(ssd_dt_bias[l], DT_LANE0),
        ssd_a=_pad_lanes(-jnp.exp(ssd_A_log[l].astype(F32)), DT_LANE0),
        ssd_d=row(jnp.repeat(ssd_D[l], SSD_HEADDIM)),
        ssd_norm_w=row(ssd_norm_w[l]),
        gdn_conv_w=gdn_conv_w[l],
        gdn_dtb=_pad_lanes(gdn_dt_bias[l], ALPHA_LANE0),
        gdn_a=_pad_lanes(-jnp.exp(gdn_A_log[l].astype(F32)), ALPHA_LANE0),
        gdn_norm_w=row(gdn_norm_w[l]),
        w_out=w_out[l].astype(BF16), w_gu=w_gate_up[l].astype(BF16), w_down=w_down[l].astype(BF16),
    )


def kernel(x_prompt, x_sample, c_prompt, c_sample, state_ssd, state_ssd_conv, state_gdn, state_gdn_conv, w_ada, b_ada, norm_mix_pre, norm_mix_post, norm_ffn_pre, norm_ffn_post, w_in, ssd_conv_w, ssd_conv_b, ssd_dt_bias, ssd_A_log, ssd_D, ssd_norm_w, gdn_conv_w, gdn_dt_bias, gdn_A_log, gdn_norm_w, w_out, w_gate_up, w_down):
    depth = w_in.shape[0]
    bp, lp, _ = x_prompt.shape
    bs, ls, _ = x_sample.shape
    yp, ys = x_prompt, x_sample
    outs = [[] for _ in range(8)]
    for l in range(depth):
        p = _layer_params(l, w_ada, b_ada, norm_mix_pre, norm_mix_post, norm_ffn_pre, norm_ffn_post, w_in,
                          ssd_conv_w, ssd_conv_b, ssd_dt_bias, ssd_A_log, ssd_D, ssd_norm_w, gdn_conv_w,
                          gdn_dt_bias, gdn_A_log, gdn_norm_w, w_out, w_gate_up, w_down)
        mod = _mod_call(jnp.concatenate([c_prompt, c_sample], axis=0), p["w_ada"], p["b_ada"])
        zeros = lambda a: jnp.zeros((bp,) + a.shape[2:], a.dtype)
        yp, a0, a1, a2, a3 = _layer(yp, mod[:bp], zeros(state_ssd), zeros(state_ssd_conv), zeros(state_gdn),
                                    zeros(state_gdn_conv), p, q=min(CHUNK, lp), bb=1, lb=min(ROW_TILE, lp))
        ys, b0, b1, b2, b3 = _layer(ys, mod[bp:], state_ssd[l], state_ssd_conv[l], state_gdn[l],
                                    state_gdn_conv[l], p, q=min(CHUNK, ls), bb=min(ROW_TILE // ls, bs), lb=ls)
        for lst, v in zip(outs, (a0, a1, a2, a3, b0, b1, b2, b3)):
            lst.append(v)
    return (yp, ys) + tuple(jnp.stack(v) for v in outs)
```

```python
import functools

import numpy as np
import jax
import jax.numpy as jnp
from jax import lax
from jax.experimental import pallas as pl
from jax.experimental.pallas import tpu as pltpu

F32 = jnp.float32
BF16 = jnp.bfloat16
HIGHEST = lax.Precision.HIGHEST

D_MODEL = 1024
SSD_HEADS = 16
SSD_HEADDIM = 64
SSD_GROUPS = 2
SSD_STATE = 128
SSD_WIDTH = SSD_HEADS * SSD_HEADDIM
SSD_CONV_DIM = SSD_WIDTH + 2 * SSD_GROUPS * SSD_STATE
GDN_HEADS = 8
GDN_DIM = 128
GDN_WIDTH = GDN_HEADS * GDN_DIM
GDN_CONV_DIM = 3 * GDN_WIDTH
CONV_WIDTH = 4
CHUNK = 64
D_FF = 2816
NORM_EPS = 1e-6
L2_EPS = 1e-6
LANES = 128
SUBLANES = 8
BIG_COLS = SSD_WIDTH + SSD_CONV_DIM + GDN_CONV_DIM + GDN_WIDTH
DT_LANE0, BETA_LANE0, ALPHA_LANE0 = 0, SSD_HEADS, SSD_HEADS + GDN_HEADS
VMEM_LIMIT = 56 * 1024 * 1024
ROW_TILE = 256
GDN_CHUNKS_PER_STEP = 4


def _dot(a, b):
    return jnp.dot(a, b, preferred_element_type=F32)


def _dot_hi(a, b):
    return jnp.dot(a, b, preferred_element_type=F32, precision=HIGHEST)


def _dot_nt(a, b):
    return lax.dot_general(a, b, (((1,), (1,)), ((), ())), preferred_element_type=F32)


def _dot_tn(a, b):
    return lax.dot_general(a, b, (((0,), (0,)), ((), ())), preferred_element_type=F32)


def _bf16_pieces(x):
    hi = x.astype(BF16)
    r1 = x - hi.astype(F32)
    mid = r1.astype(BF16)
    lo = (r1 - mid.astype(F32)).astype(BF16)
    return hi, mid, lo


def _split3_lanes(x):
    hi, mid, lo = _bf16_pieces(x)
    return jnp.concatenate([hi.astype(F32), mid.astype(F32), lo.astype(F32)], axis=1).astype(BF16)


def _dot_exact_rhs(a_bf16, x):
    hi, mid, lo = _bf16_pieces(x)
    return _dot(a_bf16, hi) + _dot(a_bf16, mid) + _dot(a_bf16, lo)


def _silu(x):
    return x * jax.nn.sigmoid(x)


def _softplus(x):
    return jnp.maximum(x, 0.0) + jnp.log1p(jnp.exp(-jnp.abs(x)))


def _rms(x, w):
    return x * lax.rsqrt(jnp.mean(x * x, axis=-1, keepdims=True) + NORM_EPS) * w


def _causal_conv(ext, w_ref, q):
    acc = ext[SUBLANES:SUBLANES + q] * w_ref[CONV_WIDTH - 1:CONV_WIDTH, :]
    for j in range(1, CONV_WIDTH):
        acc = acc + ext[SUBLANES - j:SUBLANES - j + q] * w_ref[CONV_WIDTH - 1 - j:CONV_WIDTH - j, :]
    return acc


def _mod_kernel(c_ref, w_ref, b_ref, o_ref):
    a = _silu(c_ref[...]).astype(BF16)
    o_ref[...] = _dot(a, w_ref[...].astype(BF16)) + b_ref[...]


def _mod_call(c_all, w_ada, b_ada):
    m, d = c_all.shape
    n = w_ada.shape[1]
    tn = 512
    return pl.pallas_call(
        _mod_kernel,
        grid=(n // tn,),
        in_specs=[pl.BlockSpec((m, d), lambda j: (0, 0)),
                  pl.BlockSpec((d, tn), lambda j: (0, j)),
                  pl.BlockSpec((1, tn), lambda j: (0, j))],
        out_specs=pl.BlockSpec((m, tn), lambda j: (0, j)),
        out_shape=jax.ShapeDtypeStruct((m, n), F32),
        compiler_params=pltpu.CompilerParams(dimension_semantics=("arbitrary",),
                                             vmem_limit_bytes=VMEM_LIMIT),
        name="adaln_mod",
    )(c_all, w_ada, b_ada.reshape(1, n))


def _inproj_kernel(x_ref, sh_ref, sc_ref, nw_ref, wbig_ref, wsm_ref,
                   z_ref, xbc_ref, qkv_ref, gate_ref, sm_ref):
    x = x_ref[...]
    h = _rms(x, nw_ref[...]) * (1.0 + sc_ref[...]) + sh_ref[...]
    h = h.reshape(x.shape[0] * x.shape[1], x.shape[2]).astype(BF16)
    c0 = 0
    for ref in (z_ref, xbc_ref, qkv_ref, gate_ref):
        w = ref.shape[1]
        ref[...] = _dot(h, wbig_ref[:, c0:c0 + w])
        c0 += w
    sm_ref[...] = _dot(h, wsm_ref[...])


def _const_spec(shape):
    nd = len(shape)
    return pl.BlockSpec(shape, lambda *_: (0,) * nd, pipeline_mode=pl.Buffered(1))


def _inproj_call(x, mod3, norm_w, w_big, w_small, bb, lb):
    b, l, d = x.shape
    nl = l // lb
    tm = bb * lb
    t = b * l
    row = lambda i, j: (i * nl + j, 0)
    widths = (SSD_WIDTH, SSD_CONV_DIM, GDN_CONV_DIM, GDN_WIDTH, LANES)
    return pl.pallas_call(
        _inproj_kernel,
        grid=(b // bb, nl),
        in_specs=[pl.BlockSpec((bb, lb, d), lambda i, j: (i, j, 0)),
                  pl.BlockSpec((bb, 1, d), lambda i, j: (i, 0, 0)),
                  pl.BlockSpec((bb, 1, d), lambda i, j: (i, 0, 1)),
                  _const_spec((1, d)),
                  _const_spec(w_big.shape),
                  _const_spec(w_small.shape)],
        out_specs=[pl.BlockSpec((tm, w), row) for w in widths],
        out_shape=[jax.ShapeDtypeStruct((t, w), F32) for w in widths],
        compiler_params=pltpu.CompilerParams(dimension_semantics=("arbitrary", "arbitrary"),
                                             vmem_limit_bytes=VMEM_LIMIT),
        name="inproj",
    )(x, mod3, mod3, norm_w, w_big, w_small)


def _ssd_constants(q):
    hq = SSD_HEADS * q
    hp = LANES // q
    w = hp * SSD_HEADDIM
    e_p = np.zeros((LANES, SSD_WIDTH), np.float32)
    e_q = np.zeros((LANES, hq), np.float32)
    for r in range(SSD_HEADS):
        e_p[DT_LANE0 + r, r * SSD_HEADDIM:(r + 1) * SSD_HEADDIM] = 1.0
        e_q[DT_LANE0 + r, r * q:(r + 1) * q] = 1.0
    tri = np.tril(np.ones((q, q), np.float32))
    gmask = np.zeros((hq, 2 * SSD_STATE), np.float32)
    half = hq // SSD_GROUPS
    gmask[:half, :SSD_STATE] = 1.0
    gmask[half:, SSD_STATE:] = 1.0
    xmask = np.kron(np.eye(hp, dtype=np.float32), np.ones((q, SSD_HEADDIM), np.float32))
    return dict(
        l3=jnp.asarray(tri, BF16),
        ones3=jnp.ones((q, q), BF16),
        e3p=jnp.asarray(np.tile(e_p, (3, 1)), BF16),
        e3q=jnp.asarray(np.tile(e_q, (3, 1)), BF16),
        dmask=jnp.asarray(np.tile(np.eye(q, dtype=np.float32), (1, SSD_HEADS))),
        causal=jnp.asarray(np.tile(tri, (1, SSD_HEADS))),
        gmask=jnp.asarray(gmask),
        xmask=jnp.asarray(xmask),
    ), hp, w


def _ssd_kernel(xbc_ref, z_ref, sm_ref, h0_ref, conv0_ref, cw_ref, cb_ref, dtb_ref, a_ref, d_ref, nw_ref,
                l3_ref, ones3_ref, e3p_ref, e3q_ref, dmask_ref, causal_ref, gmask_ref, xmask_ref,
                y_ref, hout_ref, convout_ref, ht_ref, tail_ref, *, q, hp, w):
    c = pl.program_id(1)
    nc = pl.num_programs(1)

    @pl.when(c == 0)
    def _():
        ht_ref[...] = h0_ref[0].T
        tail_ref[...] = conv0_ref[0]

    ext = jnp.concatenate([tail_ref[...], xbc_ref[...]], axis=0)
    act = _silu(_causal_conv(ext, cw_ref, q) + cb_ref[...])
    tail_ref[...] = ext[q:q + SUBLANES]
    convout_ref[0] = ext[q + SUBLANES - (CONV_WIDTH - 1):q + SUBLANES]
    xs = act[:, :SSD_WIDTH]
    bcat = act[:, SSD_WIDTH:SSD_WIDTH + 2 * SSD_STATE]
    ccat = act[:, SSD_WIDTH + 2 * SSD_STATE:]

    dt = _softplus(sm_ref[...] + dtb_ref[...])
    acs = _dot_exact_rhs(l3_ref[...], dt * a_ref[...])
    dec_end = jnp.exp(acs[q - 1:q, :] - acs)
    eacs = jnp.exp(acs)
    per_head = jnp.concatenate([dt, dt * dec_end, eacs], axis=0)
    exp3 = _dot(_split3_lanes(per_head), e3p_ref[...])
    dt_e, dtd_e, eacs_e = exp3[:q], exp3[q:2 * q], exp3[2 * q:]

    col_all = _dot(_split3_lanes(acs), e3q_ref[...])
    row_all = _dot_exact_rhs(ones3_ref[...], col_all * dmask_ref[...])
    decay = jnp.exp(jnp.minimum(col_all - row_all, 0.0)) * causal_ref[...]
    brep = (jnp.concatenate([bcat] * SSD_HEADS, axis=0) * gmask_ref[...]).astype(BF16)
    cb_all = _dot_nt(ccat.astype(BF16), brep)
    m_all = (cb_all * decay).astype(BF16)

    xdt = xs * dt_e
    y_blocks = []
    for j in range(SSD_WIDTH // w):
        xj = (jnp.concatenate([xdt[:, j * w:(j + 1) * w]] * hp, axis=0) * xmask_ref[...]).astype(BF16)
        y_blocks.append(_dot(m_all[:, j * LANES:(j + 1) * LANES], xj))
    y_diag = y_blocks[0] if len(y_blocks) == 1 else jnp.concatenate(y_blocks, axis=1)

    ht = ht_ref[...]
    ht_bf = ht.astype(BF16)
    xd = (xs * dtd_e).astype(BF16)
    gw = SSD_WIDTH // SSD_GROUPS
    y_off, upd = [], []
    for g in range(SSD_GROUPS):
        cg = ccat[:, g * SSD_STATE:(g + 1) * SSD_STATE].astype(BF16)
        bg = bcat[:, g * SSD_STATE:(g + 1) * SSD_STATE].astype(BF16)
        y_off.append(_dot(cg, ht_bf[:, g * gw:(g + 1) * gw]))
        upd.append(_dot_tn(bg, xd[:, g * gw:(g + 1) * gw]))
    y = y_diag + jnp.concatenate(y_off, axis=1) * eacs_e + d_ref[...] * xs
    ht_new = ht * eacs_e[q - 1:q, :] + jnp.concatenate(upd, axis=1)
    ht_ref[...] = ht_new

    yz = y * _silu(z_ref[...])
    nw = nw_ref[...]
    outs = [_rms(yz[:, g * gw:(g + 1) * gw], nw[:, g * gw:(g + 1) * gw]) for g in range(SSD_GROUPS)]
    y_ref[...] = jnp.concatenate(outs, axis=1).astype(y_ref.dtype)

    @pl.when(c == nc - 1)
    def _():
        hout_ref[0] = ht_new.T


def _ssd_call(xbc, z, small, h0, conv0, cw, cb, dtb, a_pad, d_e, nw, b, l, q):
    consts, hp, w = _ssd_constants(q)
    nc = l // q
    t = b * l
    row = lambda i, c: (i * nc + c, 0)
    params = [cw, cb, dtb, a_pad, d_e, nw] + [consts[k] for k in
                                              ("l3", "ones3", "e3p", "e3q", "dmask", "causal", "gmask", "xmask")]
    hp_shape = (SSD_WIDTH, SSD_STATE)
    return pl.pallas_call(
        functools.partial(_ssd_kernel, q=q, hp=hp, w=w),
        grid=(b, nc),
        in_specs=[pl.BlockSpec((q, SSD_CONV_DIM), row),
                  pl.BlockSpec((q, SSD_WIDTH), row),
                  pl.BlockSpec((q, LANES), row),
                  pl.BlockSpec((1,) + hp_shape, lambda i, c: (i, 0, 0)),
                  pl.BlockSpec((1, SUBLANES, SSD_CONV_DIM), lambda i, c: (i, 0, 0))]
                 + [_const_spec(p.shape) for p in params],
        out_specs=[pl.BlockSpec((q, SSD_WIDTH), row),
                   pl.BlockSpec((1,) + hp_shape, lambda i, c: (i, 0, 0)),
                   pl.BlockSpec((1, CONV_WIDTH - 1, SSD_CONV_DIM), lambda i, c: (i, 0, 0))],
        out_shape=[jax.ShapeDtypeStruct((t, SSD_WIDTH), F32),
                   jax.ShapeDtypeStruct((b,) + hp_shape, F32),
                   jax.ShapeDtypeStruct((b, CONV_WIDTH - 1, SSD_CONV_DIM), F32)],
        scratch_shapes=[pltpu.VMEM((SSD_STATE, SSD_WIDTH), F32),
                        pltpu.VMEM((SUBLANES, SSD_CONV_DIM), F32)],
        compiler_params=pltpu.CompilerParams(dimension_semantics=("arbitrary", "arbitrary"),
                                             vmem_limit_bytes=VMEM_LIMIT),
        name="ssd_scan",
    )(xbc, z, small, h0, conv0, *params)


MXU_DIM = 256


def _gdn_constants(q, nseq, hpb):
    assert nseq * hpb * q == MXU_DIM
    nhb = GDN_HEADS // hpb
    rows = nseq * q
    e_beta = np.zeros((LANES, GDN_WIDTH), np.float32)
    e_alpha = np.zeros((LANES, GDN_WIDTH), np.float32)
    e_c = np.zeros((LANES, nhb * MXU_DIM), np.float32)
    for h in range(GDN_HEADS):
        e_beta[BETA_LANE0 + h, h * GDN_DIM:(h + 1) * GDN_DIM] = 1.0
        e_alpha[ALPHA_LANE0 + h, h * GDN_DIM:(h + 1) * GDN_DIM] = 1.0
    jh, seq_c, hh_c, s_c = np.unravel_index(np.arange(nhb * MXU_DIM), (nhb, nseq, hpb, q))
    e_c[ALPHA_LANE0 + jh * hpb + hh_c, np.arange(nhb * MXU_DIM)] = 1.0
    seq_r, l_r = np.unravel_index(np.arange(rows), (nseq, q))
    same_seq = seq_r[:, None] == seq_c[None, :]
    dmask = same_seq & (l_r[:, None] == s_c[None, :])
    incl = same_seq & (l_r[:, None] >= s_c[None, :])
    strict = same_seq & (l_r[:, None] > s_c[None, :])
    seq_s, hh_s, _ = np.unravel_index(np.arange(MXU_DIM), (nseq, hpb, q))
    bdmask = (seq_s[:, None] == seq_c[None, :MXU_DIM]) & (hh_s[:, None] == hh_c[None, :MXU_DIM])
    kmask = hh_s[:, None] == (np.arange(hpb * GDN_DIM) // GDN_DIM)[None, :]
    f = lambda m: jnp.asarray(m.astype(np.float32))
    mask_dtype = BF16 if q % (2 * SUBLANES) == 0 else F32
    return dict(
        l3=jnp.asarray(np.kron(np.eye(nseq), np.tril(np.ones((q, q)))), BF16),
        ones3=jnp.ones((rows, rows), BF16),
        e3b=jnp.asarray(np.tile(e_beta, (3, 1)), BF16),
        e3a=jnp.asarray(np.tile(e_alpha, (3, 1)), BF16),
        e3c=jnp.asarray(np.tile(e_c, (3, 1)), BF16),
        dmask=f(dmask), incl=f(incl), strict=f(strict), eye=f(dmask[:, :MXU_DIM]),
        bdmask=f(bdmask).astype(mask_dtype), kmask=f(kmask).astype(mask_dtype),
    )


def _gdn_kernel_chunk_major(qkv_ref, gate_ref, sm_ref, s0_ref, conv0_ref, cw_ref, dtb_ref, a_ref, nw_ref,
                l3_ref, ones3_ref, e3b_ref, e3a_ref, e3c_ref, dmask_ref, incl_ref, strict_ref, eye_ref,
                bdmask_ref, kmask_ref,
                o_ref, sout_ref, convout_ref, s_ref, tail_ref, *, q, nseq, hpb, cps):
    c = pl.program_id(1)
    nc = pl.num_programs(1)

    @pl.when(c == 0)
    def _():
        s_ref[...] = s0_ref[...]
        tail_ref[...] = conv0_ref[...]

    for ci in range(cps):
        _gdn_chunk(ci, qkv_ref, gate_ref, sm_ref, cw_ref, dtb_ref, a_ref, nw_ref,
                   l3_ref, ones3_ref, e3b_ref, e3a_ref, e3c_ref, dmask_ref, incl_ref, strict_ref, eye_ref,
                   bdmask_ref, kmask_ref, o_ref, convout_ref, s_ref, tail_ref, q=q, nseq=nseq, hpb=hpb, cps=cps)

    @pl.when(c == nc - 1)
    def _():
        sout_ref[...] = s_ref[...]


def _gdn_chunk(ci, qkv_ref, gate_ref, sm_ref, cw_ref, dtb_ref, a_ref, nw_ref,
               l3_ref, ones3_ref, e3b_ref, e3a_ref, e3c_ref, dmask_ref, incl_ref, strict_ref, eye_ref,
               bdmask_ref, kmask_ref, o_ref, convout_ref, s_ref, tail_ref, *, q, nseq, hpb, cps):
    nhb = GDN_HEADS // hpb
    rows = nseq * q
    bw = hpb * GDN_DIM
    r0 = ci * rows

    acts = []
    for sq in range(nseq):
        if ci == 0:
            ext = jnp.concatenate([tail_ref[sq], qkv_ref[sq * q:(sq + 1) * q]], axis=0)
        else:
            ext = qkv_ref[r0 - SUBLANES:r0 + q]
        acts.append(_silu(_causal_conv(ext, cw_ref, q)))
        if ci == cps - 1:
            tail_ref[sq] = ext[q:q + SUBLANES]
            convout_ref[sq] = ext[q + SUBLANES - (CONV_WIDTH - 1):q + SUBLANES]
    act = acts[0] if nseq == 1 else jnp.concatenate(acts, axis=0)

    def l2n(x):
        parts = []
        for h in range(GDN_HEADS):
            xh = x[:, h * GDN_DIM:(h + 1) * GDN_DIM]
            parts.append(xh * lax.rsqrt(jnp.sum(xh * xh, axis=-1, keepdims=True) + L2_EPS))
        return jnp.concatenate(parts, axis=1)

    qn = l2n(act[:, :GDN_WIDTH]) * (GDN_DIM ** -0.5)
    kn = l2n(act[:, GDN_WIDTH:2 * GDN_WIDTH])
    v = act[:, 2 * GDN_WIDTH:]

    sm = sm_ref[r0:r0 + rows]
    beta = jax.nn.sigmoid(sm)
    g = a_ref[...] * _softplus(sm + dtb_ref[...])
    gc = _dot_exact_rhs(l3_ref[...], g)
    gc3 = _split3_lanes(gc)
    beta_e = _dot(_split3_lanes(beta), e3b_ref[...])
    gc_e = _dot(gc3, e3a_ref[...])
    col_c = _dot(gc3, e3c_ref[...])
    row_c = _dot_exact_rhs(ones3_ref[...], col_c * dmask_ref[...])
    decay_c = jnp.exp(jnp.minimum(col_c - row_c, 0.0)) * incl_ref[...]
    egc_e = jnp.exp(gc_e)
    lasts = [gc_e[(sq + 1) * q - 1:(sq + 1) * q, :] for sq in range(nseq)]
    eouts = [jnp.exp(lasts[sq] - gc_e[sq * q:(sq + 1) * q]) for sq in range(nseq)]
    eout_e = eouts[0] if nseq == 1 else jnp.concatenate(eouts, axis=0)
    kb = kn * beta_e
    vb = v * beta_e
    kbg = kb * egc_e
    qin = qn * egc_e
    kout = (kn * eout_e).astype(BF16)

    bdmask = bdmask_ref[...]
    kmask = kmask_ref[...]
    mask_dtype = bdmask.dtype

    def stack_rows(x):
        return jnp.concatenate([x[sq * q:(sq + 1) * q] for sq in range(nseq) for _ in range(hpb)], axis=0)

    def masked_bf16(x, mask):
        return (x.astype(mask_dtype) * mask).astype(BF16)

    def block_diag_pieces(m):
        hi = m.astype(BF16)
        lo = (m - hi.astype(F32)).astype(BF16)
        return masked_bf16(stack_rows(hi.astype(mask_dtype)), bdmask), masked_bf16(stack_rows(lo.astype(mask_dtype)), bdmask)

    def lhs_pieces(m):
        hi = m.astype(BF16)
        return hi, (m - hi.astype(F32)).astype(BF16)

    def head_lanes(h):
        return slice(h * GDN_DIM, (h + 1) * GDN_DIM)

    def seq_rows(sq):
        return slice(sq * q, (sq + 1) * q)

    n_sq = q.bit_length() - 1
    assert q == 1 << n_sq and n_sq >= 2
    o_blocks = []
    for jb in range(nhb):
        nl = slice(jb * bw, (jb + 1) * bw)
        cl = slice(jb * MXU_DIM, (jb + 1) * MXU_DIM)
        k_bd = masked_bf16(stack_rows(kn[:, nl].astype(mask_dtype)), kmask)
        prod = _dot_nt(jnp.concatenate([kb[:, nl], qn[:, nl]], axis=0).astype(BF16), k_bd)
        dec = decay_c[:, cl]
        x = -(prod[:rows] * dec * strict_ref[:, cl])
        attn_c = prod[rows:] * dec
        inv = eye_ref[...] + x
        ypow = x
        for k in range(n_sq):
            first, last = k == 0, k == n_sq - 1
            b_hi, b_lo = block_diag_pieces(ypow)
            lhs = ypow if first else (inv if last else jnp.concatenate([inv, ypow], axis=0))
            l_hi, l_lo = lhs_pieces(lhs)
            res = _dot(jnp.concatenate([l_hi, l_hi, l_lo], axis=1), jnp.concatenate([b_hi, b_lo, b_hi], axis=0))
            if first:
                ypow = res
            elif last:
                inv = inv + res
            else:
                inv, ypow = inv + res[:rows], res[rows:]
        i_hi, i_lo = block_diag_pieces(inv)
        segs = [(sq, jb * hpb + hh) for sq in range(nseq) for hh in range(hpb)]
        rhs = jnp.concatenate([jnp.concatenate([vb[seq_rows(sq), head_lanes(h)], kbg[seq_rows(sq), head_lanes(h)]],
                                               axis=1) for sq, h in segs], axis=0).astype(BF16)
        uw = _dot(jnp.concatenate([i_hi, i_lo], axis=1), jnp.concatenate([rhs, rhs], axis=0))
        ws, qs, states = [], [], []
        for i, (sq, h) in enumerate(segs):
            s = s_ref[sq, h]
            states.append(s)
            wq = jnp.concatenate([uw[i * q:(i + 1) * q, GDN_DIM:], qin[seq_rows(sq), head_lanes(h)]], axis=0)
            r = _dot(wq.astype(BF16), s.astype(BF16))
            ws.append(r[:q])
            qs.append(r[q:])
        v_new = uw[:, :GDN_DIM] - jnp.concatenate(ws, axis=0)
        v_bf = v_new.astype(BF16)
        v_rep = masked_bf16(jnp.concatenate([v_bf.astype(mask_dtype)] * hpb, axis=1), kmask)
        qs_rows = [jnp.concatenate(qs[sq * hpb:(sq + 1) * hpb], axis=1) for sq in range(nseq)]
        qs_nat = qs_rows[0] if nseq == 1 else jnp.concatenate(qs_rows, axis=0)
        o_blocks.append(qs_nat + _dot(attn_c.astype(BF16), v_rep))
        for i, (sq, h) in enumerate(segs):
            ge = jnp.exp(lasts[sq][:, head_lanes(h)])
            s_ref[sq, h] = states[i] * ge + _dot_tn(kout[seq_rows(sq), head_lanes(h)], v_bf[i * q:(i + 1) * q])

    o = o_blocks[0] if nhb == 1 else jnp.concatenate(o_blocks, axis=1)
    gate = gate_ref[r0:r0 + rows]
    nw = nw_ref[...]
    outs = [_rms(o[:, head_lanes(h)], nw) * _silu(gate[:, head_lanes(h)]) for h in range(GDN_HEADS)]
    o_ref[r0:r0 + rows] = jnp.concatenate(outs, axis=1).astype(o_ref.dtype)


def _gdn_kernel(qkv_ref, gate_ref, sm_ref, s0_ref, conv0_ref, cw_ref, dtb_ref, a_ref, nw_ref,
                l3_ref, ones3_ref, e3b_ref, e3a_ref, e3c_ref, dmask_ref, incl_ref, strict_ref, eye_ref,
                bdmask_ref, kmask_ref,
                o_ref, sout_ref, convout_ref, s_ref, tail_ref, *, q, nseq, hpb, cps):
    c = pl.program_id(1)
    nc = pl.num_programs(1)
    nhb = GDN_HEADS // hpb
    rows = nseq * q
    bw = hpb * GDN_DIM
    n_sq = q.bit_length() - 1
    assert q == 1 << n_sq and n_sq >= 2

    @pl.when(c == 0)
    def _():
        s_ref[...] = s0_ref[...]
        tail_ref[...] = conv0_ref[...]

    bdmask = bdmask_ref[...]
    kmask = kmask_ref[...]
    mask_dtype = bdmask.dtype

    def head_lanes(h):
        return slice(h * GDN_DIM, (h + 1) * GDN_DIM)

    def seq_rows(sq):
        return slice(sq * q, (sq + 1) * q)

    def stack_rows(x):
        return jnp.concatenate([x[seq_rows(sq)] for sq in range(nseq) for _ in range(hpb)], axis=0)

    def masked_bf16(x, mask):
        return (x.astype(mask_dtype) * mask).astype(BF16)

    def block_diag_pieces(m):
        hi = m.astype(BF16)
        lo = (m - hi.astype(F32)).astype(BF16)
        return (masked_bf16(stack_rows(hi.astype(mask_dtype)), bdmask),
                masked_bf16(stack_rows(lo.astype(mask_dtype)), bdmask))

    def lhs_pieces(m):
        hi = m.astype(BF16)
        return hi, (m - hi.astype(F32)).astype(BF16)

    def l2n(x):
        parts = []
        for h in range(GDN_HEADS):
            xh = x[:, head_lanes(h)]
            parts.append(xh * lax.rsqrt(jnp.sum(xh * xh, axis=-1, keepdims=True) + L2_EPS))
        return jnp.concatenate(parts, axis=1)

    def chunk_local(ci):
        r0 = ci * rows
        acts = []
        for sq in range(nseq):
            if ci == 0:
                ext = jnp.concatenate([tail_ref[sq], qkv_ref[seq_rows(sq)]], axis=0)
            else:
                ext = qkv_ref[r0 - SUBLANES:r0 + q]
            acts.append(_silu(_causal_conv(ext, cw_ref, q)))
            if ci == cps - 1:
                tail_ref[sq] = ext[q:q + SUBLANES]
                convout_ref[sq] = ext[q + SUBLANES - (CONV_WIDTH - 1):q + SUBLANES]
        act = acts[0] if nseq == 1 else jnp.concatenate(acts, axis=0)
        qn = l2n(act[:, :GDN_WIDTH]) * (GDN_DIM ** -0.5)
        kn = l2n(act[:, GDN_WIDTH:2 * GDN_WIDTH])
        v = act[:, 2 * GDN_WIDTH:]
        sm = sm_ref[r0:r0 + rows]
        beta = jax.nn.sigmoid(sm)
        g = a_ref[...] * _softplus(sm + dtb_ref[...])
        gc = _dot_exact_rhs(l3_ref[...], g)
        gc3 = _split3_lanes(gc)
        beta_e = _dot(_split3_lanes(beta), e3b_ref[...])
        gc_e = _dot(gc3, e3a_ref[...])
        col_c = _dot(gc3, e3c_ref[...])
        row_c = _dot_exact_rhs(ones3_ref[...], col_c * dmask_ref[...])
        egc_e = jnp.exp(gc_e)
        lasts = [gc_e[(sq + 1) * q - 1:(sq + 1) * q, :] for sq in range(nseq)]
        eouts = [jnp.exp(lasts[sq] - gc_e[seq_rows(sq)]) for sq in range(nseq)]
        kb = kn * beta_e
        return dict(
            qn=qn, kn=kn, kb=kb, vb=v * beta_e, kbg=kb * egc_e, qin=qn * egc_e,
            kout=(kn * (eouts[0] if nseq == 1 else jnp.concatenate(eouts, axis=0))).astype(BF16),
            decay_c=jnp.exp(jnp.minimum(col_c - row_c, 0.0)) * incl_ref[...],
            ge=[jnp.exp(last) for last in lasts])

    def chain_init(loc, jb):
        nl = slice(jb * bw, (jb + 1) * bw)
        cl = slice(jb * MXU_DIM, (jb + 1) * MXU_DIM)
        k_bd = masked_bf16(stack_rows(loc["kn"][:, nl].astype(mask_dtype)), kmask)
        prod = _dot_nt(jnp.concatenate([loc["kb"][:, nl], loc["qn"][:, nl]], axis=0).astype(BF16), k_bd)
        dec = loc["decay_c"][:, cl]
        x = -(prod[:rows] * dec * strict_ref[:, cl])
        return dict(inv=eye_ref[...] + x, ypow=x, attn=(prod[rows:] * dec).astype(BF16))

    def double_step(st, k):
        first, last = k == 0, k == n_sq - 1
        inv, ypow = st["inv"], st["ypow"]
        b_hi, b_lo = block_diag_pieces(ypow)
        lhs = ypow if first else (inv if last else jnp.concatenate([inv, ypow], axis=0))
        l_hi, l_lo = lhs_pieces(lhs)
        res = _dot(jnp.concatenate([l_hi, l_hi, l_lo], axis=1), jnp.concatenate([b_hi, b_lo, b_hi], axis=0))
        if first:
            return dict(st, ypow=res)
        if last:
            return dict(st, inv=inv + res)
        return dict(st, inv=inv + res[:rows], ypow=res[rows:])

    def solve(loc, jb, inv):
        i_hi, i_lo = block_diag_pieces(inv)
        segs = [(sq, jb * hpb + hh) for sq in range(nseq) for hh in range(hpb)]
        rhs = jnp.concatenate([jnp.concatenate([loc["vb"][seq_rows(sq), head_lanes(h)],
                                                loc["kbg"][seq_rows(sq), head_lanes(h)]], axis=1)
                               for sq, h in segs], axis=0).astype(BF16)
        return _dot(jnp.concatenate([i_hi, i_lo], axis=1), jnp.concatenate([rhs, rhs], axis=0))

    def state_step(loc, jb, uw, attn):
        segs = [(sq, jb * hpb + hh) for sq in range(nseq) for hh in range(hpb)]
        ws, qs, states = [], [], []
        for i, (sq, h) in enumerate(segs):
            s = s_ref[sq, h]
            states.append(s)
            wq = jnp.concatenate([uw[i * q:(i + 1) * q, GDN_DIM:], loc["qin"][seq_rows(sq), head_lanes(h)]], axis=0)
            r = _dot(wq.astype(BF16), s.astype(BF16))
            ws.append(r[:q])
            qs.append(r[q:])
        v_bf = (uw[:, :GDN_DIM] - jnp.concatenate(ws, axis=0)).astype(BF16)
        v_rep = masked_bf16(jnp.concatenate([v_bf.astype(mask_dtype)] * hpb, axis=1), kmask)
        qs_rows = [jnp.concatenate(qs[sq * hpb:(sq + 1) * hpb], axis=1) for sq in range(nseq)]
        qs_nat = qs_rows[0] if nseq == 1 else jnp.concatenate(qs_rows, axis=0)
        for i, (sq, h) in enumerate(segs):
            s_ref[sq, h] = (states[i] * loc["ge"][sq][:, head_lanes(h)]
                            + _dot_tn(loc["kout"][seq_rows(sq), head_lanes(h)], v_bf[i * q:(i + 1) * q]))
        return qs_nat + _dot(attn, v_rep)

    local = [chunk_local(ci) for ci in range(cps)]
    chains = [(ci, jb) for ci in range(cps) for jb in range(nhb)]
    st = [chain_init(local[ci], jb) for ci, jb in chains]
    for k in range(n_sq):
        st = [double_step(s, k) for s in st]
    uws = [solve(local[ci], jb, s["inv"]) for (ci, jb), s in zip(chains, st)]
    nw = nw_ref[...]
    for ci in range(cps):
        o_blocks = [state_step(local[ci], jb, uws[ci * nhb + jb], st[ci * nhb + jb]["attn"]) for jb in range(nhb)]
        o = o_blocks[0] if nhb == 1 else jnp.concatenate(o_blocks, axis=1)
        gate = gate_ref[ci * rows:(ci + 1) * rows]
        outs = [_rms(o[:, head_lanes(h)], nw) * _silu(gate[:, head_lanes(h)]) for h in range(GDN_HEADS)]
        o_ref[ci * rows:(ci + 1) * rows] = jnp.concatenate(outs, axis=1).astype(o_ref.dtype)

    @pl.when(c == nc - 1)
    def _():
        sout_ref[...] = s_ref[...]


def _gdn_call(qkv, gate, small, s0, conv0, cw, dtb, a_pad, nw, b, l, q):
    nseq = max(1, MXU_DIM // (GDN_HEADS * q))
    hpb = MXU_DIM // (nseq * q)
    consts = _gdn_constants(q, nseq, hpb)
    cps = GDN_CHUNKS_PER_STEP if nseq == 1 and (l // q) % GDN_CHUNKS_PER_STEP == 0 else 1
    nc = l // (q * cps)
    assert b % nseq == 0 and (nseq == 1 or nc == 1)
    t = b * l
    rows = nseq * q * cps
    row = lambda i, c: (i * nc + c, 0)
    params = [cw, dtb, a_pad, nw] + [consts[k] for k in
                                     ("l3", "ones3", "e3b", "e3a", "e3c", "dmask", "incl", "strict", "eye",
                                      "bdmask", "kmask")]
    s_shape = (GDN_HEADS, GDN_DIM, GDN_DIM)
    return pl.pallas_call(
        functools.partial(_gdn_kernel, q=q, nseq=nseq, hpb=hpb, cps=cps),
        grid=(b // nseq, nc),
        in_specs=[pl.BlockSpec((rows, GDN_CONV_DIM), row),
                  pl.BlockSpec((rows, GDN_WIDTH), row),
                  pl.BlockSpec((rows, LANES), row),
                  pl.BlockSpec((nseq,) + s_shape, lambda i, c: (i, 0, 0, 0)),
                  pl.BlockSpec((nseq, SUBLANES, GDN_CONV_DIM), lambda i, c: (i, 0, 0))]
                 + [_const_spec(p.shape) for p in params],
        out_specs=[pl.BlockSpec((rows, GDN_WIDTH), row),
                   pl.BlockSpec((nseq,) + s_shape, lambda i, c: (i, 0, 0, 0)),
                   pl.BlockSpec((nseq, CONV_WIDTH - 1, GDN_CONV_DIM), lambda i, c: (i, 0, 0))],
        out_shape=[jax.ShapeDtypeStruct((t, GDN_WIDTH), F32),
                   jax.ShapeDtypeStruct((b,) + s_shape, F32),
                   jax.ShapeDtypeStruct((b, CONV_WIDTH - 1, GDN_CONV_DIM), F32)],
        scratch_shapes=[pltpu.VMEM((nseq,) + s_shape, F32),
                        pltpu.VMEM((nseq, SUBLANES, GDN_CONV_DIM), F32)],
        compiler_params=pltpu.CompilerParams(dimension_semantics=("arbitrary", "arbitrary"),
                                             vmem_limit_bytes=VMEM_LIMIT),
        name="gdn_scan",
    )(qkv, gate, small, s0, conv0, *params)


def _outffn_kernel(x_ref, y_ref, o_ref, g1_ref, sh2_ref, sc2_ref, g2_ref, nwm_ref, nwf_ref, nwp_ref,
                   wout_ref, wgu_ref, wdown_ref, out_ref, *, ff_chunk):
    x = x_ref[...]
    shape3 = x.shape
    tm = shape3[0] * shape3[1]
    m = (_dot(y_ref[...].astype(BF16), wout_ref[:SSD_WIDTH, :])
         + _dot(o_ref[...].astype(BF16), wout_ref[SSD_WIDTH:, :]))
    x1 = x + g1_ref[...] * _rms(m, nwm_ref[...]).reshape(shape3)
    h2 = _rms(x1, nwf_ref[...]) * (1.0 + sc2_ref[...]) + sh2_ref[...]
    h2 = h2.reshape(tm, shape3[2]).astype(BF16)
    f = jnp.zeros((tm, shape3[2]), F32)
    for j in range(D_FF // ff_chunk):
        gj = _dot(h2, wgu_ref[:, j * ff_chunk:(j + 1) * ff_chunk])
        uj = _dot(h2, wgu_ref[:, D_FF + j * ff_chunk:D_FF + (j + 1) * ff_chunk])
        f = f + _dot((_silu(gj) * uj).astype(BF16), wdown_ref[j * ff_chunk:(j + 1) * ff_chunk, :])
    out_ref[...] = x1 + g2_ref[...] * _rms(f, nwp_ref[...]).reshape(shape3)


def _outffn_call(x, y, o, mod3, nw_mix_post, nw_ffn_pre, nw_ffn_post, w_out, w_gu, w_down, bb, lb):
    b, l, d = x.shape
    nl = l // lb
    tm = bb * lb
    row = lambda i, j: (i * nl + j, 0)
    mod_spec = lambda k: pl.BlockSpec((bb, 1, d), lambda i, j: (i, 0, k))
    return pl.pallas_call(
        functools.partial(_outffn_kernel, ff_chunk=D_FF // 2),
        grid=(b // bb, nl),
        in_specs=[pl.BlockSpec((bb, lb, d), lambda i, j: (i, j, 0)),
                  pl.BlockSpec((tm, SSD_WIDTH), row),
                  pl.BlockSpec((tm, GDN_WIDTH), row),
                  mod_spec(2), mod_spec(3), mod_spec(4), mod_spec(5),
                  _const_spec((1, d)), _const_spec((1, d)), _const_spec((1, d)),
                  _const_spec(w_out.shape), _const_spec(w_gu.shape), _const_spec(w_down.shape)],
        out_specs=pl.BlockSpec((bb, lb, d), lambda i, j: (i, j, 0)),
        out_shape=jax.ShapeDtypeStruct((b, l, d), F32),
        compiler_params=pltpu.CompilerParams(dimension_semantics=("arbitrary", "arbitrary"),
                                             vmem_limit_bytes=VMEM_LIMIT),
        name="outproj_ffn",
    )(x, y, o, mod3, mod3, mod3, mod3, nw_mix_post, nw_ffn_pre, nw_ffn_post, w_out, w_gu, w_down)


def _pad_lanes(v, lane0):
    out = jnp.zeros((1, LANES), F32)
    return out.at[0, lane0:lane0 + v.shape[0]].set(v.astype(F32))


def _conv_hist(buf):
    return jnp.pad(buf, ((0, 0), (SUBLANES - (CONV_WIDTH - 1), 0), (0, 0)))


def _layer(x, mod, ssd_h0, ssd_conv0, gdn_s0, gdn_conv0, p, q, bb, lb):
    b, l, d = x.shape
    mod3 = mod.reshape(b, 1, 6 * d)
    z, xbc, qkv, gate, small = _inproj_call(x, mod3, p["norm_mix_pre"], p["w_big"], p["w_small"], bb, lb)
    y, ssd_h, ssd_conv = _ssd_call(
        xbc, z, small, ssd_h0.reshape(b, SSD_WIDTH, SSD_STATE), _conv_hist(ssd_conv0),
        p["ssd_conv_w"], p["ssd_conv_b"], p["ssd_dtb"], p["ssd_a"], p["ssd_d"], p["ssd_norm_w"], b, l, q)
    o, gdn_s, gdn_conv = _gdn_call(
        qkv, gate, small, gdn_s0, _conv_hist(gdn_conv0),
        p["gdn_conv_w"], p["gdn_dtb"], p["gdn_a"], p["gdn_norm_w"], b, l, q)
    out = _outffn_call(x, y, o, mod3, p["norm_mix_post"], p["norm_ffn_pre"], p["norm_ffn_post"],
                       p["w_out"], p["w_gu"], p["w_down"], bb, lb)
    return out, ssd_h.reshape(ssd_h0.shape), ssd_conv, gdn_s, gdn_conv


def _layer_params(l, w_ada, b_ada, norm_mix_pre, norm_mix_post, norm_ffn_pre, norm_ffn_post, w_in, ssd_conv_w,
                  ssd_conv_b, ssd_dt_bias, ssd_A_log, ssd_D, ssd_norm_w, gdn_conv_w, gdn_dt_bias, gdn_A_log,
                  gdn_norm_w, w_out, w_gate_up, w_down):
    w = w_in[l]
    o_z, o_xbc = 0, SSD_WIDTH
    o_dt = o_xbc + SSD_CONV_DIM
    o_qkv = o_dt + SSD_HEADS
    o_gate = o_qkv + GDN_CONV_DIM
    o_beta = o_gate + GDN_WIDTH
    o_alpha = o_beta + GDN_HEADS
    w_big = jnp.concatenate([w[:, o_z:o_dt], w[:, o_qkv:o_beta]], axis=1).astype(BF16)
    w_small = jnp.concatenate([w[:, o_dt:o_qkv], w[:, o_beta:o_alpha + GDN_HEADS],
                               jnp.zeros((w.shape[0], LANES - SSD_HEADS - 2 * GDN_HEADS), w.dtype)],
                              axis=1).astype(BF16)
    row = lambda v: v.reshape(1, -1).astype(F32)
    return dict(
        w_ada=w_ada[l], b_ada=b_ada[l],
        norm_mix_pre=row(norm_mix_pre[l]), norm_mix_post=row(norm_mix_post[l]),
        norm_ffn_pre=row(norm_ffn_pre[l]), norm_ffn_post=row(norm_ffn_post[l]),
        w_big=w_big, w_small=w_small,
        ssd_conv_w=ssd_conv_w[l], ssd_conv_b=row(ssd_conv_b[l]),
        ssd_dtb=_pad_lanes(ssd_dt_bias[l], DT_LANE0),
        ssd_a=_pad_lanes(-jnp.exp(ssd_A_log[l].astype(F32)), DT_LANE0),
        ssd_d=row(jnp.repeat(ssd_D[l], SSD_HEADDIM)),
        ssd_norm_w=row(ssd_norm_w[l]),
        gdn_conv_w=gdn_conv_w[l],
        gdn_dtb=_pad_lanes(gdn_dt_bias[l], ALPHA_LANE0),
        gdn_a=_pad_lanes(-jnp.exp(gdn_A_log[l].astype(F32)), ALPHA_LANE0),
        gdn_norm_w=row(gdn_norm_w[l]),
        w_out=w_out[l].astype(BF16), w_gu=w_gate_up[l].astype(BF16), w_down=w_down[l].astype(BF16),
    )


def kernel(x_prompt, x_sample, c_prompt, c_sample, state_ssd, state_ssd_conv, state_gdn, state_gdn_conv, w_ada, b_ada, norm_mix_pre, norm_mix_post, norm_ffn_pre, norm_ffn_post, w_in, ssd_conv_w, ssd_conv_b, ssd_dt_bias, ssd_A_log, ssd_D, ssd_norm_w, gdn_conv_w, gdn_dt_bias, gdn_A_log, gdn_norm_w, w_out, w_gate_up, w_down):
    depth = w_in.shape[0]
    bp, lp, _ = x_prompt.shape
    bs, ls, _ = x_sample.shape
    yp, ys = x_prompt, x_sample
    outs = [[] for _ in range(8)]
    for l in range(depth):
        p = _layer_params(l, w_ada, b_ada, norm_mix_pre, norm_mix_post, norm_ffn_pre, norm_ffn_post, w_in,
                          ssd_conv_w, ssd_conv_b, ssd_dt_bias, ssd_A_log, ssd_D, ssd_norm_w, gdn_conv_w,
                          gdn_dt_bias, gdn_A_log, gdn_norm_w, w_out, w_gate_up, w_down)
        mod = _mod_call(jnp.concatenate([c_prompt, c_sample], axis=0), p["w_ada"], p["b_ada"])
        zeros = lambda a: jnp.zeros((bp,) + a.shape[2:], a.dtype)
        yp, a0, a1, a2, a3 = _layer(yp, mod[:bp], zeros(state_ssd), zeros(state_ssd_conv), zeros(state_gdn),
                                    zeros(state_gdn_conv), p, q=min(CHUNK, lp), bb=1, lb=min(ROW_TILE, lp))
        ys, b0, b1, b2, b3 = _layer(ys, mod[bp:], state_ssd[l], state_ssd_conv[l], state_gdn[l],
                                    state_gdn_conv[l], p, q=min(CHUNK, ls), bb=min(ROW_TILE // ls, bs), lb=ls)
        for lst, v in zip(outs, (a0, a1, a2, a3, b0, b1, b2, b3)):
            lst.append(v)
    return (yp, ys) + tuple(jnp.stack(v) for v in outs)
```

```python
import functools

import numpy as np
import jax
import jax.numpy as jnp
from jax import lax
from jax.experimental import pallas as pl
from jax.experimental.pallas import tpu as pltpu

F32 = jnp.float32
BF16 = jnp.bfloat16
HIGHEST = lax.Precision.HIGHEST

D_MODEL = 1024
SSD_HEADS = 16
SSD_HEADDIM = 64
SSD_GROUPS = 2
SSD_STATE = 128
SSD_WIDTH = SSD_HEADS * SSD_HEADDIM
SSD_CONV_DIM = SSD_WIDTH + 2 * SSD_GROUPS * SSD_STATE
GDN_HEADS = 8
GDN_DIM = 128
GDN_WIDTH = GDN_HEADS * GDN_DIM
GDN_CONV_DIM = 3 * GDN_WIDTH
CONV_WIDTH = 4
CHUNK = 64
D_FF = 2816
NORM_EPS = 1e-6
L2_EPS = 1e-6
LANES = 128
SUBLANES = 8
BIG_COLS = SSD_WIDTH + SSD_CONV_DIM + GDN_CONV_DIM + GDN_WIDTH
DT_LANE0, BETA_LANE0, ALPHA_LANE0 = 0, SSD_HEADS, SSD_HEADS + GDN_HEADS
VMEM_LIMIT = 56 * 1024 * 1024
ROW_TILE = 256
SCAN_UNITS_PER_STEP = 4


def _dot(a, b):
    return jnp.dot(a, b, preferred_element_type=F32)


def _dot_hi(a, b):
    return jnp.dot(a, b, preferred_element_type=F32, precision=HIGHEST)


def _dot_nt(a, b):
    return lax.dot_general(a, b, (((1,), (1,)), ((), ())), preferred_element_type=F32)


def _dot_tn(a, b):
    return lax.dot_general(a, b, (((0,), (0,)), ((), ())), preferred_element_type=F32)


def _bf16_pieces(x):
    hi = x.astype(BF16)
    r1 = x - hi.astype(F32)
    mid = r1.astype(BF16)
    lo = (r1 - mid.astype(F32)).astype(BF16)
    return hi, mid, lo


def _split3_lanes(x):
    hi, mid, lo = _bf16_pieces(x)
    return jnp.concatenate([hi.astype(F32), mid.astype(F32), lo.astype(F32)], axis=1).astype(BF16)


def _dot_exact_rhs(a_bf16, x):
    hi, mid, lo = _bf16_pieces(x)
    return _dot(a_bf16, hi) + _dot(a_bf16, mid) + _dot(a_bf16, lo)


def _silu(x):
    return x * jax.nn.sigmoid(x)


def _softplus(x):
    return jnp.maximum(x, 0.0) + jnp.log1p(jnp.exp(-jnp.abs(x)))


def _rms(x, w):
    return x * lax.rsqrt(jnp.mean(x * x, axis=-1, keepdims=True) + NORM_EPS) * w


def _causal_conv(ext, w_ref, q):
    acc = ext[SUBLANES:SUBLANES + q] * w_ref[CONV_WIDTH - 1:CONV_WIDTH, :]
    for j in range(1, CONV_WIDTH):
        acc = acc + ext[SUBLANES - j:SUBLANES - j + q] * w_ref[CONV_WIDTH - 1 - j:CONV_WIDTH - j, :]
    return acc


def _mod_kernel(c_ref, w_ref, b_ref, o_ref):
    a = _silu(c_ref[...]).astype(BF16)
    o_ref[...] = _dot(a, w_ref[...].astype(BF16)) + b_ref[...]


def _mod_call(c_all, w_ada, b_ada):
    m, d = c_all.shape
    n = w_ada.shape[1]
    tn = 512
    return pl.pallas_call(
        _mod_kernel,
        grid=(n // tn,),
        in_specs=[pl.BlockSpec((m, d), lambda j: (0, 0)),
                  pl.BlockSpec((d, tn), lambda j: (0, j)),
                  pl.BlockSpec((1, tn), lambda j: (0, j))],
        out_specs=pl.BlockSpec((m, tn), lambda j: (0, j)),
        out_shape=jax.ShapeDtypeStruct((m, n), F32),
        compiler_params=pltpu.CompilerParams(dimension_semantics=("arbitrary",),
                                             vmem_limit_bytes=VMEM_LIMIT),
        name="adaln_mod",
    )(c_all, w_ada, b_ada.reshape(1, n))


def _inproj_kernel(x_ref, sh_ref, sc_ref, nw_ref, wbig_ref, wsm_ref,
                   z_ref, xbc_ref, qkv_ref, gate_ref, sm_ref):
    x = x_ref[...]
    h = _rms(x, nw_ref[...]) * (1.0 + sc_ref[...]) + sh_ref[...]
    h = h.reshape(x.shape[0] * x.shape[1], x.shape[2]).astype(BF16)
    c0 = 0
    for ref in (z_ref, xbc_ref, qkv_ref, gate_ref):
        w = ref.shape[1]
        ref[...] = _dot(h, wbig_ref[:, c0:c0 + w])
        c0 += w
    sm_ref[...] = _dot(h, wsm_ref[...])


def _const_spec(shape):
    nd = len(shape)
    return pl.BlockSpec(shape, lambda *_: (0,) * nd, pipeline_mode=pl.Buffered(1))


def _inproj_call(x, mod3, norm_w, w_big, w_small, bb, lb):
    b, l, d = x.shape
    nl = l // lb
    tm = bb * lb
    t = b * l
    row = lambda i, j: (i * nl + j, 0)
    widths = (SSD_WIDTH, SSD_CONV_DIM, GDN_CONV_DIM, GDN_WIDTH, LANES)
    return pl.pallas_call(
        _inproj_kernel,
        grid=(b // bb, nl),
        in_specs=[pl.BlockSpec((bb, lb, d), lambda i, j: (i, j, 0)),
                  pl.BlockSpec((bb, 1, d), lambda i, j: (i, 0, 0)),
                  pl.BlockSpec((bb, 1, d), lambda i, j: (i, 0, 1)),
                  _const_spec((1, d)),
                  _const_spec(w_big.shape),
                  _const_spec(w_small.shape)],
        out_specs=[pl.BlockSpec((tm, w), row) for w in widths],
        out_shape=[jax.ShapeDtypeStruct((t, w), F32) for w in widths],
        compiler_params=pltpu.CompilerParams(dimension_semantics=("arbitrary", "arbitrary"),
                                             vmem_limit_bytes=VMEM_LIMIT),
        name="inproj",
    )(x, mod3, mod3, norm_w, w_big, w_small)


def _ssd_constants(q, nu):
    hq = SSD_HEADS * q
    hp = LANES // q
    w = hp * SSD_HEADDIM
    e_p = np.zeros((LANES, SSD_WIDTH), np.float32)
    e_q = np.zeros((LANES, hq), np.float32)
    for r in range(SSD_HEADS):
        e_p[DT_LANE0 + r, r * SSD_HEADDIM:(r + 1) * SSD_HEADDIM] = 1.0
        e_q[DT_LANE0 + r, r * q:(r + 1) * q] = 1.0
    tri = np.tril(np.ones((q, q), np.float32))
    gmask = np.zeros((hq, 2 * SSD_STATE), np.float32)
    half = hq // SSD_GROUPS
    gmask[:half, :SSD_STATE] = 1.0
    gmask[half:, SSD_STATE:] = 1.0
    xmask = np.kron(np.eye(hp, dtype=np.float32), np.ones((q, SSD_HEADDIM), np.float32))
    unit_eye = np.eye(nu, dtype=np.float32)
    return dict(
        l3=jnp.asarray(np.kron(unit_eye, tri), BF16),
        ones3=jnp.asarray(np.kron(unit_eye, np.ones((q, q), np.float32)), BF16),
        e3p=jnp.asarray(np.tile(e_p, (3, 1)), BF16),
        e3q=jnp.asarray(np.tile(e_q, (3, 1)), BF16),
        dmask=jnp.asarray(np.tile(np.eye(q, dtype=np.float32), (nu, SSD_HEADS))),
        causal=jnp.asarray(np.tile(tri, (nu, SSD_HEADS))),
        gmask=jnp.asarray(gmask),
        xmask=jnp.asarray(xmask),
    ), hp, w


def _ssd_kernel(xbc_ref, z_ref, sm_ref, h0_ref, conv0_ref, cw_ref, cb_ref, dtb_ref, a_ref, d_ref, nw_ref,
                l3_ref, ones3_ref, e3p_ref, e3q_ref, dmask_ref, causal_ref, gmask_ref, xmask_ref,
                y_ref, hout_ref, convout_ref, ht_ref, tail_ref, *, q, hp, w, nseq, cps):
    c = pl.program_id(1)
    nc = pl.num_programs(1)
    nu = nseq * cps
    rows = nu * q
    gw = SSD_WIDTH // SSD_GROUPS

    @pl.when(c == 0)
    def _():
        for sq in range(nseq):
            ht_ref[sq] = h0_ref[sq].T
        tail_ref[...] = conv0_ref[...]

    def unit_rows(u):
        return slice(u * q, (u + 1) * q)

    acts = []
    for sq in range(nseq):
        n = cps * q
        ext = jnp.concatenate([tail_ref[sq], xbc_ref[sq * n:(sq + 1) * n]], axis=0)
        acts.append(_silu(_causal_conv(ext, cw_ref, n) + cb_ref[...]))
        tail_ref[sq] = ext[n:n + SUBLANES]
        convout_ref[sq] = ext[n + SUBLANES - (CONV_WIDTH - 1):n + SUBLANES]
    act = acts[0] if nseq == 1 else jnp.concatenate(acts, axis=0)
    xs = act[:, :SSD_WIDTH]
    bcat = act[:, SSD_WIDTH:SSD_WIDTH + 2 * SSD_STATE]
    ccat = act[:, SSD_WIDTH + 2 * SSD_STATE:].astype(BF16)

    dt = _softplus(sm_ref[...] + dtb_ref[...])
    acs = _dot_exact_rhs(l3_ref[...], dt * a_ref[...])
    dec_ends = [jnp.exp(acs[(u + 1) * q - 1:(u + 1) * q, :] - acs[unit_rows(u)]) for u in range(nu)]
    dec_end = dec_ends[0] if nu == 1 else jnp.concatenate(dec_ends, axis=0)
    per_head = jnp.concatenate([dt, dt * dec_end, jnp.exp(acs)], axis=0)
    exp3 = _dot(_split3_lanes(per_head), e3p_ref[...])
    dt_e, dtd_e, eacs_e = exp3[:rows], exp3[rows:2 * rows], exp3[2 * rows:]

    col_all = _dot(_split3_lanes(acs), e3q_ref[...])
    row_all = _dot_exact_rhs(ones3_ref[...], col_all * dmask_ref[...])
    decay = jnp.exp(jnp.minimum(col_all - row_all, 0.0)) * causal_ref[...]
    xdt = xs * dt_e
    xd = (xs * dtd_e).astype(BF16)
    dxs = d_ref[...] * xs
    gmask = gmask_ref[...]
    xmask = xmask_ref[...]

    breps = [(jnp.concatenate([bcat[unit_rows(u)]] * SSD_HEADS, axis=0) * gmask).astype(BF16) for u in range(nu)]
    m_alls = [(_dot_nt(ccat[unit_rows(u)], breps[u]) * decay[unit_rows(u)]).astype(BF16)
              for u in range(nu)]
    y_diags, upds = [], []
    for u in range(nu):
        blocks = []
        for j in range(SSD_WIDTH // w):
            xj = (jnp.concatenate([xdt[unit_rows(u), j * w:(j + 1) * w]] * hp, axis=0) * xmask).astype(BF16)
            blocks.append(_dot(m_alls[u][:, j * LANES:(j + 1) * LANES], xj))
        y_diags.append(blocks[0] if len(blocks) == 1 else jnp.concatenate(blocks, axis=1))
        upds.append(jnp.concatenate(
            [_dot_tn(bcat[unit_rows(u), g * SSD_STATE:(g + 1) * SSD_STATE].astype(BF16),
                     xd[unit_rows(u), g * gw:(g + 1) * gw]) for g in range(SSD_GROUPS)], axis=1))

    nw = nw_ref[...]
    for sq in range(nseq):
        ht = ht_ref[sq]
        for ci in range(cps):
            u = sq * cps + ci
            ht_bf = ht.astype(BF16)
            y_off = jnp.concatenate([_dot(ccat[unit_rows(u), g * SSD_STATE:(g + 1) * SSD_STATE],
                                          ht_bf[:, g * gw:(g + 1) * gw]) for g in range(SSD_GROUPS)], axis=1)
            e_u = eacs_e[unit_rows(u)]
            y = y_diags[u] + y_off * e_u + dxs[unit_rows(u)]
            ht = ht * e_u[q - 1:q, :] + upds[u]
            yz = y * _silu(z_ref[unit_rows(u)])
            outs = [_rms(yz[:, g * gw:(g + 1) * gw], nw[:, g * gw:(g + 1) * gw]) for g in range(SSD_GROUPS)]
            y_ref[unit_rows(u)] = jnp.concatenate(outs, axis=1).astype(y_ref.dtype)
        ht_ref[sq] = ht

    @pl.when(c == nc - 1)
    def _():
        for sq in range(nseq):
            hout_ref[sq] = ht_ref[sq].T


def _ssd_call(xbc, z, small, h0, conv0, cw, cb, dtb, a_pad, d_e, nw, b, l, q):
    nchunks = l // q
    cps = SCAN_UNITS_PER_STEP if nchunks % SCAN_UNITS_PER_STEP == 0 else 1
    nseq = SCAN_UNITS_PER_STEP if nchunks == 1 and b % SCAN_UNITS_PER_STEP == 0 else 1
    consts, hp, w = _ssd_constants(q, nseq * cps)
    nc = nchunks // cps
    t = b * l
    rows = nseq * cps * q
    row = lambda i, c: (i * nc + c, 0)
    params = [cw, cb, dtb, a_pad, d_e, nw] + [consts[k] for k in
                                              ("l3", "ones3", "e3p", "e3q", "dmask", "causal", "gmask", "xmask")]
    hp_shape = (SSD_WIDTH, SSD_STATE)
    return pl.pallas_call(
        functools.partial(_ssd_kernel, q=q, hp=hp, w=w, nseq=nseq, cps=cps),
        grid=(b // nseq, nc),
        in_specs=[pl.BlockSpec((rows, SSD_CONV_DIM), row),
                  pl.BlockSpec((rows, SSD_WIDTH), row),
                  pl.BlockSpec((rows, LANES), row),
                  pl.BlockSpec((nseq,) + hp_shape, lambda i, c: (i, 0, 0)),
                  pl.BlockSpec((nseq, SUBLANES, SSD_CONV_DIM), lambda i, c: (i, 0, 0))]
                 + [_const_spec(p.shape) for p in params],
        out_specs=[pl.BlockSpec((rows, SSD_WIDTH), row),
                   pl.BlockSpec((nseq,) + hp_shape, lambda i, c: (i, 0, 0)),
                   pl.BlockSpec((nseq, CONV_WIDTH - 1, SSD_CONV_DIM), lambda i, c: (i, 0, 0))],
        out_shape=[jax.ShapeDtypeStruct((t, SSD_WIDTH), F32),
                   jax.ShapeDtypeStruct((b,) + hp_shape, F32),
                   jax.ShapeDtypeStruct((b, CONV_WIDTH - 1, SSD_CONV_DIM), F32)],
        scratch_shapes=[pltpu.VMEM((nseq, SSD_STATE, SSD_WIDTH), F32),
                        pltpu.VMEM((nseq, SUBLANES, SSD_CONV_DIM), F32)],
        compiler_params=pltpu.CompilerParams(dimension_semantics=("arbitrary", "arbitrary"),
                                             vmem_limit_bytes=VMEM_LIMIT),
        name="ssd_scan",
    )(xbc, z, small, h0, conv0, *params)


MXU_DIM = 256


def _gdn_constants(q, nseq, hpb):
    assert nseq * hpb * q == MXU_DIM
    nhb = GDN_HEADS // hpb
    rows = nseq * q
    e_beta = np.zeros((LANES, GDN_WIDTH), np.float32)
    e_alpha = np.zeros((LANES, GDN_WIDTH), np.float32)
    e_c = np.zeros((LANES, nhb * MXU_DIM), np.float32)
    for h in range(GDN_HEADS):
        e_beta[BETA_LANE0 + h, h * GDN_DIM:(h + 1) * GDN_DIM] = 1.0
        e_alpha[ALPHA_LANE0 + h, h * GDN_DIM:(h + 1) * GDN_DIM] = 1.0
    jh, seq_c, hh_c, s_c = np.unravel_index(np.arange(nhb * MXU_DIM), (nhb, nseq, hpb, q))
    e_c[ALPHA_LANE0 + jh * hpb + hh_c, np.arange(nhb * MXU_DIM)] = 1.0
    seq_r, l_r = np.unravel_index(np.arange(rows), (nseq, q))
    same_seq = seq_r[:, None] == seq_c[None, :]
    dmask = same_seq & (l_r[:, None] == s_c[None, :])
    incl = same_seq & (l_r[:, None] >= s_c[None, :])
    strict = same_seq & (l_r[:, None] > s_c[None, :])
    seq_s, hh_s, _ = np.unravel_index(np.arange(MXU_DIM), (nseq, hpb, q))
    bdmask = (seq_s[:, None] == seq_c[None, :MXU_DIM]) & (hh_s[:, None] == hh_c[None, :MXU_DIM])
    kmask = hh_s[:, None] == (np.arange(hpb * GDN_DIM) // GDN_DIM)[None, :]
    f = lambda m: jnp.asarray(m.astype(np.float32))
    mask_dtype = BF16 if q % (2 * SUBLANES) == 0 else F32
    return dict(
        l3=jnp.asarray(np.kron(np.eye(nseq), np.tril(np.ones((q, q)))), BF16),
        ones3=jnp.ones((rows, rows), BF16),
        e3b=jnp.asarray(np.tile(e_beta, (3, 1)), BF16),
        e3a=jnp.asarray(np.tile(e_alpha, (3, 1)), BF16),
        e3c=jnp.asarray(np.tile(e_c, (3, 1)), BF16),
        dmask=f(dmask), incl=f(incl), strict=f(strict), eye=f(dmask[:, :MXU_DIM]),
        bdmask=f(bdmask).astype(mask_dtype), kmask=f(kmask).astype(mask_dtype),
    )


def _gdn_kernel_chunk_major(qkv_ref, gate_ref, sm_ref, s0_ref, conv0_ref, cw_ref, dtb_ref, a_ref, nw_ref,
                l3_ref, ones3_ref, e3b_ref, e3a_ref, e3c_ref, dmask_ref, incl_ref, strict_ref, eye_ref,
                bdmask_ref, kmask_ref,
                o_ref, sout_ref, convout_ref, s_ref, tail_ref, *, q, nseq, hpb, cps):
    c = pl.program_id(1)
    nc = pl.num_programs(1)

    @pl.when(c == 0)
    def _():
        s_ref[...] = s0_ref[...]
        tail_ref[...] = conv0_ref[...]

    for ci in range(cps):
        _gdn_chunk(ci, qkv_ref, gate_ref, sm_ref, cw_ref, dtb_ref, a_ref, nw_ref,
                   l3_ref, ones3_ref, e3b_ref, e3a_ref, e3c_ref, dmask_ref, incl_ref, strict_ref, eye_ref,
                   bdmask_ref, kmask_ref, o_ref, convout_ref, s_ref, tail_ref, q=q, nseq=nseq, hpb=hpb, cps=cps)

    @pl.when(c == nc - 1)
    def _():
        sout_ref[...] = s_ref[...]


def _gdn_chunk(ci, qkv_ref, gate_ref, sm_ref, cw_ref, dtb_ref, a_ref, nw_ref,
               l3_ref, ones3_ref, e3b_ref, e3a_ref, e3c_ref, dmask_ref, incl_ref, strict_ref, eye_ref,
               bdmask_ref, kmask_ref, o_ref, convout_ref, s_ref, tail_ref, *, q, nseq, hpb, cps):
    nhb = GDN_HEADS // hpb
    rows = nseq * q
    bw = hpb * GDN_DIM
    r0 = ci * rows

    acts = []
    for sq in range(nseq):
        if ci == 0:
            ext = jnp.concatenate([tail_ref[sq], qkv_ref[sq * q:(sq + 1) * q]], axis=0)
        else:
            ext = qkv_ref[r0 - SUBLANES:r0 + q]
        acts.append(_silu(_causal_conv(ext, cw_ref, q)))
        if ci == cps - 1:
            tail_ref[sq] = ext[q:q + SUBLANES]
            convout_ref[sq] = ext[q + SUBLANES - (CONV_WIDTH - 1):q + SUBLANES]
    act = acts[0] if nseq == 1 else jnp.concatenate(acts, axis=0)

    def l2n(x):
        parts = []
        for h in range(GDN_HEADS):
            xh = x[:, h * GDN_DIM:(h + 1) * GDN_DIM]
            parts.append(xh * lax.rsqrt(jnp.sum(xh * xh, axis=-1, keepdims=True) + L2_EPS))
        return jnp.concatenate(parts, axis=1)

    qn = l2n(act[:, :GDN_WIDTH]) * (GDN_DIM ** -0.5)
    kn = l2n(act[:, GDN_WIDTH:2 * GDN_WIDTH])
    v = act[:, 2 * GDN_WIDTH:]

    sm = sm_ref[r0:r0 + rows]
    beta = jax.nn.sigmoid(sm)
    g = a_ref[...] * _softplus(sm + dtb_ref[...])
    gc = _dot_exact_rhs(l3_ref[...], g)
    gc3 = _split3_lanes(gc)
    beta_e = _dot(_split3_lanes(beta), e3b_ref[...])
    gc_e = _dot(gc3, e3a_ref[...])
    col_c = _dot(gc3, e3c_ref[...])
    row_c = _dot_exact_rhs(ones3_ref[...], col_c * dmask_ref[...])
    decay_c = jnp.exp(jnp.minimum(col_c - row_c, 0.0)) * incl_ref[...]
    egc_e = jnp.exp(gc_e)
    lasts = [gc_e[(sq + 1) * q - 1:(sq + 1) * q, :] for sq in range(nseq)]
    eouts = [jnp.exp(lasts[sq] - gc_e[sq * q:(sq + 1) * q]) for sq in range(nseq)]
    eout_e = eouts[0] if nseq == 1 else jnp.concatenate(eouts, axis=0)
    kb = kn * beta_e
    vb = v * beta_e
    kbg = kb * egc_e
    qin = qn * egc_e
    kout = (kn * eout_e).astype(BF16)

    bdmask = bdmask_ref[...]
    kmask = kmask_ref[...]
    mask_dtype = bdmask.dtype

    def stack_rows(x):
        return jnp.concatenate([x[sq * q:(sq + 1) * q] for sq in range(nseq) for _ in range(hpb)], axis=0)

    def masked_bf16(x, mask):
        return (x.astype(mask_dtype) * mask).astype(BF16)

    def block_diag_pieces(m):
        hi = m.astype(BF16)
        lo = (m - hi.astype(F32)).astype(BF16)
        return masked_bf16(stack_rows(hi.astype(mask_dtype)), bdmask), masked_bf16(stack_rows(lo.astype(mask_dtype)), bdmask)

    def lhs_pieces(m):
        hi = m.astype(BF16)
        return hi, (m - hi.astype(F32)).astype(BF16)

    def head_lanes(h):
        return slice(h * GDN_DIM, (h + 1) * GDN_DIM)

    def seq_rows(sq):
        return slice(sq * q, (sq + 1) * q)

    n_sq = q.bit_length() - 1
    assert q == 1 << n_sq and n_sq >= 2
    o_blocks = []
    for jb in range(nhb):
        nl = slice(jb * bw, (jb + 1) * bw)
        cl = slice(jb * MXU_DIM, (jb + 1) * MXU_DIM)
        k_bd = masked_bf16(stack_rows(kn[:, nl].astype(mask_dtype)), kmask)
        prod = _dot_nt(jnp.concatenate([kb[:, nl], qn[:, nl]], axis=0).astype(BF16), k_bd)
        dec = decay_c[:, cl]
        x = -(prod[:rows] * dec * strict_ref[:, cl])
        attn_c = prod[rows:] * dec
        inv = eye_ref[...] + x
        ypow = x
        for k in range(n_sq):
            first, last = k == 0, k == n_sq - 1
            b_hi, b_lo = block_diag_pieces(ypow)
            lhs = ypow if first else (inv if last else jnp.concatenate([inv, ypow], axis=0))
            l_hi, l_lo = lhs_pieces(lhs)
            res = _dot(jnp.concatenate([l_hi, l_hi, l_lo], axis=1), jnp.concatenate([b_hi, b_lo, b_hi], axis=0))
            if first:
                ypow = res
            elif last:
                inv = inv + res
            else:
                inv, ypow = inv + res[:rows], res[rows:]
        i_hi, i_lo = block_diag_pieces(inv)
        segs = [(sq, jb * hpb + hh) for sq in range(nseq) for hh in range(hpb)]
        rhs = jnp.concatenate([jnp.concatenate([vb[seq_rows(sq), head_lanes(h)], kbg[seq_rows(sq), head_lanes(h)]],
                                               axis=1) for sq, h in segs], axis=0).astype(BF16)
        uw = _dot(jnp.concatenate([i_hi, i_lo], axis=1), jnp.concatenate([rhs, rhs], axis=0))
        ws, qs, states = [], [], []
        for i, (sq, h) in enumerate(segs):
            s = s_ref[sq, h]
            states.append(s)
            wq = jnp.concatenate([uw[i * q:(i + 1) * q, GDN_DIM:], qin[seq_rows(sq), head_lanes(h)]], axis=0)
            r = _dot(wq.astype(BF16), s.astype(BF16))
            ws.append(r[:q])
            qs.append(r[q:])
        v_new = uw[:, :GDN_DIM] - jnp.concatenate(ws, axis=0)
        v_bf = v_new.astype(BF16)
        v_rep = masked_bf16(jnp.concatenate([v_bf.astype(mask_dtype)] * hpb, axis=1), kmask)
        qs_rows = [jnp.concatenate(qs[sq * hpb:(sq + 1) * hpb], axis=1) for sq in range(nseq)]
        qs_nat = qs_rows[0] if nseq == 1 else jnp.concatenate(qs_rows, axis=0)
        o_blocks.append(qs_nat + _dot(attn_c.astype(BF16), v_rep))
        for i, (sq, h) in enumerate(segs):
            ge = jnp.exp(lasts[sq][:, head_lanes(h)])
            s_ref[sq, h] = states[i] * ge + _dot_tn(kout[seq_rows(sq), head_lanes(h)], v_bf[i * q:(i + 1) * q])

    o = o_blocks[0] if nhb == 1 else jnp.concatenate(o_blocks, axis=1)
    gate = gate_ref[r0:r0 + rows]
    nw = nw_ref[...]
    outs = [_rms(o[:, head_lanes(h)], nw) * _silu(gate[:, head_lanes(h)]) for h in range(GDN_HEADS)]
    o_ref[r0:r0 + rows] = jnp.concatenate(outs, axis=1).astype(o_ref.dtype)


def _gdn_kernel(qkv_ref, gate_ref, sm_ref, s0_ref, conv0_ref, cw_ref, dtb_ref, a_ref, nw_ref,
                l3_ref, ones3_ref, e3b_ref, e3a_ref, e3c_ref, dmask_ref, incl_ref, strict_ref, eye_ref,
                bdmask_ref, kmask_ref,
                o_ref, sout_ref, convout_ref, s_ref, tail_ref, *, q, nseq, hpb, cps):
    c = pl.program_id(1)
    nc = pl.num_programs(1)
    nhb = GDN_HEADS // hpb
    rows = nseq * q
    bw = hpb * GDN_DIM
    n_sq = q.bit_length() - 1
    assert q == 1 << n_sq and n_sq >= 2

    @pl.when(c == 0)
    def _():
        s_ref[...] = s0_ref[...]
        tail_ref[...] = conv0_ref[...]

    bdmask = bdmask_ref[...]
    kmask = kmask_ref[...]
    mask_dtype = bdmask.dtype

    def head_lanes(h):
        return slice(h * GDN_DIM, (h + 1) * GDN_DIM)

    def seq_rows(sq):
        return slice(sq * q, (sq + 1) * q)

    def stack_rows(x):
        return jnp.concatenate([x[seq_rows(sq)] for sq in range(nseq) for _ in range(hpb)], axis=0)

    def masked_bf16(x, mask):
        return (x.astype(mask_dtype) * mask).astype(BF16)

    def block_diag_pieces(m):
        hi = m.astype(BF16)
        lo = (m - hi.astype(F32)).astype(BF16)
        return (masked_bf16(stack_rows(hi.astype(mask_dtype)), bdmask),
                masked_bf16(stack_rows(lo.astype(mask_dtype)), bdmask))

    def lhs_pieces(m):
        hi = m.astype(BF16)
        return hi, (m - hi.astype(F32)).astype(BF16)

    def l2n(x):
        parts = []
        for h in range(GDN_HEADS):
            xh = x[:, head_lanes(h)]
            parts.append(xh * lax.rsqrt(jnp.sum(xh * xh, axis=-1, keepdims=True) + L2_EPS))
        return jnp.concatenate(parts, axis=1)

    def chunk_local(ci):
        r0 = ci * rows
        acts = []
        for sq in range(nseq):
            if ci == 0:
                ext = jnp.concatenate([tail_ref[sq], qkv_ref[seq_rows(sq)]], axis=0)
            else:
                ext = qkv_ref[r0 - SUBLANES:r0 + q]
            acts.append(_silu(_causal_conv(ext, cw_ref, q)))
            if ci == cps - 1:
                tail_ref[sq] = ext[q:q + SUBLANES]
                convout_ref[sq] = ext[q + SUBLANES - (CONV_WIDTH - 1):q + SUBLANES]
        act = acts[0] if nseq == 1 else jnp.concatenate(acts, axis=0)
        qn = l2n(act[:, :GDN_WIDTH]) * (GDN_DIM ** -0.5)
        kn = l2n(act[:, GDN_WIDTH:2 * GDN_WIDTH])
        v = act[:, 2 * GDN_WIDTH:]
        sm = sm_ref[r0:r0 + rows]
        beta = jax.nn.sigmoid(sm)
        g = a_ref[...] * _softplus(sm + dtb_ref[...])
        gc = _dot_exact_rhs(l3_ref[...], g)
        gc3 = _split3_lanes(gc)
        beta_e = _dot(_split3_lanes(beta), e3b_ref[...])
        gc_e = _dot(gc3, e3a_ref[...])
        col_c = _dot(gc3, e3c_ref[...])
        row_c = _dot_exact_rhs(ones3_ref[...], col_c * dmask_ref[...])
        egc_e = jnp.exp(gc_e)
        lasts = [gc_e[(sq + 1) * q - 1:(sq + 1) * q, :] for sq in range(nseq)]
        eouts = [jnp.exp(lasts[sq] - gc_e[seq_rows(sq)]) for sq in range(nseq)]
        kb = kn * beta_e
        return dict(
            qn=qn, kn=kn, kb=kb, vb=v * beta_e, kbg=kb * egc_e, qin=qn * egc_e,
            kout=(kn * (eouts[0] if nseq == 1 else jnp.concatenate(eouts, axis=0))).astype(BF16),
            decay_c=jnp.exp(jnp.minimum(col_c - row_c, 0.0)) * incl_ref[...],
            ge=[jnp.exp(last) for last in lasts])

    def chain_init(loc, jb):
        nl = slice(jb * bw, (jb + 1) * bw)
        cl = slice(jb * MXU_DIM, (jb + 1) * MXU_DIM)
        k_bd = masked_bf16(stack_rows(loc["kn"][:, nl].astype(mask_dtype)), kmask)
        prod = _dot_nt(jnp.concatenate([loc["kb"][:, nl], loc["qn"][:, nl]], axis=0).astype(BF16), k_bd)
        dec = loc["decay_c"][:, cl]
        x = -(prod[:rows] * dec * strict_ref[:, cl])
        return dict(inv=eye_ref[...] + x, ypow=x, attn=(prod[rows:] * dec).astype(BF16))

    def double_step(st, k):
        first, last = k == 0, k == n_sq - 1
        inv, ypow = st["inv"], st["ypow"]
        b_hi, b_lo = block_diag_pieces(ypow)
        lhs = ypow if first else (inv if last else jnp.concatenate([inv, ypow], axis=0))
        l_hi, l_lo = lhs_pieces(lhs)
        res = _dot(jnp.concatenate([l_hi, l_hi, l_lo], axis=1), jnp.concatenate([b_hi, b_lo, b_hi], axis=0))
        if first:
            return dict(st, ypow=res)
        if last:
            return dict(st, inv=inv + res)
        return dict(st, inv=inv + res[:rows], ypow=res[rows:])

    def solve(loc, jb, inv):
        i_hi, i_lo = block_diag_pieces(inv)
        segs = [(sq, jb * hpb + hh) for sq in range(nseq) for hh in range(hpb)]
        rhs = jnp.concatenate([jnp.concatenate([loc["vb"][seq_rows(sq), head_lanes(h)],
                                                loc["kbg"][seq_rows(sq), head_lanes(h)]], axis=1)
                               for sq, h in segs], axis=0).astype(BF16)
        return _dot(jnp.concatenate([i_hi, i_lo], axis=1), jnp.concatenate([rhs, rhs], axis=0))

    def state_step(loc, jb, uw, attn):
        segs = [(sq, jb * hpb + hh) for sq in range(nseq) for hh in range(hpb)]
        ws, qs, states = [], [], []
        for i, (sq, h) in enumerate(segs):
            s = s_ref[sq, h]
            states.append(s)
            wq = jnp.concatenate([uw[i * q:(i + 1) * q, GDN_DIM:], loc["qin"][seq_rows(sq), head_lanes(h)]], axis=0)
            r = _dot(wq.astype(BF16), s.astype(BF16))
            ws.append(r[:q])
            qs.append(r[q:])
        v_bf = (uw[:, :GDN_DIM] - jnp.concatenate(ws, axis=0)).astype(BF16)
        v_rep = masked_bf16(jnp.concatenate([v_bf.astype(mask_dtype)] * hpb, axis=1), kmask)
        qs_rows = [jnp.concatenate(qs[sq * hpb:(sq + 1) * hpb], axis=1) for sq in range(nseq)]
        qs_nat = qs_rows[0] if nseq == 1 else jnp.concatenate(qs_rows, axis=0)
        for i, (sq, h) in enumerate(segs):
            s_ref[sq, h] = (states[i] * loc["ge"][sq][:, head_lanes(h)]
                            + _dot_tn(loc["kout"][seq_rows(sq), head_lanes(h)], v_bf[i * q:(i + 1) * q]))
        return qs_nat + _dot(attn, v_rep)

    local = [chunk_local(ci) for ci in range(cps)]
    chains = [(ci, jb) for ci in range(cps) for jb in range(nhb)]
    st = [chain_init(local[ci], jb) for ci, jb in chains]
    for k in range(n_sq):
        st = [double_step(s, k) for s in st]
    uws = [solve(local[ci], jb, s["inv"]) for (ci, jb), s in zip(chains, st)]
    nw = nw_ref[...]
    for ci in range(cps):
        o_blocks = [state_step(local[ci], jb, uws[ci * nhb + jb], st[ci * nhb + jb]["attn"]) for jb in range(nhb)]
        o = o_blocks[0] if nhb == 1 else jnp.concatenate(o_blocks, axis=1)
        gate = gate_ref[ci * rows:(ci + 1) * rows]
        outs = [_rms(o[:, head_lanes(h)], nw) * _silu(gate[:, head_lanes(h)]) for h in range(GDN_HEADS)]
        o_ref[ci * rows:(ci + 1) * rows] = jnp.concatenate(outs, axis=1).astype(o_ref.dtype)

    @pl.when(c == nc - 1)
    def _():
        sout_ref[...] = s_ref[...]


def _gdn_call(qkv, gate, small, s0, conv0, cw, dtb, a_pad, nw, b, l, q):
    nseq = max(1, MXU_DIM // (GDN_HEADS * q))
    hpb = MXU_DIM // (nseq * q)
    consts = _gdn_constants(q, nseq, hpb)
    cps = SCAN_UNITS_PER_STEP if nseq == 1 and (l // q) % SCAN_UNITS_PER_STEP == 0 else 1
    nc = l // (q * cps)
    assert b % nseq == 0 and (nseq == 1 or nc == 1)
    t = b * l
    rows = nseq * q * cps
    row = lambda i, c: (i * nc + c, 0)
    params = [cw, dtb, a_pad, nw] + [consts[k] for k in
                                     ("l3", "ones3", "e3b", "e3a", "e3c", "dmask", "incl", "strict", "eye",
                                      "bdmask", "kmask")]
    s_shape = (GDN_HEADS, GDN_DIM, GDN_DIM)
    return pl.pallas_call(
        functools.partial(_gdn_kernel, q=q, nseq=nseq, hpb=hpb, cps=cps),
        grid=(b // nseq, nc),
        in_specs=[pl.BlockSpec((rows, GDN_CONV_DIM), row),
                  pl.BlockSpec((rows, GDN_WIDTH), row),
                  pl.BlockSpec((rows, LANES), row),
                  pl.BlockSpec((nseq,) + s_shape, lambda i, c: (i, 0, 0, 0)),
                  pl.BlockSpec((nseq, SUBLANES, GDN_CONV_DIM), lambda i, c: (i, 0, 0))]
                 + [_const_spec(p.shape) for p in params],
        out_specs=[pl.BlockSpec((rows, GDN_WIDTH), row),
                   pl.BlockSpec((nseq,) + s_shape, lambda i, c: (i, 0, 0, 0)),
                   pl.BlockSpec((nseq, CONV_WIDTH - 1, GDN_CONV_DIM), lambda i, c: (i, 0, 0))],
        out_shape=[jax.ShapeDtypeStruct((t, GDN_WIDTH), F32),
                   jax.ShapeDtypeStruct((b,) + s_shape, F32),
                   jax.ShapeDtypeStruct((b, CONV_WIDTH - 1, GDN_CONV_DIM), F32)],
        scratch_shapes=[pltpu.VMEM((nseq,) + s_shape, F32),
                        pltpu.VMEM((nseq, SUBLANES, GDN_CONV_DIM), F32)],
        compiler_params=pltpu.CompilerParams(dimension_semantics=("arbitrary", "arbitrary"),
                                             vmem_limit_bytes=VMEM_LIMIT),
        name="gdn_scan",
    )(qkv, gate, small, s0, conv0, *params)


def _outffn_kernel(x_ref, y_ref, o_ref, g1_ref, sh2_ref, sc2_ref, g2_ref, nwm_ref, nwf_ref, nwp_ref,
                   wout_ref, wgu_ref, wdown_ref, out_ref, *, ff_chunk):
    x = x_ref[...]
    shape3 = x.shape
    tm = shape3[0] * shape3[1]
    m = (_dot(y_ref[...].astype(BF16), wout_ref[:SSD_WIDTH, :])
         + _dot(o_ref[...].astype(BF16), wout_ref[SSD_WIDTH:, :]))
    x1 = x + g1_ref[...] * _rms(m, nwm_ref[...]).reshape(shape3)
    h2 = _rms(x1, nwf_ref[...]) * (1.0 + sc2_ref[...]) + sh2_ref[...]
    h2 = h2.reshape(tm, shape3[2]).astype(BF16)
    f = jnp.zeros((tm, shape3[2]), F32)
    for j in range(D_FF // ff_chunk):
        gj = _dot(h2, wgu_ref[:, j * ff_chunk:(j + 1) * ff_chunk])
        uj = _dot(h2, wgu_ref[:, D_FF + j * ff_chunk:D_FF + (j + 1) * ff_chunk])
        f = f + _dot((_silu(gj) * uj).astype(BF16), wdown_ref[j * ff_chunk:(j + 1) * ff_chunk, :])
    out_ref[...] = x1 + g2_ref[...] * _rms(f, nwp_ref[...]).reshape(shape3)


def _outffn_call(x, y, o, mod3, nw_mix_post, nw_ffn_pre, nw_ffn_post, w_out, w_gu, w_down, bb, lb):
    b, l, d = x.shape
    nl = l // lb
    tm = bb * lb
    row = lambda i, j: (i * nl + j, 0)
    mod_spec = lambda k: pl.BlockSpec((bb, 1, d), lambda i, j: (i, 0, k))
    return pl.pallas_call(
        functools.partial(_outffn_kernel, ff_chunk=D_FF // 2),
        grid=(b // bb, nl),
        in_specs=[pl.BlockSpec((bb, lb, d), lambda i, j: (i, j, 0)),
                  pl.BlockSpec((tm, SSD_WIDTH), row),
                  pl.BlockSpec((tm, GDN_WIDTH), row),
                  mod_spec(2), mod_spec(3), mod_spec(4), mod_spec(5),
                  _const_spec((1, d)), _const_spec((1, d)), _const_spec((1, d)),
                  _const_spec(w_out.shape), _const_spec(w_gu.shape), _const_spec(w_down.shape)],
        out_specs=pl.BlockSpec((bb, lb, d), lambda i, j: (i, j, 0)),
        out_shape=jax.ShapeDtypeStruct((b, l, d), F32),
        compiler_params=pltpu.CompilerParams(dimension_semantics=("arbitrary", "arbitrary"),
                                             vmem_limit_bytes=VMEM_LIMIT),
        name="outproj_ffn",
    )(x, y, o, mod3, mod3, mod3, mod3, nw_mix_post, nw_ffn_pre, nw_ffn_post, w_out, w_gu, w_down)


def _pad_lanes(v, lane0):
    out = jnp.zeros((1, LANES), F32)
    return out.at[0, lane0:lane0 + v.shape[0]].set(v.astype(F32))


def _conv_hist(buf):
    return jnp.pad(buf, ((0, 0), (SUBLANES - (CONV_WIDTH - 1), 0), (0, 0)))


def _layer(x, mod, ssd_h0, ssd_conv0, gdn_s0, gdn_conv0, p, q, bb, lb):
    b, l, d = x.shape
    mod3 = mod.reshape(b, 1, 6 * d)
    z, xbc, qkv, gate, small = _inproj_call(x, mod3, p["norm_mix_pre"], p["w_big"], p["w_small"], bb, lb)
    y, ssd_h, ssd_conv = _ssd_call(
        xbc, z, small, ssd_h0.reshape(b, SSD_WIDTH, SSD_STATE), _conv_hist(ssd_conv0),
        p["ssd_conv_w"], p["ssd_conv_b"], p["ssd_dtb"], p["ssd_a"], p["ssd_d"], p["ssd_norm_w"], b, l, q)
    o, gdn_s, gdn_conv = _gdn_call(
        qkv, gate, small, gdn_s0, _conv_hist(gdn_conv0),
        p["gdn_conv_w"], p["gdn_dtb"], p["gdn_a"], p["gdn_norm_w"], b, l, q)
    out = _outffn_call(x, y, o, mod3, p["norm_mix_post"], p["norm_ffn_pre"], p["norm_ffn_post"],
                       p["w_out"], p["w_gu"], p["w_down"], bb, lb)
    return out, ssd_h.reshape(ssd_h0.shape), ssd_conv, gdn_s, gdn_conv


def _layer_params(l, w_ada, b_ada, norm_mix_pre, norm_mix_post, norm_ffn_pre, norm_ffn_post, w_in, ssd_conv_w,
                  ssd_conv_b, ssd_dt_bias, ssd_A_log, ssd_D, ssd_norm_w, gdn_conv_w, gdn_dt_bias, gdn_A_log,
                  gdn_norm_w, w_out, w_gate_up, w_down):
    w = w_in[l]
    o_z, o_xbc = 0, SSD_WIDTH
    o_dt = o_xbc + SSD_CONV_DIM
    o_qkv = o_dt + SSD_HEADS
    o_gate = o_qkv + GDN_CONV_DIM
    o_beta = o_gate + GDN_WIDTH
    o_alpha = o_beta + GDN_HEADS
    w_big = jnp.concatenate([w[:, o_z:o_dt], w[:, o_qkv:o_beta]], axis=1).astype(BF16)
    w_small = jnp.concatenate([w[:, o_dt:o_qkv], w[:, o_beta:o_alpha + GDN_HEADS],
                               jnp.zeros((w.shape[0], LANES - SSD_HEADS - 2 * GDN_HEADS), w.dtype)],
                              axis=1).astype(BF16)
    row = lambda v: v.reshape(1, -1).astype(F32)
    return dict(
        w_ada=w_ada[l], b_ada=b_ada[l],
        norm_mix_pre=row(norm_mix_pre[l]), norm_mix_post=row(norm_mix_post[l]),
        norm_ffn_pre=row(norm_ffn_pre[l]), norm_ffn_post=row(norm_ffn_post[l]),
        w_big=w_big, w_small=w_small,
        ssd_conv_w=ssd_conv_w[l], ssd_conv_b=row(ssd_conv_b[l]),
        ssd_dtb=_pad_lanes(ssd_dt_bias[l], DT_LANE0),
        ssd_a=_pad_lanes(-jnp.exp(ssd_A_log[l].astype(F32)), DT_LANE0),
        ssd_d=row(jnp.repeat(ssd_D[l], SSD_HEADDIM)),
        ssd_norm_w=row(ssd_norm_w[l]),
        gdn_conv_w=gdn_conv_w[l],
        gdn_dtb=_pad_lanes(gdn_dt_bias[l], ALPHA_LANE0),
        gdn_a=_pad_lanes(-jnp.exp(gdn_A_log[l].astype(F32)), ALPHA_LANE0),
        gdn_norm_w=row(gdn_norm_w[l]),
        w_out=w_out[l].astype(BF16), w_gu=w_gate_up[l].astype(BF16), w_down=w_down[l].astype(BF16),
    )


def kernel(x_prompt, x_sample, c_prompt, c_sample, state_ssd, state_ssd_conv, state_gdn, state_gdn_conv, w_ada, b_ada, norm_mix_pre, norm_mix_post, norm_ffn_pre, norm_ffn_post, w_in, ssd_conv_w, ssd_conv_b, ssd_dt_bias, ssd_A_log, ssd_D, ssd_norm_w, gdn_conv_w, gdn_dt_bias, gdn_A_log, gdn_norm_w, w_out, w_gate_up, w_down):
    depth = w_in.shape[0]
    bp, lp, _ = x_prompt.shape
    bs, ls, _ = x_sample.shape
    yp, ys = x_prompt, x_sample
    outs = [[] for _ in range(8)]
    for l in range(depth):
        p = _layer_params(l, w_ada, b_ada, norm_mix_pre, norm_mix_post, norm_ffn_pre, norm_ffn_post, w_in,
                          ssd_conv_w, ssd_conv_b, ssd_dt_bias, ssd_A_log, ssd_D, ssd_norm_w, gdn_conv_w,
                          gdn_dt_bias, gdn_A_log, gdn_norm_w, w_out, w_gate_up, w_down)
        mod = _mod_call(jnp.concatenate([c_prompt, c_sample], axis=0), p["w_ada"], p["b_ada"])
        zeros = lambda a: jnp.zeros((bp,) + a.shape[2:], a.dtype)
        yp, a0, a1, a2, a3 = _layer(yp, mod[:bp], zeros(state_ssd), zeros(state_ssd_conv), zeros(state_gdn),
                                    zeros(state_gdn_conv), p, q=min(CHUNK, lp), bb=1, lb=min(ROW_TILE, lp))
        ys, b0, b1, b2, b3 = _layer(ys, mod[bp:], state_ssd[l], state_ssd_conv[l], state_gdn[l],
                                    state_gdn_conv[l], p, q=min(CHUNK, ls), bb=min(ROW_TILE // ls, bs), lb=ls)
        for lst, v in zip(outs, (a0, a1, a2, a3, b0, b1, b2, b3)):
            lst.append(v)
    return (yp, ys) + tuple(jnp.stack(v) for v in outs)
```

```python
import functools

import numpy as np
import jax
import jax.numpy as jnp
from jax import lax
from jax.experimental import pallas as pl
from jax.experimental.pallas import tpu as pltpu

F32 = jnp.float32
BF16 = jnp.bfloat16

D_MODEL = 1024
SSD_HEADS = 16
SSD_HEADDIM = 64
SSD_GROUPS = 2
SSD_STATE = 128
SSD_WIDTH = SSD_HEADS * SSD_HEADDIM
SSD_CONV_DIM = SSD_WIDTH + 2 * SSD_GROUPS * SSD_STATE
GDN_HEADS = 8
GDN_DIM = 128
GDN_WIDTH = GDN_HEADS * GDN_DIM
GDN_CONV_DIM = 3 * GDN_WIDTH
CONV_WIDTH = 4
CHUNK = 64
D_FF = 2816
NORM_EPS = 1e-6
L2_EPS = 1e-6
LANES = 128
SUBLANES = 8
MXU_DIM = 256
DT_LANE0, BETA_LANE0, ALPHA_LANE0 = 0, SSD_HEADS, SSD_HEADS + GDN_HEADS
VMEM_LIMIT = 56 * 1024 * 1024
ROW_TILE = 256
SCAN_UNITS_PER_STEP = 4
CONV_COLS = 512


def _dot(a, b):
    return jnp.dot(a, b, preferred_element_type=F32)


def _dot_nt(a, b):
    return lax.dot_general(a, b, (((1,), (1,)), ((), ())), preferred_element_type=F32)


def _dot_tn(a, b):
    return lax.dot_general(a, b, (((0,), (0,)), ((), ())), preferred_element_type=F32)


def _bf16_pieces(x):
    hi = x.astype(BF16)
    r1 = x - hi.astype(F32)
    mid = r1.astype(BF16)
    lo = (r1 - mid.astype(F32)).astype(BF16)
    return hi, mid, lo


def _split3_lanes(x):
    hi, mid, lo = _bf16_pieces(x)
    return jnp.concatenate([hi.astype(F32), mid.astype(F32), lo.astype(F32)], axis=1).astype(BF16)


def _dot_exact_rhs(a_bf16, x):
    hi, mid, lo = _bf16_pieces(x)
    return _dot(a_bf16, hi) + _dot(a_bf16, mid) + _dot(a_bf16, lo)


def _silu(x):
    return x * jax.nn.sigmoid(x)


def _softplus(x):
    return jnp.maximum(x, 0.0) + jnp.log1p(jnp.exp(-jnp.abs(x)))


def _rms(x, w):
    return x * lax.rsqrt(jnp.mean(x * x, axis=-1, keepdims=True) + NORM_EPS) * w


def _const_spec(shape):
    nd = len(shape)
    return pl.BlockSpec(shape, lambda *_: (0,) * nd, pipeline_mode=pl.Buffered(1))


def _params2(sem=("arbitrary", "arbitrary")):
    return pltpu.CompilerParams(dimension_semantics=sem, vmem_limit_bytes=VMEM_LIMIT)


def _mod_kernel(c_ref, w_ref, b_ref, o_ref):
    a = _silu(c_ref[...]).astype(BF16)
    o_ref[...] = _dot(a, w_ref[...].astype(BF16)) + b_ref[...]


def _mod_call(c_all, w_ada, b_ada):
    m, d = c_all.shape
    n = w_ada.shape[1]
    tn = 512
    return pl.pallas_call(
        _mod_kernel,
        grid=(n // tn,),
        in_specs=[pl.BlockSpec((m, d), lambda j: (0, 0)),
                  pl.BlockSpec((d, tn), lambda j: (0, j)),
                  pl.BlockSpec((1, tn), lambda j: (0, j))],
        out_specs=pl.BlockSpec((m, tn), lambda j: (0, j)),
        out_shape=jax.ShapeDtypeStruct((m, n), F32),
        compiler_params=_params2(("arbitrary",)),
        name="adaln_mod",
    )(c_all, w_ada, b_ada.reshape(1, n))


def _proj_conv_silu(h, w_ref, w_col0, width, bb, l, tail_ref, convout_ref, cw_ref, cb_ref, out_ref, n_norm, norm_scale):
    tm = bb * l
    for k in range(width // CONV_COLS):
        cols = slice(k * CONV_COLS, (k + 1) * CONV_COLS)
        raw = _dot(h, w_ref[:, w_col0 + k * CONV_COLS:w_col0 + (k + 1) * CONV_COLS]).reshape(bb, l, CONV_COLS)
        ext = jnp.concatenate([tail_ref[:, :, cols], raw], axis=1)
        acc = ext[:, SUBLANES:SUBLANES + l] * cw_ref[CONV_WIDTH - 1:CONV_WIDTH, cols]
        for j in range(1, CONV_WIDTH):
            acc = acc + ext[:, SUBLANES - j:SUBLANES - j + l] * cw_ref[CONV_WIDTH - 1 - j:CONV_WIDTH - j, cols]
        tail_ref[:, :, cols] = ext[:, l:l + SUBLANES]
        convout_ref[:, :, cols] = ext[:, l + SUBLANES - (CONV_WIDTH - 1):l + SUBLANES]
        if cb_ref is not None:
            acc = acc + cb_ref[:, cols]
        act = _silu(acc).reshape(tm, CONV_COLS)
        for hh in range(CONV_COLS // LANES):
            head = k * (CONV_COLS // LANES) + hh
            xh = act[:, hh * LANES:(hh + 1) * LANES]
            if head < n_norm:
                xh = xh * (lax.rsqrt(jnp.sum(xh * xh, axis=-1, keepdims=True) + L2_EPS) * norm_scale[head])
            out_ref[:, head * LANES:(head + 1) * LANES] = xh


def _inproj_kernel(x_ref, sh_ref, sc_ref, nw_ref, wbig_ref, wsm_ref, sconv0_ref, gconv0_ref,
                   scw_ref, scb_ref, gcw_ref,
                   z_ref, xbc_ref, qkv_ref, gate_ref, sm_ref, sconv_ref, gconv_ref, stail_ref, gtail_ref):
    @pl.when(pl.program_id(1) == 0)
    def _():
        for tail_ref, conv0_ref in ((stail_ref, sconv0_ref), (gtail_ref, gconv0_ref)):
            tail_ref[...] = jnp.zeros(tail_ref.shape, F32)
            tail_ref[:, SUBLANES - (CONV_WIDTH - 1):, :] = conv0_ref[...]

    x = x_ref[...]
    bb, l, d = x.shape
    tm = bb * l
    h = _rms(x, nw_ref[...]) * (1.0 + sc_ref[...]) + sh_ref[...]
    h = h.reshape(tm, d).astype(BF16)
    c_xbc = SSD_WIDTH
    c_qkv = c_xbc + SSD_CONV_DIM
    c_gate = c_qkv + GDN_CONV_DIM
    z_ref[...] = _dot(h, wbig_ref[:, :c_xbc])
    gate_ref[...] = _dot(h, wbig_ref[:, c_gate:])
    sm_ref[...] = _dot(h, wsm_ref[...])
    _proj_conv_silu(h, wbig_ref, c_xbc, SSD_CONV_DIM, bb, l, stail_ref, sconv_ref, scw_ref, scb_ref, xbc_ref, 0, ())
    qk_scale = (GDN_DIM ** -0.5,) * GDN_HEADS + (1.0,) * GDN_HEADS
    _proj_conv_silu(h, wbig_ref, c_qkv, GDN_CONV_DIM, bb, l, gtail_ref, gconv_ref, gcw_ref, None, qkv_ref,
                    2 * GDN_HEADS, qk_scale)


def _inproj_call(x, mod3, norm_w, w_big, w_small, sconv0, gconv0, scw, scb, gcw, bb, lb):
    b, l, d = x.shape
    nl = l // lb
    tm = bb * lb
    t = b * l
    row = lambda i, j: (i * nl + j, 0)
    seq = lambda i, j: (i, 0, 0)
    widths = (SSD_WIDTH, SSD_CONV_DIM, GDN_CONV_DIM, GDN_WIDTH, LANES)
    hist = CONV_WIDTH - 1
    return pl.pallas_call(
        _inproj_kernel,
        grid=(b // bb, nl),
        in_specs=[pl.BlockSpec((bb, lb, d), lambda i, j: (i, j, 0)),
                  pl.BlockSpec((bb, 1, d), lambda i, j: (i, 0, 0)),
                  pl.BlockSpec((bb, 1, d), lambda i, j: (i, 0, 1)),
                  _const_spec((1, d)),
                  _const_spec(w_big.shape),
                  _const_spec(w_small.shape),
                  pl.BlockSpec((bb, hist, SSD_CONV_DIM), seq),
                  pl.BlockSpec((bb, hist, GDN_CONV_DIM), seq),
                  _const_spec(scw.shape), _const_spec(scb.shape), _const_spec(gcw.shape)],
        out_specs=[pl.BlockSpec((tm, w), row) for w in widths]
                  + [pl.BlockSpec((bb, hist, SSD_CONV_DIM), seq), pl.BlockSpec((bb, hist, GDN_CONV_DIM), seq)],
        out_shape=[jax.ShapeDtypeStruct((t, w), F32) for w in widths]
                  + [jax.ShapeDtypeStruct((b, hist, SSD_CONV_DIM), F32),
                     jax.ShapeDtypeStruct((b, hist, GDN_CONV_DIM), F32)],
        scratch_shapes=[pltpu.VMEM((bb, SUBLANES, SSD_CONV_DIM), F32),
                        pltpu.VMEM((bb, SUBLANES, GDN_CONV_DIM), F32)],
        compiler_params=_params2(),
        name="inproj",
    )(x, mod3, mod3, norm_w, w_big, w_small, sconv0, gconv0, scw, scb, gcw)


def _ssd_constants(q, nu):
    hq = SSD_HEADS * q
    hp = LANES // q
    w = hp * SSD_HEADDIM
    e_p = np.zeros((LANES, SSD_WIDTH), np.float32)
    e_q = np.zeros((LANES, hq), np.float32)
    for r in range(SSD_HEADS):
        e_p[DT_LANE0 + r, r * SSD_HEADDIM:(r + 1) * SSD_HEADDIM] = 1.0
        e_q[DT_LANE0 + r, r * q:(r + 1) * q] = 1.0
    tri = np.tril(np.ones((q, q), np.float32))
    gmask = np.zeros((hq, 2 * SSD_STATE), np.float32)
    half = hq // SSD_GROUPS
    gmask[:half, :SSD_STATE] = 1.0
    gmask[half:, SSD_STATE:] = 1.0
    xmask = np.kron(np.eye(hp, dtype=np.float32), np.ones((q, SSD_HEADDIM), np.float32))
    unit_eye = np.eye(nu, dtype=np.float32)
    return dict(
        l3=jnp.asarray(np.kron(unit_eye, tri), BF16),
        ones3=jnp.asarray(np.kron(unit_eye, np.ones((q, q), np.float32)), BF16),
        e3p=jnp.asarray(np.tile(e_p, (3, 1)), BF16),
        e3q=jnp.asarray(np.tile(e_q, (3, 1)), BF16),
        dmask=jnp.asarray(np.tile(np.eye(q, dtype=np.float32), (nu, SSD_HEADS))),
        causal=jnp.asarray(np.tile(tri, (nu, SSD_HEADS))),
        gmask=jnp.asarray(gmask),
        xmask=jnp.asarray(xmask),
    ), hp, w


def _ssd_kernel(xbc_ref, z_ref, sm_ref, h0_ref, dtb_ref, a_ref, d_ref, nw_ref,
                l3_ref, ones3_ref, e3p_ref, e3q_ref, dmask_ref, causal_ref, gmask_ref, xmask_ref,
                y_ref, hout_ref, ht_ref, *, q, hp, w, nseq, cps):
    c = pl.program_id(1)
    nc = pl.num_programs(1)
    nu = nseq * cps
    rows = nu * q
    gw = SSD_WIDTH // SSD_GROUPS

    @pl.when(c == 0)
    def _():
        for sq in range(nseq):
            ht_ref[sq] = h0_ref[sq].T

    def unit_rows(u):
        return slice(u * q, (u + 1) * q)

    xs = xbc_ref[:, :SSD_WIDTH]
    bcat = xbc_ref[:, SSD_WIDTH:SSD_WIDTH + 2 * SSD_STATE]
    ccat = xbc_ref[:, SSD_WIDTH + 2 * SSD_STATE:].astype(BF16)

    dt = _softplus(sm_ref[...] + dtb_ref[...])
    acs = _dot_exact_rhs(l3_ref[...], dt * a_ref[...])
    dec_ends = [jnp.exp(acs[(u + 1) * q - 1:(u + 1) * q, :] - acs[unit_rows(u)]) for u in range(nu)]
    dec_end = dec_ends[0] if nu == 1 else jnp.concatenate(dec_ends, axis=0)
    per_head = jnp.concatenate([dt, dt * dec_end, jnp.exp(acs)], axis=0)
    exp3 = _dot(_split3_lanes(per_head), e3p_ref[...])
    dt_e, dtd_e, eacs_e = exp3[:rows], exp3[rows:2 * rows], exp3[2 * rows:]

    col_all = _dot(_split3_lanes(acs), e3q_ref[...])
    row_all = _dot_exact_rhs(ones3_ref[...], col_all * dmask_ref[...])
    decay = jnp.exp(jnp.minimum(col_all - row_all, 0.0)) * causal_ref[...]
    xdt = xs * dt_e
    xd = (xs * dtd_e).astype(BF16)
    dxs = d_ref[...] * xs
    gmask = gmask_ref[...]
    xmask = xmask_ref[...]

    def cb_all(u):
        if 2 * q == LANES:
            per_group = []
            for g in range(SSD_GROUPS):
                gl = slice(g * SSD_STATE, (g + 1) * SSD_STATE)
                cb = _dot_nt(ccat[unit_rows(u), gl], bcat[unit_rows(u), gl].astype(BF16))
                per_group += [jnp.concatenate([cb, cb], axis=1)] * (SSD_HEADS // SSD_GROUPS // 2)
            return jnp.concatenate(per_group, axis=1)
        brep = (jnp.concatenate([bcat[unit_rows(u)]] * SSD_HEADS, axis=0) * gmask).astype(BF16)
        return _dot_nt(ccat[unit_rows(u)], brep)

    cbs = [cb_all(u) for u in range(nu)]
    m_alls = [(cbs[u] * decay[unit_rows(u)]).astype(BF16) for u in range(nu)]
    y_diags, upds = [], []
    for u in range(nu):
        blocks = []
        for j in range(SSD_WIDTH // w):
            xj = (jnp.concatenate([xdt[unit_rows(u), j * w:(j + 1) * w]] * hp, axis=0) * xmask).astype(BF16)
            blocks.append(_dot(m_alls[u][:, j * LANES:(j + 1) * LANES], xj))
        y_diags.append(blocks[0] if len(blocks) == 1 else jnp.concatenate(blocks, axis=1))
        upds.append(jnp.concatenate(
            [_dot_tn(bcat[unit_rows(u), g * SSD_STATE:(g + 1) * SSD_STATE].astype(BF16),
                     xd[unit_rows(u), g * gw:(g + 1) * gw]) for g in range(SSD_GROUPS)], axis=1))

    nw = nw_ref[...]
    for sq in range(nseq):
        ht = ht_ref[sq]
        for ci in range(cps):
            u = sq * cps + ci
            ht_bf = ht.astype(BF16)
            y_off = jnp.concatenate([_dot(ccat[unit_rows(u), g * SSD_STATE:(g + 1) * SSD_STATE],
                                          ht_bf[:, g * gw:(g + 1) * gw]) for g in range(SSD_GROUPS)], axis=1)
            e_u = eacs_e[unit_rows(u)]
            y = y_diags[u] + y_off * e_u + dxs[unit_rows(u)]
            ht = ht * e_u[q - 1:q, :] + upds[u]
            yz = y * _silu(z_ref[unit_rows(u)])
            outs = [_rms(yz[:, g * gw:(g + 1) * gw], nw[:, g * gw:(g + 1) * gw]) for g in range(SSD_GROUPS)]
            y_ref[unit_rows(u)] = jnp.concatenate(outs, axis=1).astype(y_ref.dtype)
        ht_ref[sq] = ht

    @pl.when(c == nc - 1)
    def _():
        for sq in range(nseq):
            hout_ref[sq] = ht_ref[sq].T


def _scan_units(b, l, q):
    nchunks = l // q
    cps = SCAN_UNITS_PER_STEP if nchunks % SCAN_UNITS_PER_STEP == 0 else 1
    nseq = SCAN_UNITS_PER_STEP if nchunks == 1 and b % SCAN_UNITS_PER_STEP == 0 else 1
    return nseq, cps


def _ssd_call(xbc, z, small, h0, dtb, a_pad, d_e, nw, b, l, q):
    nseq, cps = _scan_units(b, l, q)
    consts, hp, w = _ssd_constants(q, nseq * cps)
    nc = l // (q * cps)
    t = b * l
    rows = nseq * cps * q
    row = lambda i, c: (i * nc + c, 0)
    seq = lambda i, c: (i, 0, 0)
    params = [dtb, a_pad, d_e, nw] + [consts[k] for k in
                                      ("l3", "ones3", "e3p", "e3q", "dmask", "causal", "gmask", "xmask")]
    hp_shape = (SSD_WIDTH, SSD_STATE)
    return pl.pallas_call(
        functools.partial(_ssd_kernel, q=q, hp=hp, w=w, nseq=nseq, cps=cps),
        grid=(b // nseq, nc),
        in_specs=[pl.BlockSpec((rows, SSD_CONV_DIM), row),
                  pl.BlockSpec((rows, SSD_WIDTH), row),
                  pl.BlockSpec((rows, LANES), row),
                  pl.BlockSpec((nseq,) + hp_shape, seq)]
                 + [_const_spec(p.shape) for p in params],
        out_specs=[pl.BlockSpec((rows, SSD_WIDTH), row),
                   pl.BlockSpec((nseq,) + hp_shape, seq)],
        out_shape=[jax.ShapeDtypeStruct((t, SSD_WIDTH), F32),
                   jax.ShapeDtypeStruct((b,) + hp_shape, F32)],
        scratch_shapes=[pltpu.VMEM((nseq, SSD_STATE, SSD_WIDTH), F32)],
        compiler_params=_params2(),
        name="ssd_scan",
    )(xbc, z, small, h0, *params)


def _gdn_constants(q, nseq, hpb, cps):
    assert nseq * hpb * q == MXU_DIM
    nhb = GDN_HEADS // hpb
    rows = nseq * q
    e_beta = np.zeros((LANES, GDN_WIDTH), np.float32)
    e_alpha = np.zeros((LANES, GDN_WIDTH), np.float32)
    e_c = np.zeros((LANES, nhb * MXU_DIM), np.float32)
    for h in range(GDN_HEADS):
        e_beta[BETA_LANE0 + h, h * GDN_DIM:(h + 1) * GDN_DIM] = 1.0
        e_alpha[ALPHA_LANE0 + h, h * GDN_DIM:(h + 1) * GDN_DIM] = 1.0
    jh, seq_c, hh_c, s_c = np.unravel_index(np.arange(nhb * MXU_DIM), (nhb, nseq, hpb, q))
    e_c[ALPHA_LANE0 + jh * hpb + hh_c, np.arange(nhb * MXU_DIM)] = 1.0
    seq_r, l_r = np.unravel_index(np.arange(rows), (nseq, q))
    same_seq = seq_r[:, None] == seq_c[None, :]
    dmask = same_seq & (l_r[:, None] == s_c[None, :])
    incl = same_seq & (l_r[:, None] >= s_c[None, :])
    strict = same_seq & (l_r[:, None] > s_c[None, :])
    seq_s, hh_s, _ = np.unravel_index(np.arange(MXU_DIM), (nseq, hpb, q))
    bdmask = (seq_s[:, None] == seq_c[None, :MXU_DIM]) & (hh_s[:, None] == hh_c[None, :MXU_DIM])
    kmask = hh_s[:, None] == (np.arange(hpb * GDN_DIM) // GDN_DIM)[None, :]
    f = lambda m: jnp.asarray(m.astype(np.float32))
    per_chunk = lambda m: jnp.asarray(np.tile(m.astype(np.float32), (cps, 1)))
    chunk_eye = np.eye(cps)
    mask_dtype = BF16 if q % (2 * SUBLANES) == 0 else F32
    return dict(
        l3=jnp.asarray(np.kron(np.eye(cps * nseq), np.tril(np.ones((q, q)))), BF16),
        ones3=jnp.asarray(np.kron(chunk_eye, np.ones((rows, rows))), BF16),
        e3b=jnp.asarray(np.tile(e_beta, (3, 1)), BF16),
        e3a=jnp.asarray(np.tile(e_alpha, (3, 1)), BF16),
        e3c=jnp.asarray(np.tile(e_c, (3, 1)), BF16),
        dmask=per_chunk(dmask), incl=per_chunk(incl), strict=f(strict), eye=f(dmask[:, :MXU_DIM]),
        bdmask=f(bdmask).astype(mask_dtype), kmask=f(kmask).astype(mask_dtype),
    )


def _gdn_kernel(qkv_ref, gate_ref, sm_ref, s0_ref, dtb_ref, a_ref, nw_ref,
                l3_ref, ones3_ref, e3b_ref, e3a_ref, e3c_ref, dmask_ref, incl_ref, strict_ref, eye_ref,
                bdmask_ref, kmask_ref,
                o_ref, sout_ref, s_ref, *, q, nseq, hpb, cps):
    c = pl.program_id(1)
    nc = pl.num_programs(1)
    nhb = GDN_HEADS // hpb
    rows = nseq * q
    bw = hpb * GDN_DIM
    n_sq = q.bit_length() - 1
    assert q == 1 << n_sq and n_sq >= 2

    @pl.when(c == 0)
    def _():
        s_ref[...] = s0_ref[...]

    bdmask = bdmask_ref[...]
    kmask = kmask_ref[...]
    mask_dtype = bdmask.dtype

    def head_lanes(h):
        return slice(h * GDN_DIM, (h + 1) * GDN_DIM)

    def seq_rows(sq):
        return slice(sq * q, (sq + 1) * q)

    def chunk_rows(ci):
        return slice(ci * rows, (ci + 1) * rows)

    def stack_rows(x):
        return jnp.concatenate([x[seq_rows(sq)] for sq in range(nseq) for _ in range(hpb)], axis=0)

    def masked_bf16(x, mask):
        return (x.astype(mask_dtype) * mask).astype(BF16)

    def block_diag(m_bf16):
        return masked_bf16(stack_rows(m_bf16.astype(mask_dtype)), bdmask)

    def block_diag_pieces(m):
        hi = m.astype(BF16)
        lo = (m - hi.astype(F32)).astype(BF16)
        return block_diag(hi), block_diag(lo)

    def lhs_pieces(m):
        hi = m.astype(BF16)
        return hi, (m - hi.astype(F32)).astype(BF16)

    sm = sm_ref[...]
    beta = jax.nn.sigmoid(sm)
    g = a_ref[...] * _softplus(sm + dtb_ref[...])
    gc = _dot_exact_rhs(l3_ref[...], g)
    gc3 = _split3_lanes(gc)
    beta_e = _dot(_split3_lanes(beta), e3b_ref[...])
    gc_e = _dot(gc3, e3a_ref[...])
    col_c = _dot(gc3, e3c_ref[...])
    row_c = _dot_exact_rhs(ones3_ref[...], col_c * dmask_ref[...])
    decay_c = jnp.exp(jnp.minimum(col_c - row_c, 0.0)) * incl_ref[...]
    egc_e = jnp.exp(gc_e)
    qn = qkv_ref[:, :GDN_WIDTH]
    kn = qkv_ref[:, GDN_WIDTH:2 * GDN_WIDTH]
    kb = kn * beta_e
    vb = qkv_ref[:, 2 * GDN_WIDTH:] * beta_e
    kbg = kb * egc_e
    qin = qn * egc_e

    def chunk_local(ci):
        r0 = ci * rows
        lasts = [gc_e[r0 + (sq + 1) * q - 1:r0 + (sq + 1) * q, :] for sq in range(nseq)]
        eouts = [jnp.exp(lasts[sq] - gc_e[r0 + sq * q:r0 + (sq + 1) * q]) for sq in range(nseq)]
        cr = chunk_rows(ci)
        return dict(
            qn=qn[cr], kn=kn[cr], kb=kb[cr], vb=vb[cr], kbg=kbg[cr], qin=qin[cr], decay_c=decay_c[cr],
            kout=(kn[cr] * (eouts[0] if nseq == 1 else jnp.concatenate(eouts, axis=0))).astype(BF16),
            ge=[jnp.exp(last) for last in lasts])

    def chain_init(loc, jb):
        nl = slice(jb * bw, (jb + 1) * bw)
        cl = slice(jb * MXU_DIM, (jb + 1) * MXU_DIM)
        k_bd = masked_bf16(stack_rows(loc["kn"][:, nl].astype(mask_dtype)), kmask)
        prod = _dot_nt(jnp.concatenate([loc["kb"][:, nl], loc["qn"][:, nl]], axis=0).astype(BF16), k_bd)
        dec = loc["decay_c"][:, cl]
        x = -(prod[:rows] * dec * strict_ref[:, cl])
        return dict(inv=eye_ref[...] + x, ypow=x, attn=(prod[rows:] * dec).astype(BF16))

    def double_step(st, k):
        first, last = k == 0, k == n_sq - 1
        inv, ypow = st["inv"], st["ypow"]
        b_hi, b_lo = block_diag_pieces(ypow)
        lhs = ypow if first else (inv if last else jnp.concatenate([inv, ypow], axis=0))
        l_hi, l_lo = lhs_pieces(lhs)
        res = _dot(jnp.concatenate([l_hi, l_hi, l_lo], axis=1), jnp.concatenate([b_hi, b_lo, b_hi], axis=0))
        if first:
            return dict(st, ypow=res)
        if last:
            return dict(st, inv=inv + res)
        return dict(st, inv=inv + res[:rows], ypow=res[rows:])

    def solve(loc, jb, inv):
        i_hi, i_lo = block_diag_pieces(inv)
        segs = [(sq, jb * hpb + hh) for sq in range(nseq) for hh in range(hpb)]
        rhs = jnp.concatenate([jnp.concatenate([loc["vb"][seq_rows(sq), head_lanes(h)],
                                                loc["kbg"][seq_rows(sq), head_lanes(h)]], axis=1)
                               for sq, h in segs], axis=0).astype(BF16)
        return _dot(jnp.concatenate([i_hi, i_lo], axis=1), jnp.concatenate([rhs, rhs], axis=0))

    def state_step(loc, jb, uw, attn):
        segs = [(sq, jb * hpb + hh) for sq in range(nseq) for hh in range(hpb)]
        ws, qs, states = [], [], []
        for i, (sq, h) in enumerate(segs):
            s = s_ref[sq, h]
            states.append(s)
            wq = jnp.concatenate([uw[i * q:(i + 1) * q, GDN_DIM:], loc["qin"][seq_rows(sq), head_lanes(h)]], axis=0)
            r = _dot(wq.astype(BF16), s.astype(BF16))
            ws.append(r[:q])
            qs.append(r[q:])
        v_bf = (uw[:, :GDN_DIM] - jnp.concatenate(ws, axis=0)).astype(BF16)
        v_rep = masked_bf16(jnp.concatenate([v_bf.astype(mask_dtype)] * hpb, axis=1), kmask)
        qs_rows = [jnp.concatenate(qs[sq * hpb:(sq + 1) * hpb], axis=1) for sq in range(nseq)]
        qs_nat = qs_rows[0] if nseq == 1 else jnp.concatenate(qs_rows, axis=0)
        for i, (sq, h) in enumerate(segs):
            s_ref[sq, h] = (states[i] * loc["ge"][sq][:, head_lanes(h)]
                            + _dot_tn(loc["kout"][seq_rows(sq), head_lanes(h)], v_bf[i * q:(i + 1) * q]))
        return qs_nat + _dot(attn, v_rep)

    local = [chunk_local(ci) for ci in range(cps)]
    chains = [(ci, jb) for ci in range(cps) for jb in range(nhb)]
    st = [chain_init(local[ci], jb) for ci, jb in chains]
    for k in range(n_sq):
        st = [double_step(s, k) for s in st]
    uws =[solve(local[ci], jb, s["inv"]) for (ci, jb), s in zip(chains, st)]
    nw = nw_ref[...]
    for ci in range(cps):
        o_blocks = [state_step(local[ci], jb, uws[ci * nhb + jb], st[ci * nhb + jb]["attn"]) for jb in range(nhb)]
        o = o_blocks[0] if nhb == 1 else jnp.concatenate(o_blocks, axis=1)
        gate = gate_ref[chunk_rows(ci)]
        outs = [_rms(o[:, head_lanes(h)], nw) * _silu(gate[:, head_lanes(h)]) for h in range(GDN_HEADS)]
        o_ref[chunk_rows(ci)] = jnp.concatenate(outs, axis=1).astype(o_ref.dtype)

    @pl.when(c == nc - 1)
    def _():
        sout_ref[...] = s_ref[...]


def _gdn_call(qkv, gate, small, s0, dtb, a_pad, nw, b, l, q):
    nseq = max(1, MXU_DIM // (GDN_HEADS * q))
    hpb = MXU_DIM // (nseq * q)
    cps = SCAN_UNITS_PER_STEP if nseq == 1 and (l // q) % SCAN_UNITS_PER_STEP == 0 else 1
    consts = _gdn_constants(q, nseq, hpb, cps)
    nc = l // (q * cps)
    assert b % nseq == 0 and (nseq == 1 or nc == 1)
    t = b * l
    rows = nseq * q * cps
    row = lambda i, c: (i * nc + c, 0)
    seq = lambda i, c: (i, 0, 0, 0)
    params = [dtb, a_pad, nw] + [consts[k] for k in
                                 ("l3", "ones3", "e3b", "e3a", "e3c", "dmask", "incl", "strict", "eye",
                                  "bdmask", "kmask")]
    s_shape = (GDN_HEADS, GDN_DIM, GDN_DIM)
    return pl.pallas_call(
        functools.partial(_gdn_kernel, q=q, nseq=nseq, hpb=hpb, cps=cps),
        grid=(b // nseq, nc),
        in_specs=[pl.BlockSpec((rows, GDN_CONV_DIM), row),
                  pl.BlockSpec((rows, GDN_WIDTH), row),
                  pl.BlockSpec((rows, LANES), row),
                  pl.BlockSpec((nseq,) + s_shape, seq)]
                 + [_const_spec(p.shape) for p in params],
        out_specs=[pl.BlockSpec((rows, GDN_WIDTH), row),
                   pl.BlockSpec((nseq,) + s_shape, seq)],
        out_shape=[jax.ShapeDtypeStruct((t, GDN_WIDTH), F32),
                   jax.ShapeDtypeStruct((b,) + s_shape, F32)],
        scratch_shapes=[pltpu.VMEM((nseq,) + s_shape, F32)],
        compiler_params=_params2(),
        name="gdn_scan",
    )(qkv, gate, small, s0, *params)


def _outffn_kernel(x_ref, y_ref, o_ref, g1_ref, sh2_ref, sc2_ref, g2_ref, nwm_ref, nwf_ref, nwp_ref,
                   wout_ref, wgu_ref, wdown_ref, out_ref, *, ff_chunk):
    x = x_ref[...]
    shape3 = x.shape
    tm = shape3[0] * shape3[1]
    m = (_dot(y_ref[...].astype(BF16), wout_ref[:SSD_WIDTH, :])
         + _dot(o_ref[...].astype(BF16), wout_ref[SSD_WIDTH:, :]))
    x1 = x + g1_ref[...] * _rms(m, nwm_ref[...]).reshape(shape3)
    h2 = _rms(x1, nwf_ref[...]) * (1.0 + sc2_ref[...]) + sh2_ref[...]
    h2 = h2.reshape(tm, shape3[2]).astype(BF16)
    f = jnp.zeros((tm, shape3[2]), F32)
    for j in range(D_FF // ff_chunk):
        gj = _dot(h2, wgu_ref[:, j * ff_chunk:(j + 1) * ff_chunk])
        uj = _dot(h2, wgu_ref[:, D_FF + j * ff_chunk:D_FF + (j + 1) * ff_chunk])
        f = f + _dot((_silu(gj) * uj).astype(BF16), wdown_ref[j * ff_chunk:(j + 1) * ff_chunk, :])
    out_ref[...] = x1 + g2_ref[...] * _rms(f, nwp_ref[...]).reshape(shape3)


def _outffn_call(x, y, o, mod3, nw_mix_post, nw_ffn_pre, nw_ffn_post, w_out, w_gu, w_down, bb, lb):
    b, l, d = x.shape
    nl = l // lb
    tm = bb * lb
    row = lambda i, j: (i * nl + j, 0)
    mod_spec = lambda k: pl.BlockSpec((bb, 1, d), lambda i, j: (i, 0, k))
    return pl.pallas_call(
        functools.partial(_outffn_kernel, ff_chunk=D_FF // 2),
        grid=(b // bb, nl),
        in_specs=[pl.BlockSpec((bb, lb, d), lambda i, j: (i, j, 0)),
                  pl.BlockSpec((tm, SSD_WIDTH), row),
                  pl.BlockSpec((tm, GDN_WIDTH), row),
                  mod_spec(2), mod_spec(3), mod_spec(4), mod_spec(5),
                  _const_spec((1, d)), _const_spec((1, d)), _const_spec((1, d)),
                  _const_spec(w_out.shape), _const_spec(w_gu.shape), _const_spec(w_down.shape)],
        out_specs=pl.BlockSpec((bb, lb, d), lambda i, j: (i, j, 0)),
        out_shape=jax.ShapeDtypeStruct((b, l, d), F32),
        compiler_params=_params2(),
        name="outproj_ffn",
    )(x, y, o, mod3, mod3, mod3, mod3, nw_mix_post, nw_ffn_pre, nw_ffn_post, w_out, w_gu, w_down)


def _pad_lanes(v, lane0):
    out = jnp.zeros((1, LANES), F32)
    return out.at[0, lane0:lane0 + v.shape[0]].set(v.astype(F32))


def _layer(x, mod, ssd_h0, ssd_conv0, gdn_s0, gdn_conv0, p, q, bb, lb):
    b, l, d = x.shape
    mod3 = mod.reshape(b, 1, 6 * d)
    z, xbc, qkv, gate, small, ssd_conv, gdn_conv = _inproj_call(
        x, mod3, p["norm_mix_pre"], p["w_big"], p["w_small"], ssd_conv0, gdn_conv0,
        p["ssd_conv_w"], p["ssd_conv_b"], p["gdn_conv_w"], bb, lb)
    y, ssd_h = _ssd_call(xbc, z, small, ssd_h0.reshape(b, SSD_WIDTH, SSD_STATE),
                         p["ssd_dtb"], p["ssd_a"], p["ssd_d"], p["ssd_norm_w"], b, l, q)
    o, gdn_s = _gdn_call(qkv, gate, small, gdn_s0, p["gdn_dtb"], p["gdn_a"], p["gdn_norm_w"], b, l, q)
    out = _outffn_call(x, y, o, mod3, p["norm_mix_post"], p["norm_ffn_pre"], p["norm_ffn_post"],
                       p["w_out"], p["w_gu"], p["w_down"], bb, lb)
    return out, ssd_h.reshape(ssd_h0.shape), ssd_conv, gdn_s, gdn_conv


def _layer_params(l, w_ada, b_ada, norm_mix_pre, norm_mix_post, norm_ffn_pre, norm_ffn_post, w_in, ssd_conv_w,
                  ssd_conv_b, ssd_dt_bias, ssd_A_log, ssd_D, ssd_norm_w, gdn_conv_w, gdn_dt_bias, gdn_A_log,
                  gdn_norm_w, w_out, w_gate_up, w_down):
    w = w_in[l]
    o_z, o_xbc = 0, SSD_WIDTH
    o_dt = o_xbc + SSD_CONV_DIM
    o_qkv = o_dt + SSD_HEADS
    o_gate = o_qkv + GDN_CONV_DIM
    o_beta = o_gate + GDN_WIDTH
    o_alpha = o_beta + GDN_HEADS
    w_big = jnp.concatenate([w[:, o_z:o_dt], w[:, o_qkv:o_beta]], axis=1).astype(BF16)
    w_small = jnp.concatenate([w[:, o_dt:o_qkv], w[:, o_beta:o_alpha + GDN_HEADS],
                               jnp.zeros((w.shape[0], LANES - SSD_HEADS - 2 * GDN_HEADS), w.dtype)],
                              axis=1).astype(BF16)
    row = lambda v: v.reshape(1, -1).astype(F32)
    return dict(
        w_ada=w_ada[l], b_ada=b_ada[l],
        norm_mix_pre=row(norm_mix_pre[l]), norm_mix_post=row(norm_mix_post[l]),
        norm_ffn_pre=row(norm_ffn_pre[l]), norm_ffn_post=row(norm_ffn_post[l]),
        w_big=w_big, w_small=w_small,
        ssd_conv_w=ssd_conv_w[l], ssd_conv_b=row(ssd_conv_b[l]),
        ssd_dtb=_pad_lanes(ssd_dt_bias[l], DT_LANE0),
        ssd_a=_pad_lanes(-jnp.exp(ssd_A_log[l].astype(F32)), DT_LANE0),
        ssd_d=row(jnp.repeat(ssd_D[l], SSD_HEADDIM)),
        ssd_norm_w=row(ssd_norm_w[l]),
        gdn_conv_w=gdn_conv_w[l],
        gdn_dtb=_pad_lanes(gdn_dt_bias[l], ALPHA_LANE0),
        gdn_a=_pad_lanes(-jnp.exp(gdn_A_log[l].astype(F32)), ALPHA_LANE0),
        gdn_norm_w=row(gdn_norm_w[l]),
        w_out=w_out[l].astype(BF16), w_gu=w_gate_up[l].astype(BF16), w_down=w_down[l].astype(BF16),
    )


def kernel(x_prompt, x_sample, c_prompt, c_sample, state_ssd, state_ssd_conv, state_gdn, state_gdn_conv, w_ada, b_ada, norm_mix_pre, norm_mix_post, norm_ffn_pre, norm_ffn_post, w_in, ssd_conv_w, ssd_conv_b, ssd_dt_bias, ssd_A_log, ssd_D, ssd_norm_w, gdn_conv_w, gdn_dt_bias, gdn_A_log, gdn_norm_w, w_out, w_gate_up, w_down):
    depth = w_in.shape[0]
    bp, lp, _ = x_prompt.shape
    bs, ls, _ = x_sample.shape
    yp, ys = x_prompt, x_sample
    outs = [[] for _ in range(8)]
    for l in range(depth):
        p = _layer_params(l, w_ada, b_ada, norm_mix_pre, norm_mix_post, norm_ffn_pre, norm_ffn_post, w_in,
                          ssd_conv_w, ssd_conv_b, ssd_dt_bias, ssd_A_log, ssd_D, ssd_norm_w, gdn_conv_w,
                          gdn_dt_bias, gdn_A_log, gdn_norm_w, w_out, w_gate_up, w_down)
        mod = _mod_call(jnp.concatenate([c_prompt, c_sample], axis=0), p["w_ada"], p["b_ada"])
        zeros = lambda a: jnp.zeros((bp,) + a.shape[2:], a.dtype)
        yp, a0, a1, a2, a3 = _layer(yp, mod[:bp], zeros(state_ssd), zeros(state_ssd_conv), zeros(state_gdn),
                                    zeros(state_gdn_conv), p, q=min(CHUNK, lp), bb=1, lb=min(ROW_TILE, lp))
        ys, b0, b1, b2, b3 = _layer(ys, mod[bp:], state_ssd[l], state_ssd_conv[l], state_gdn[l],
                                    state_gdn_conv[l], p, q=min(CHUNK, ls), bb=min(ROW_TILE // ls, bs), lb=ls)
        for lst, v in zip(outs, (a0, a1, a2, a3, b0, b1, b2, b3)):
            lst.append(v)
    return (yp, ys) + tuple(v[0][None] if depth == 1 else jnp.stack(v) for v in outs)
```

```python
import functools

import numpy as np
import jax
import jax.numpy as jnp
from jax import lax
from jax.experimental import pallas as pl
from jax.experimental.pallas import tpu as pltpu

F32 = jnp.float32
BF16 = jnp.bfloat16

D_MODEL = 1024
SSD_HEADS = 16
SSD_HEADDIM = 64
SSD_GROUPS = 2
SSD_STATE = 128
SSD_WIDTH = SSD_HEADS * SSD_HEADDIM
SSD_CONV_DIM = SSD_WIDTH + 2 * SSD_GROUPS * SSD_STATE
GDN_HEADS = 8
GDN_DIM = 128
GDN_WIDTH = GDN_HEADS * GDN_DIM
GDN_CONV_DIM = 3 * GDN_WIDTH
CONV_WIDTH = 4
CHUNK = 64
D_FF = 2816
NORM_EPS = 1e-6
L2_EPS = 1e-6
LANES = 128
SUBLANES = 8
MXU_DIM = 256
DT_LANE0, BETA_LANE0, ALPHA_LANE0 = 0, SSD_HEADS, SSD_HEADS + GDN_HEADS
VMEM_LIMIT = 56 * 1024 * 1024
ROW_TILE = 256
SCAN_UNITS_PER_STEP = 4
CONV_COLS = 512


def _dot(a, b):
    return jnp.dot(a, b, preferred_element_type=F32)


def _dot_nt(a, b):
    return lax.dot_general(a, b, (((1,), (1,)), ((), ())), preferred_element_type=F32)


def _dot_tn(a, b):
    return lax.dot_general(a, b, (((0,), (0,)), ((), ())), preferred_element_type=F32)


def _bf16_pieces(x):
    hi = x.astype(BF16)
    r1 = x - hi.astype(F32)
    mid = r1.astype(BF16)
    lo = (r1 - mid.astype(F32)).astype(BF16)
    return hi, mid, lo


def _split3_lanes(x):
    hi, mid, lo = _bf16_pieces(x)
    return jnp.concatenate([hi.astype(F32), mid.astype(F32), lo.astype(F32)], axis=1).astype(BF16)


def _dot_exact_rhs(a_bf16, x):
    hi, mid, lo = _bf16_pieces(x)
    return _dot(a_bf16, hi) + _dot(a_bf16, mid) + _dot(a_bf16, lo)


def _silu(x):
    return x * jax.nn.sigmoid(x)


def _softplus(x):
    return jnp.maximum(x, 0.0) + jnp.log1p(jnp.exp(-jnp.abs(x)))


def _rms(x, w):
    return x * lax.rsqrt(jnp.mean(x * x, axis=-1, keepdims=True) + NORM_EPS) * w


def _const_spec(shape):
    nd = len(shape)
    return pl.BlockSpec(shape, lambda *_: (0,) * nd, pipeline_mode=pl.Buffered(1))


def _params2(sem=("arbitrary", "arbitrary")):
    return pltpu.CompilerParams(dimension_semantics=sem, vmem_limit_bytes=VMEM_LIMIT)


def _mod_kernel(c_ref, w_ref, b_ref, o_ref):
    a = _silu(c_ref[...]).astype(BF16)
    o_ref[...] = _dot(a, w_ref[...].astype(BF16)) + b_ref[...]


def _mod_call(c_all, w_ada, b_ada):
    m, d = c_all.shape
    n = w_ada.shape[1]
    tn = 512
    return pl.pallas_call(
        _mod_kernel,
        grid=(n // tn,),
        in_specs=[pl.BlockSpec((m, d), lambda j: (0, 0)),
                  pl.BlockSpec((d, tn), lambda j: (0, j)),
                  pl.BlockSpec((1, tn), lambda j: (0, j))],
        out_specs=pl.BlockSpec((m, tn), lambda j: (0, j)),
        out_shape=jax.ShapeDtypeStruct((m, n), F32),
        compiler_params=_params2(("arbitrary",)),
        name="adaln_mod",
    )(c_all, w_ada, b_ada.reshape(1, n))


def _proj_conv_silu(h, w_ref, w_col0, width, bb, l, tail_ref, convout_ref, cw_ref, cb_ref, out_ref, n_norm, norm_scale):
    tm = bb * l
    for k in range(width // CONV_COLS):
        cols = slice(k * CONV_COLS, (k + 1) * CONV_COLS)
        raw = _dot(h, w_ref[:, w_col0 + k * CONV_COLS:w_col0 + (k + 1) * CONV_COLS]).reshape(bb, l, CONV_COLS)
        ext = jnp.concatenate([tail_ref[:, :, cols], raw], axis=1)
        acc = ext[:, SUBLANES:SUBLANES + l] * cw_ref[CONV_WIDTH - 1:CONV_WIDTH, cols]
        for j in range(1, CONV_WIDTH):
            acc = acc + ext[:, SUBLANES - j:SUBLANES - j + l] * cw_ref[CONV_WIDTH - 1 - j:CONV_WIDTH - j, cols]
        tail_ref[:, :, cols] = ext[:, l:l + SUBLANES]
        convout_ref[:, :, cols] = ext[:, l + SUBLANES - (CONV_WIDTH - 1):l + SUBLANES]
        if cb_ref is not None:
            acc = acc + cb_ref[:, cols]
        act = _silu(acc).reshape(tm, CONV_COLS)
        for hh in range(CONV_COLS // LANES):
            head = k * (CONV_COLS // LANES) + hh
            xh = act[:, hh * LANES:(hh + 1) * LANES]
            if head < n_norm:
                xh = xh * (lax.rsqrt(jnp.sum(xh * xh, axis=-1, keepdims=True) + L2_EPS) * norm_scale[head])
            out_ref[:, head * LANES:(head + 1) * LANES] = xh


def _inproj_kernel(x_ref, sh_ref, sc_ref, nw_ref, wbig_ref, wsm_ref, sconv0_ref, gconv0_ref,
                   scw_ref, scb_ref, gcw_ref,
                   z_ref, xbc_ref, qkv_ref, gate_ref, sm_ref, sconv_ref, gconv_ref, stail_ref, gtail_ref):
    @pl.when(pl.program_id(1) == 0)
    def _():
        for tail_ref, conv0_ref in ((stail_ref, sconv0_ref), (gtail_ref, gconv0_ref)):
            tail_ref[...] = jnp.zeros(tail_ref.shape, F32)
            tail_ref[:, SUBLANES - (CONV_WIDTH - 1):, :] = conv0_ref[...]

    x = x_ref[...]
    bb, l, d = x.shape
    tm = bb * l
    h = _rms(x, nw_ref[...]) * (1.0 + sc_ref[...]) + sh_ref[...]
    h = h.reshape(tm, d).astype(BF16)
    c_xbc = SSD_WIDTH
    c_qkv = c_xbc + SSD_CONV_DIM
    c_gate = c_qkv + GDN_CONV_DIM
    z_ref[...] = _dot(h, wbig_ref[:, :c_xbc])
    gate_ref[...] = _dot(h, wbig_ref[:, c_gate:])
    sm_ref[...] = _dot(h, wsm_ref[...])
    _proj_conv_silu(h, wbig_ref, c_xbc, SSD_CONV_DIM, bb, l, stail_ref, sconv_ref, scw_ref, scb_ref, xbc_ref, 0, ())
    qk_scale = (GDN_DIM ** -0.5,) * GDN_HEADS + (1.0,) * GDN_HEADS
    _proj_conv_silu(h, wbig_ref, c_qkv, GDN_CONV_DIM, bb, l, gtail_ref, gconv_ref, gcw_ref, None, qkv_ref,
                    2 * GDN_HEADS, qk_scale)


def _inproj_call(x, mod3, norm_w, w_big, w_small, sconv0, gconv0, scw, scb, gcw, bb, lb):
    b, l, d = x.shape
    nl = l // lb
    tm = bb * lb
    t = b * l
    row = lambda i, j: (i * nl + j, 0)
    seq = lambda i, j: (i, 0, 0)
    widths = (SSD_WIDTH, SSD_CONV_DIM, GDN_CONV_DIM, GDN_WIDTH, LANES)
    hist = CONV_WIDTH - 1
    return pl.pallas_call(
        _inproj_kernel,
        grid=(b // bb, nl),
        in_specs=[pl.BlockSpec((bb, lb, d), lambda i, j: (i, j, 0)),
                  pl.BlockSpec((bb, 1, d), lambda i, j: (i, 0, 0)),
                  pl.BlockSpec((bb, 1, d), lambda i, j: (i, 0, 1)),
                  _const_spec((1, d)),
                  _const_spec(w_big.shape),
                  _const_spec(w_small.shape),
                  pl.BlockSpec((bb, hist, SSD_CONV_DIM), seq),
                  pl.BlockSpec((bb, hist, GDN_CONV_DIM), seq),
                  _const_spec(scw.shape), _const_spec(scb.shape), _const_spec(gcw.shape)],
        out_specs=[pl.BlockSpec((tm, w), row) for w in widths]
                  + [pl.BlockSpec((bb, hist, SSD_CONV_DIM), seq), pl.BlockSpec((bb, hist, GDN_CONV_DIM), seq)],
        out_shape=[jax.ShapeDtypeStruct((t, w), F32) for w in widths]
                  + [jax.ShapeDtypeStruct((b, hist, SSD_CONV_DIM), F32),
                     jax.ShapeDtypeStruct((b, hist, GDN_CONV_DIM), F32)],
        scratch_shapes=[pltpu.VMEM((bb, SUBLANES, SSD_CONV_DIM), F32),
                        pltpu.VMEM((bb, SUBLANES, GDN_CONV_DIM), F32)],
        compiler_params=_params2(),
        name="inproj",
    )(x, mod3, mod3, norm_w, w_big, w_small, sconv0, gconv0, scw, scb, gcw)


def _ssd_constants(q, nu):
    hq = SSD_HEADS * q
    hp = LANES // q
    w = hp * SSD_HEADDIM
    e_p = np.zeros((LANES, SSD_WIDTH), np.float32)
    e_q = np.zeros((LANES, hq), np.float32)
    for r in range(SSD_HEADS):
        e_p[DT_LANE0 + r, r * SSD_HEADDIM:(r + 1) * SSD_HEADDIM] = 1.0
        e_q[DT_LANE0 + r, r * q:(r + 1) * q] = 1.0
    tri = np.tril(np.ones((q, q), np.float32))
    gmask = np.zeros((hq, 2 * SSD_STATE), np.float32)
    half = hq // SSD_GROUPS
    gmask[:half, :SSD_STATE] = 1.0
    gmask[half:, SSD_STATE:] = 1.0
    xmask = np.kron(np.eye(hp, dtype=np.float32), np.ones((q, SSD_HEADDIM), np.float32))
    unit_eye = np.eye(nu, dtype=np.float32)
    return dict(
        l3=jnp.asarray(np.kron(unit_eye, tri), BF16),
        ones3=jnp.asarray(np.kron(unit_eye, np.ones((q, q), np.float32)), BF16),
        e3p=jnp.asarray(np.tile(e_p, (3, 1)), BF16),
        e3q=jnp.asarray(np.tile(e_q, (3, 1)), BF16),
        dmask=jnp.asarray(np.tile(np.eye(q, dtype=np.float32), (nu, SSD_HEADS))),
        causal=jnp.asarray(np.tile(tri, (nu, SSD_HEADS))),
        gmask=jnp.asarray(gmask),
        xmask=jnp.asarray(xmask),
    ), hp, w


def _ssd_kernel(xbc_ref, z_ref, sm_ref, h0_ref, dtb_ref, a_ref, d_ref, nw_ref,
                l3_ref, ones3_ref, e3p_ref, e3q_ref, dmask_ref, causal_ref, gmask_ref, xmask_ref,
                y_ref, hout_ref, ht_ref, *, q, hp, w, nseq, cps):
    c = pl.program_id(1)
    nc = pl.num_programs(1)
    nu = nseq * cps
    rows = nu * q
    gw = SSD_WIDTH // SSD_GROUPS

    @pl.when(c == 0)
    def _():
        for sq in range(nseq):
            ht_ref[sq] = h0_ref[sq].T

    def unit_rows(u):
        return slice(u * q, (u + 1) * q)

    xs = xbc_ref[:, :SSD_WIDTH]
    bcat = xbc_ref[:, SSD_WIDTH:SSD_WIDTH + 2 * SSD_STATE]
    ccat = xbc_ref[:, SSD_WIDTH + 2 * SSD_STATE:].astype(BF16)

    dt = _softplus(sm_ref[...] + dtb_ref[...])
    acs = _dot_exact_rhs(l3_ref[...], dt * a_ref[...])
    dec_ends = [jnp.exp(acs[(u + 1) * q - 1:(u + 1) * q, :] - acs[unit_rows(u)]) for u in range(nu)]
    dec_end = dec_ends[0] if nu == 1 else jnp.concatenate(dec_ends, axis=0)
    per_head = jnp.concatenate([dt, dt * dec_end, jnp.exp(acs)], axis=0)
    exp3 = _dot(_split3_lanes(per_head), e3p_ref[...])
    dt_e, dtd_e, eacs_e = exp3[:rows], exp3[rows:2 * rows], exp3[2 * rows:]

    col_all = _dot(_split3_lanes(acs), e3q_ref[...])
    row_all = _dot_exact_rhs(ones3_ref[...], col_all * dmask_ref[...])
    decay = jnp.exp(jnp.minimum(col_all - row_all, 0.0)) * causal_ref[...]
    xdt = xs * dt_e
    xd = (xs * dtd_e).astype(BF16)
    dxs = d_ref[...] * xs
    gmask = gmask_ref[...]
    xmask = xmask_ref[...]

    def cb_all(u):
        if 2 * q == LANES:
            per_group = []
            for g in range(SSD_GROUPS):
                gl = slice(g * SSD_STATE, (g + 1) * SSD_STATE)
                cb = _dot_nt(ccat[unit_rows(u), gl], bcat[unit_rows(u), gl].astype(BF16))
                per_group += [jnp.concatenate([cb, cb], axis=1)] * (SSD_HEADS // SSD_GROUPS // 2)
            return jnp.concatenate(per_group, axis=1)
        brep = (jnp.concatenate([bcat[unit_rows(u)]] * SSD_HEADS, axis=0) * gmask).astype(BF16)
        return _dot_nt(ccat[unit_rows(u)], brep)

    cbs = [cb_all(u) for u in range(nu)]
    m_alls = [(cbs[u] * decay[unit_rows(u)]).astype(BF16) for u in range(nu)]
    y_diags, upds = [], []
    for u in range(nu):
        blocks = []
        for j in range(SSD_WIDTH // w):
            xj = (jnp.concatenate([xdt[unit_rows(u), j * w:(j + 1) * w]] * hp, axis=0) * xmask).astype(BF16)
            blocks.append(_dot(m_alls[u][:, j * LANES:(j + 1) * LANES], xj))
        y_diags.append(blocks[0] if len(blocks) == 1 else jnp.concatenate(blocks, axis=1))
        upds.append(jnp.concatenate(
            [_dot_tn(bcat[unit_rows(u), g * SSD_STATE:(g + 1) * SSD_STATE].astype(BF16),
                     xd[unit_rows(u), g * gw:(g + 1) * gw]) for g in range(SSD_GROUPS)], axis=1))

    nw = nw_ref[...]
    for sq in range(nseq):
        ht = ht_ref[sq]
        for ci in range(cps):
            u = sq * cps + ci
            ht_bf = ht.astype(BF16)
            y_off = jnp.concatenate([_dot(ccat[unit_rows(u), g * SSD_STATE:(g + 1) * SSD_STATE],
                                          ht_bf[:, g * gw:(g + 1) * gw]) for g in range(SSD_GROUPS)], axis=1)
            e_u = eacs_e[unit_rows(u)]
            y = y_diags[u] + y_off * e_u + dxs[unit_rows(u)]
            ht = ht * e_u[q - 1:q, :] + upds[u]
            yz = y * _silu(z_ref[unit_rows(u)])
            outs = [_rms(yz[:, g * gw:(g + 1) * gw], nw[:, g * gw:(g + 1) * gw]) for g in range(SSD_GROUPS)]
            y_ref[unit_rows(u)] = jnp.concatenate(outs, axis=1).astype(y_ref.dtype)
        ht_ref[sq] = ht

    @pl.when(c == nc - 1)
    def _():
        for sq in range(nseq):
            hout_ref[sq] = ht_ref[sq].T


def _scan_units(b, l, q):
    nchunks = l // q
    cps = SCAN_UNITS_PER_STEP if nchunks % SCAN_UNITS_PER_STEP == 0 else 1
    nseq = SCAN_UNITS_PER_STEP if nchunks == 1 and b % SCAN_UNITS_PER_STEP == 0 else 1
    return nseq, cps


def _ssd_call(xbc, z, small, h0, dtb, a_pad, d_e, nw, b, l, q):
    nseq, cps = _scan_units(b, l, q)
    consts, hp, w = _ssd_constants(q, nseq * cps)
    nc = l // (q * cps)
    t = b * l
    rows = nseq * cps * q
    row = lambda i, c: (i * nc + c, 0)
    seq = lambda i, c: (i, 0, 0)
    params = [dtb, a_pad, d_e, nw] + [consts[k] for k in
                                      ("l3", "ones3", "e3p", "e3q", "dmask", "causal", "gmask", "xmask")]
    hp_shape = (SSD_WIDTH, SSD_STATE)
    return pl.pallas_call(
        functools.partial(_ssd_kernel, q=q, hp=hp, w=w, nseq=nseq, cps=cps),
        grid=(b // nseq, nc),
        in_specs=[pl.BlockSpec((rows, SSD_CONV_DIM), row),
                  pl.BlockSpec((rows, SSD_WIDTH), row),
                  pl.BlockSpec((rows, LANES), row),
                  pl.BlockSpec((nseq,) + hp_shape, seq)]
                 + [_const_spec(p.shape) for p in params],
        out_specs=[pl.BlockSpec((rows, SSD_WIDTH), row),
                   pl.BlockSpec((nseq,) + hp_shape, seq)],
        out_shape=[jax.ShapeDtypeStruct((t, SSD_WIDTH), F32),
                   jax.ShapeDtypeStruct((b,) + hp_shape, F32)],
        scratch_shapes=[pltpu.VMEM((nseq, SSD_STATE, SSD_WIDTH), F32)],
        compiler_params=_params2(),
        name="ssd_scan",
    )(xbc, z, small, h0, *params)


def _gdn_constants(q, nseq, hpb, cps):
    assert nseq * hpb * q == MXU_DIM
    nhb = GDN_HEADS // hpb
    rows = nseq * q
    e_beta = np.zeros((LANES, GDN_WIDTH), np.float32)
    e_alpha = np.zeros((LANES, GDN_WIDTH), np.float32)
    e_c = np.zeros((LANES, nhb * MXU_DIM), np.float32)
    for h in range(GDN_HEADS):
        e_beta[BETA_LANE0 + h, h * GDN_DIM:(h + 1) * GDN_DIM] = 1.0
        e_alpha[ALPHA_LANE0 + h, h * GDN_DIM:(h + 1) * GDN_DIM] = 1.0
    jh, seq_c, hh_c, s_c = np.unravel_index(np.arange(nhb * MXU_DIM), (nhb, nseq, hpb, q))
    e_c[ALPHA_LANE0 + jh * hpb + hh_c, np.arange(nhb * MXU_DIM)] = 1.0
    seq_r, l_r = np.unravel_index(np.arange(rows), (nseq, q))
    same_seq = seq_r[:, None] == seq_c[None, :]
    dmask = same_seq & (l_r[:, None] == s_c[None, :])
    incl = same_seq & (l_r[:, None] >= s_c[None, :])
    strict = same_seq & (l_r[:, None] > s_c[None, :])
    seq_s, hh_s, _ = np.unravel_index(np.arange(MXU_DIM), (nseq, hpb, q))
    bdmask = (seq_s[:, None] == seq_c[None, :MXU_DIM]) & (hh_s[:, None] == hh_c[None, :MXU_DIM])
    kmask = hh_s[:, None] == (np.arange(hpb * GDN_DIM) // GDN_DIM)[None, :]
    f = lambda m: jnp.asarray(m.astype(np.float32))
    per_chunk = lambda m: jnp.asarray(np.tile(m.astype(np.float32), (cps, 1)))
    chunk_eye = np.eye(cps)
    mask_dtype = BF16 if q % (2 * SUBLANES) == 0 else F32
    return dict(
        l3=jnp.asarray(np.kron(np.eye(cps * nseq), np.tril(np.ones((q, q)))), BF16),
        ones3=jnp.asarray(np.kron(chunk_eye, np.ones((rows, rows))), BF16),
        e3b=jnp.asarray(np.tile(e_beta, (3, 1)), BF16),
        e3a=jnp.asarray(np.tile(e_alpha, (3, 1)), BF16),
        e3c=jnp.asarray(np.tile(e_c, (3, 1)), BF16),
        dmask=per_chunk(dmask), incl=per_chunk(incl), strict=f(strict), eye=f(dmask[:, :MXU_DIM]),
        bdmask=f(bdmask).astype(mask_dtype), kmask=f(kmask).astype(mask_dtype),
    )


def _gdn_kernel(qkv_ref, gate_ref, sm_ref, s0_ref, dtb_ref, a_ref, nw_ref,
                l3_ref, ones3_ref, e3b_ref, e3a_ref, e3c_ref, dmask_ref, incl_ref, strict_ref, eye_ref,
                bdmask_ref, kmask_ref,
                o_ref, sout_ref, s_ref, *, q, nseq, hpb, cps):
    c = pl.program_id(1)
    nc = pl.num_programs(1)
    nhb = GDN_HEADS // hpb
    rows = nseq * q
    bw = hpb * GDN_DIM
    n_sq = q.bit_length() - 1
    assert q == 1 << n_sq and n_sq >= 2

    @pl.when(c == 0)
    def _():
        s_ref[...] = s0_ref[...]

    bdmask = bdmask_ref[...]
    kmask = kmask_ref[...]
    mask_dtype = bdmask.dtype

    def head_lanes(h):
        return slice(h * GDN_DIM, (h + 1) * GDN_DIM)

    def seq_rows(sq):
        return slice(sq * q, (sq + 1) * q)

    def chunk_rows(ci):
        return slice(ci * rows, (ci + 1) * rows)

    def stack_rows(x):
        return jnp.concatenate([x[seq_rows(sq)] for sq in range(nseq) for _ in range(hpb)], axis=0)

    def masked_bf16(x, mask):
        return (x.astype(mask_dtype) * mask).astype(BF16)

    def block_diag(m_bf16):
        return masked_bf16(stack_rows(m_bf16.astype(mask_dtype)), bdmask)

    def block_diag_pieces(m):
        hi = m.astype(BF16)
        lo = (m - hi.astype(F32)).astype(BF16)
        return block_diag(hi), block_diag(lo)

    def lhs_pieces(m):
        hi = m.astype(BF16)
        return hi, (m - hi.astype(F32)).astype(BF16)

    sm = sm_ref[...]
    beta = jax.nn.sigmoid(sm)
    g = a_ref[...] * _softplus(sm + dtb_ref[...])
    gc = _dot_exact_rhs(l3_ref[...], g)
    gc3 = _split3_lanes(gc)
    beta_e = _dot(_split3_lanes(beta), e3b_ref[...])
    gc_e = _dot(gc3, e3a_ref[...])
    col_c = _dot(gc3, e3c_ref[...])
    row_c = _dot_exact_rhs(ones3_ref[...], col_c * dmask_ref[...])
    decay_c = jnp.exp(jnp.minimum(col_c - row_c, 0.0)) * incl_ref[...]
    egc_e = jnp.exp(gc_e)
    qn = qkv_ref[:, :GDN_WIDTH]
    kn = qkv_ref[:, GDN_WIDTH:2 * GDN_WIDTH]
    kb = kn * beta_e
    vb = qkv_ref[:, 2 * GDN_WIDTH:] * beta_e
    kbg = kb * egc_e
    qin = qn * egc_e

    def chunk_local(ci):
        r0 = ci * rows
        lasts = [gc_e[r0 + (sq + 1) * q - 1:r0 + (sq + 1) * q, :] for sq in range(nseq)]
        eouts = [jnp.exp(lasts[sq] - gc_e[r0 + sq * q:r0 + (sq + 1) * q]) for sq in range(nseq)]
        cr = chunk_rows(ci)
        return dict(
            qn=qn[cr], kn=kn[cr], kb=kb[cr], vb=vb[cr], kbg=kbg[cr], qin=qin[cr], decay_c=decay_c[cr],
            kout=(kn[cr] * (eouts[0] if nseq == 1 else jnp.concatenate(eouts, axis=0))).astype(BF16),
            ge=[jnp.exp(last) for last in lasts])

    def chain_init(loc, jb):
        nl = slice(jb * bw, (jb + 1) * bw)
        cl = slice(jb * MXU_DIM, (jb + 1) * MXU_DIM)
        k_bd = masked_bf16(stack_rows(loc["kn"][:, nl].astype(mask_dtype)), kmask)
        prod = _dot_nt(jnp.concatenate([loc["kb"][:, nl], loc["qn"][:, nl]], axis=0).astype(BF16), k_bd)
        dec = loc["decay_c"][:, cl]
        x = -(prod[:rows] * dec * strict_ref[:, cl])
        return dict(inv=eye_ref[...] + x, ypow=x, attn=(prod[rows:] * dec).astype(BF16))

    def double_step(st, k):
        first, last = k == 0, k == n_sq - 1
        inv, ypow = st["inv"], st["ypow"]
        b_hi, b_lo = block_diag_pieces(ypow)
        lhs = ypow if first else (inv if last else jnp.concatenate([inv, ypow], axis=0))
        l_hi, l_lo = lhs_pieces(lhs)
        res = _dot(jnp.concatenate([l_hi, l_hi, l_lo], axis=1), jnp.concatenate([b_hi, b_lo, b_hi], axis=0))
        if first:
            return dict(st, ypow=res)
        if last:
            return dict(st, inv=inv + res)
        return dict(st, inv=inv + res[:rows], ypow=res[rows:])

    def solve(loc, jb, inv):
        i_hi, i_lo = block_diag_pieces(inv)
        segs = [(sq, jb * hpb + hh) for sq in range(nseq) for hh in range(hpb)]
        rhs = jnp.concatenate([jnp.concatenate([loc["vb"][seq_rows(sq), head_lanes(h)],
                                                loc["kbg"][seq_rows(sq), head_lanes(h)]], axis=1)
                               for sq, h in segs], axis=0).astype(BF16)
        return _dot(jnp.concatenate([i_hi, i_lo], axis=1), jnp.concatenate([rhs, rhs], axis=0))

    def state_step(loc, jb, uw, attn):
        segs = [(sq, jb * hpb + hh) for sq in range(nseq) for hh in range(hpb)]
        ws, qs, states = [], [], []
        for i, (sq, h) in enumerate(segs):
            s = s_ref[sq, h]
            states.append(s)
            wq = jnp.concatenate([uw[i * q:(i + 1) * q, GDN_DIM:], loc["qin"][seq_rows(sq), head_lanes(h)]], axis=0)
            r = _dot(wq.astype(BF16), s.astype(BF16))
            ws.append(r[:q])
            qs.append(r[q:])
        v_bf = (uw[:, :GDN_DIM] - jnp.concatenate(ws, axis=0)).astype(BF16)
        v_rep = masked_bf16(jnp.concatenate([v_bf.astype(mask_dtype)] * hpb, axis=1), kmask)
        qs_rows = [jnp.concatenate(qs[sq * hpb:(sq + 1) * hpb], axis=1) for sq in range(nseq)]
        qs_nat = qs_rows[0] if nseq == 1 else jnp.concatenate(qs_rows, axis=0)
        for i, (sq, h) in enumerate(segs):
            s_ref[sq, h] = (states[i] * loc["ge"][sq][:, head_lanes(h)]
                            + _dot_tn(loc["kout"][seq_rows(sq), head_lanes(h)], v_bf[i * q:(i + 1) * q]))
        return qs_nat + _dot(attn, v_rep)

    local = [chunk_local(ci) for ci in range(cps)]
    chains = [(ci, jb) for ci in range(cps) for jb in range(nhb)]
    st = [chain_init(local[ci], jb) for ci, jb in chains]
    for k in range(n_sq):
        st = [double_step(s, k) for s in st]
    uws =[solve(local[ci], jb, s["inv"]) for (ci, jb), s in zip(chains, st)]
    nw = nw_ref[...]
    for ci in range(cps):
        o_blocks = [state_step(local[ci], jb, uws[ci * nhb + jb], st[ci * nhb + jb]["attn"]) for jb in range(nhb)]
        o = o_blocks[0] if nhb == 1 else jnp.concatenate(o_blocks, axis=1)
        gate = gate_ref[chunk_rows(ci)]
        outs = [_rms(o[:, head_lanes(h)], nw) * _silu(gate[:, head_lanes(h)]) for h in range(GDN_HEADS)]
        o_ref[chunk_rows(ci)] = jnp.concatenate(outs, axis=1).astype(o_ref.dtype)

    @pl.when(c == nc - 1)
    def _():
        sout_ref[...] = s_ref[...]


def _gdn_call(qkv, gate, small, s0, dtb, a_pad, nw, b, l, q):
    nseq = max(1, MXU_DIM // (GDN_HEADS * q))
    hpb = MXU_DIM // (nseq * q)
    cps = SCAN_UNITS_PER_STEP if nseq == 1 and (l // q) % SCAN_UNITS_PER_STEP == 0 else 1
    consts = _gdn_constants(q, nseq, hpb, cps)
    nc = l // (q * cps)
    assert b % nseq == 0 and (nseq == 1 or nc == 1)
    t = b * l
    rows = nseq * q * cps
    row = lambda i, c: (i * nc + c, 0)
    seq = lambda i, c: (i, 0, 0, 0)
    params = [dtb, a_pad, nw] + [consts[k] for k in
                                 ("l3", "ones3", "e3b", "e3a", "e3c", "dmask", "incl", "strict", "eye",
                                  "bdmask", "kmask")]
    s_shape = (GDN_HEADS, GDN_DIM, GDN_DIM)
    return pl.pallas_call(
        functools.partial(_gdn_kernel, q=q, nseq=nseq, hpb=hpb, cps=cps),
        grid=(b // nseq, nc),
        in_specs=[pl.BlockSpec((rows, GDN_CONV_DIM), row),
                  pl.BlockSpec((rows, GDN_WIDTH), row),
                  pl.BlockSpec((rows, LANES), row),
                  pl.BlockSpec((nseq,) + s_shape, seq)]
                 + [_const_spec(p.shape) for p in params],
        out_specs=[pl.BlockSpec((rows, GDN_WIDTH), row),
                   pl.BlockSpec((nseq,) + s_shape, seq)],
        out_shape=[jax.ShapeDtypeStruct((t, GDN_WIDTH), F32),
                   jax.ShapeDtypeStruct((b,) + s_shape, F32)],
        scratch_shapes=[pltpu.VMEM((nseq,) + s_shape, F32)],
        compiler_params=_params2(),
        name="gdn_scan",
    )(qkv, gate, small, s0, *params)


def _outffn_kernel(x_ref, y_ref, o_ref, g1_ref, sh2_ref, sc2_ref, g2_ref, nwm_ref, nwf_ref, nwp_ref,
                   wout_ref, wgu_ref, wdown_ref, out_ref, *, ff_chunk):
    x = x_ref[...]
    shape3 = x.shape
    tm = shape3[0] * shape3[1]
    m = (_dot(y_ref[...].astype(BF16), wout_ref[:SSD_WIDTH, :])
         + _dot(o_ref[...].astype(BF16), wout_ref[SSD_WIDTH:, :]))
    x1 = x + g1_ref[...] * _rms(m, nwm_ref[...]).reshape(shape3)
    h2 = _rms(x1, nwf_ref[...]) * (1.0 + sc2_ref[...]) + sh2_ref[...]
    h2 = h2.reshape(tm, shape3[2]).astype(BF16)
    f = jnp.zeros((tm, shape3[2]), F32)
    for j in range(D_FF // ff_chunk):
        gj = _dot(h2, wgu_ref[:, j * ff_chunk:(j + 1) * ff_chunk])
        uj = _dot(h2, wgu_ref[:, D_FF + j * ff_chunk:D_FF + (j + 1) * ff_chunk])
        f = f + _dot((_silu(gj) * uj).astype(BF16), wdown_ref[j * ff_chunk:(j + 1) * ff_chunk, :])
    out_ref[...] = x1 + g2_ref[...] * _rms(f, nwp_ref[...]).reshape(shape3)


def _outffn_call(x, y, o, mod3, nw_mix_post, nw_ffn_pre, nw_ffn_post, w_out, w_gu, w_down, bb, lb):
    b, l, d = x.shape
    nl = l // lb
    tm = bb * lb
    row = lambda i, j: (i * nl + j, 0)
    mod_spec = lambda k: pl.BlockSpec((bb, 1, d), lambda i, j: (i, 0, k))
    return pl.pallas_call(
        functools.partial(_outffn_kernel, ff_chunk=D_FF // 2),
        grid=(b // bb, nl),
        in_specs=[pl.BlockSpec((bb, lb, d), lambda i, j: (i, j, 0)),
                  pl.BlockSpec((tm, SSD_WIDTH), row),
                  pl.BlockSpec((tm, GDN_WIDTH), row),
                  mod_spec(2), mod_spec(3), mod_spec(4), mod_spec(5),
                  _const_spec((1, d)), _const_spec((1, d)), _const_spec((1, d)),
                  _const_spec(w_out.shape), _const_spec(w_gu.shape), _const_spec(w_down.shape)],
        out_specs=pl.BlockSpec((bb, lb, d), lambda i, j: (i, j, 0)),
        out_shape=jax.ShapeDtypeStruct((b, l, d), F32),
        compiler_params=_params2(),
        name="outproj_ffn",
    )(x, y, o, mod3, mod3, mod3, mod3, nw_mix_post, nw_ffn_pre, nw_ffn_post, w_out, w_gu, w_down)


_O_DT = SSD_WIDTH + SSD_CONV_DIM
_O_QKV = _O_DT + SSD_HEADS
_O_BETA = _O_QKV + GDN_CONV_DIM + GDN_WIDTH
_IN_DIM = _O_BETA + 2 * GDN_HEADS


def _repack_kernel(w_ref, big_ref, small_ref):
    w = w_ref[...]
    big_ref[:, :_O_DT] = w[:, :_O_DT].astype(BF16)
    big_ref[:, _O_DT:] = w[:, _O_QKV:_O_BETA].astype(BF16)
    pad = jnp.zeros((w.shape[0], LANES - SSD_HEADS - 2 * GDN_HEADS), F32)
    small_ref[...] = jnp.concatenate([w[:, _O_DT:_O_QKV], w[:, _O_BETA:], pad], axis=1).astype(BF16)


def _repack_w_in(w):
    k, n = w.shape
    assert n == _IN_DIM
    big = n - SSD_HEADS - 2 * GDN_HEADS
    tr = 128
    return pl.pallas_call(
        _repack_kernel,
        grid=(k // tr,),
        in_specs=[pl.BlockSpec((tr, n), lambda i: (i, 0))],
        out_specs=[pl.BlockSpec((tr, big), lambda i: (i, 0)), pl.BlockSpec((tr, LANES), lambda i: (i, 0))],
        out_shape=[jax.ShapeDtypeStruct((k, big), BF16), jax.ShapeDtypeStruct((k, LANES), BF16)],
        compiler_params=_params2(("arbitrary",)),
        name="repack_w_in",
    )(w)


def _pad_lanes(v, lane0):
    out = jnp.zeros((1, LANES), F32)
    return out.at[0, lane0:lane0 + v.shape[0]].set(v.astype(F32))


def _layer(x, mod, ssd_h0, ssd_conv0, gdn_s0, gdn_conv0, p, q, bb, lb):
    b, l, d = x.shape
    mod3 = mod.reshape(b, 1, 6 * d)
    z, xbc, qkv, gate, small, ssd_conv, gdn_conv = _inproj_call(
        x, mod3, p["norm_mix_pre"], p["w_big"], p["w_small"], ssd_conv0, gdn_conv0,
        p["ssd_conv_w"], p["ssd_conv_b"], p["gdn_conv_w"], bb, lb)
    y, ssd_h = _ssd_call(xbc, z, small, ssd_h0.reshape(b, SSD_WIDTH, SSD_STATE),
                         p["ssd_dtb"], p["ssd_a"], p["ssd_d"], p["ssd_norm_w"], b, l, q)
    o, gdn_s = _gdn_call(qkv, gate, small, gdn_s0, p["gdn_dtb"], p["gdn_a"], p["gdn_norm_w"], b, l, q)
    out = _outffn_call(x, y, o, mod3, p["norm_mix_post"], p["norm_ffn_pre"], p["norm_ffn_post"],
                       p["w_out"], p["w_gu"], p["w_down"], bb, lb)
    return out, ssd_h.reshape(ssd_h0.shape), ssd_conv, gdn_s, gdn_conv


def _layer_params(l, w_ada, b_ada, norm_mix_pre, norm_mix_post, norm_ffn_pre, norm_ffn_post, w_in, ssd_conv_w,
                  ssd_conv_b, ssd_dt_bias, ssd_A_log, ssd_D, ssd_norm_w, gdn_conv_w, gdn_dt_bias, gdn_A_log,
                  gdn_norm_w, w_out, w_gate_up, w_down):
    w_big, w_small = _repack_w_in(w_in[l])
    row = lambda v: v.reshape(1, -1).astype(F32)
    return dict(
        w_ada=w_ada[l], b_ada=b_ada[l],
        norm_mix_pre=row(norm_mix_pre[l]), norm_mix_post=row(norm_mix_post[l]),
        norm_ffn_pre=row(norm_ffn_pre[l]), norm_ffn_post=row(norm_ffn_post[l]),
        w_big=w_big, w_small=w_small,
        ssd_conv_w=ssd_conv_w[l], ssd_conv_b=row(ssd_conv_b[l]),
        ssd_dtb=_pad_lanes(ssd_dt_bias[l], DT_LANE0),
        ssd_a=_pad_lanes(-jnp.exp(ssd_A_log[l].astype(F32)), DT_LANE0),
        ssd_d=row(jnp.repeat(ssd_D[l], SSD_HEADDIM)),
        ssd_norm_w=row(ssd_norm_w[l]),
        gdn_conv_w=gdn_conv_w[l],
        gdn_dtb=_pad_lanes(gdn_dt_bias[l], ALPHA_LANE0),
        gdn_a=_pad_lanes(-jnp.exp(gdn_A_log[l].astype(F32)), ALPHA_LANE0),
        gdn_norm_w=row(gdn_norm_w[l]),
        w_out=w_out[l].astype(BF16), w_gu=w_gate_up[l].astype(BF16), w_down=w_down[l].astype(BF16),
    )


def kernel(x_prompt, x_sample, c_prompt, c_sample, state_ssd, state_ssd_conv, state_gdn, state_gdn_conv, w_ada, b_ada, norm_mix_pre, norm_mix_post, norm_ffn_pre, norm_ffn_post, w_in, ssd_conv_w, ssd_conv_b, ssd_dt_bias, ssd_A_log, ssd_D, ssd_norm_w, gdn_conv_w, gdn_dt_bias, gdn_A_log, gdn_norm_w, w_out, w_gate_up, w_down):
    depth = w_in.shape[0]
    bp, lp, _ = x_prompt.shape
    bs, ls, _ = x_sample.shape
    yp, ys = x_prompt, x_sample
    outs = [[] for _ in range(8)]
    for l in range(depth):
        p = _layer_params(l, w_ada, b_ada, norm_mix_pre, norm_mix_post, norm_ffn_pre, norm_ffn_post, w_in,
                          ssd_conv_w, ssd_conv_b, ssd_dt_bias, ssd_A_log, ssd_D, ssd_norm_w, gdn_conv_w,
                          gdn_dt_bias, gdn_A_log, gdn_norm_w, w_out, w_gate_up, w_down)
        mod = _mod_call(jnp.concatenate([c_prompt, c_sample], axis=0), p["w_ada"], p["b_ada"])
        zeros = lambda a: jnp.zeros((bp,) + a.shape[2:], a.dtype)
        yp, a0, a1, a2, a3 = _layer(yp, mod[:bp], zeros(state_ssd), zeros(state_ssd_conv), zeros(state_gdn),
                                    zeros(state_gdn_conv), p, q=min(CHUNK, lp), bb=1, lb=min(ROW_TILE, lp))
        ys, b0, b1, b2, b3 = _layer(ys, mod[bp:], state_ssd[l], state_ssd_conv[l], state_gdn[l],
                                    state_gdn_conv[l], p, q=min(CHUNK, ls), bb=min(ROW_TILE // ls, bs), lb=ls)
        for lst, v in zip(outs, (a0, a1, a2, a3, b0, b1, b2, b3)):
            lst.append(v)
    return (yp, ys) + tuple(v[0][None] if depth == 1 else jnp.stack(v) for v in outs)
```

```python
import functools

import numpy as np
import jax
import jax.numpy as jnp
from jax import lax
from jax.experimental import pallas as pl
from jax.experimental.pallas import tpu as pltpu

F32 = jnp.float32
BF16 = jnp.bfloat16

D_MODEL = 1024
SSD_HEADS = 16
SSD_HEADDIM = 64
SSD_GROUPS = 2
SSD_STATE = 128
SSD_WIDTH = SSD_HEADS * SSD_HEADDIM
SSD_CONV_DIM = SSD_WIDTH + 2 * SSD_GROUPS * SSD_STATE
GDN_HEADS = 8
GDN_DIM = 128
GDN_WIDTH = GDN_HEADS * GDN_DIM
GDN_CONV_DIM = 3 * GDN_WIDTH
CONV_WIDTH = 4
CHUNK = 64
D_FF = 2816
NORM_EPS = 1e-6
L2_EPS = 1e-6
LANES = 128
SUBLANES = 8
MXU_DIM = 256
DT_LANE0, BETA_LANE0, ALPHA_LANE0 = 0, SSD_HEADS, SSD_HEADS + GDN_HEADS
VMEM_LIMIT = 56 * 1024 * 1024
ROW_TILE = 256
SCAN_UNITS_PER_STEP = 4
CONV_COLS = 512


def _dot(a, b):
    return jnp.dot(a, b, preferred_element_type=F32)


def _dot_nt(a, b):
    return lax.dot_general(a, b, (((1,), (1,)), ((), ())), preferred_element_type=F32)


def _dot_tn(a, b):
    return lax.dot_general(a, b, (((0,), (0,)), ((), ())), preferred_element_type=F32)


def _bf16_pieces(x):
    hi = x.astype(BF16)
    r1 = x - hi.astype(F32)
    mid = r1.astype(BF16)
    lo = (r1 - mid.astype(F32)).astype(BF16)
    return hi, mid, lo


def _split3_lanes(x):
    hi, mid, lo = _bf16_pieces(x)
    return jnp.concatenate([hi.astype(F32), mid.astype(F32), lo.astype(F32)], axis=1).astype(BF16)


def _dot_exact_rhs(a_bf16, x):
    hi, mid, lo = _bf16_pieces(x)
    return _dot(a_bf16, hi) + _dot(a_bf16, mid) + _dot(a_bf16, lo)


def _silu(x):
    return x * jax.nn.sigmoid(x)


def _softplus(x):
    return jnp.maximum(x, 0.0) + jnp.log1p(jnp.exp(-jnp.abs(x)))


def _rms(x, w):
    return x * lax.rsqrt(jnp.mean(x * x, axis=-1, keepdims=True) + NORM_EPS) * w


def _const_spec(shape):
    nd = len(shape)
    return pl.BlockSpec(shape, lambda *_: (0,) * nd, pipeline_mode=pl.Buffered(1))


def _params2(sem=("arbitrary", "arbitrary")):
    return pltpu.CompilerParams(dimension_semantics=sem, vmem_limit_bytes=VMEM_LIMIT)


def _mod_kernel(c_ref, w_ref, b_ref, o_ref):
    a = _silu(c_ref[...]).astype(BF16)
    o_ref[...] = _dot(a, w_ref[...].astype(BF16)) + b_ref[...]


def _mod_call(c_all, w_ada, b_ada):
    m, d = c_all.shape
    n = w_ada.shape[1]
    tn = 512
    return pl.pallas_call(
        _mod_kernel,
        grid=(n // tn,),
        in_specs=[pl.BlockSpec((m, d), lambda j: (0, 0)),
                  pl.BlockSpec((d, tn), lambda j: (0, j)),
                  pl.BlockSpec((1, tn), lambda j: (0, j))],
        out_specs=pl.BlockSpec((m, tn), lambda j: (0, j)),
        out_shape=jax.ShapeDtypeStruct((m, n), F32),
        compiler_params=_params2(("arbitrary",)),
        name="adaln_mod",
    )(c_all, w_ada, b_ada.reshape(1, n))


def _proj_conv_silu(h, w_ref, w_col0, width, bb, l, tail_ref, convout_ref, cw_ref, cb_ref, out_ref, n_norm, norm_scale):
    tm = bb * l
    for k in range(width // CONV_COLS):
        cols = slice(k * CONV_COLS, (k + 1) * CONV_COLS)
        raw = _dot(h, w_ref[:, w_col0 + k * CONV_COLS:w_col0 + (k + 1) * CONV_COLS]).reshape(bb, l, CONV_COLS)
        ext = jnp.concatenate([tail_ref[:, :, cols], raw], axis=1)
        acc = ext[:, SUBLANES:SUBLANES + l] * cw_ref[CONV_WIDTH - 1:CONV_WIDTH, cols]
        for j in range(1, CONV_WIDTH):
            acc = acc + ext[:, SUBLANES - j:SUBLANES - j + l] * cw_ref[CONV_WIDTH - 1 - j:CONV_WIDTH - j, cols]
        tail_ref[:, :, cols] = ext[:, l:l + SUBLANES]
        convout_ref[:, :, cols] = ext[:, l + SUBLANES - (CONV_WIDTH - 1):l + SUBLANES]
        if cb_ref is not None:
            acc = acc + cb_ref[:, cols]
        act = _silu(acc).reshape(tm, CONV_COLS)
        for hh in range(CONV_COLS // LANES):
            head = k * (CONV_COLS // LANES) + hh
            xh = act[:, hh * LANES:(hh + 1) * LANES]
            if head < n_norm:
                xh = xh * (lax.rsqrt(jnp.sum(xh * xh, axis=-1, keepdims=True) + L2_EPS) * norm_scale[head])
            out_ref[:, head * LANES:(head + 1) * LANES] = xh


def _inproj_kernel(x_ref, mod_ref, nw_ref, wbig_ref, wsm_ref, sconv0_ref, gconv0_ref,
                   scw_ref, scb_ref, gcw_ref,
                   z_ref, xbc_ref, qkv_ref, gate_ref, sm_ref, sconv_ref, gconv_ref, stail_ref, gtail_ref):
    @pl.when(pl.program_id(1) == 0)
    def _():
        for tail_ref, conv0_ref in ((stail_ref, sconv0_ref), (gtail_ref, gconv0_ref)):
            tail_ref[...] = jnp.zeros(tail_ref.shape, F32)
            tail_ref[:, SUBLANES - (CONV_WIDTH - 1):, :] = conv0_ref[...]

    x = x_ref[...]
    bb, l, d = x.shape
    tm = bb * l
    mod = mod_ref[...]
    h = _rms(x, nw_ref[...]) * (1.0 + mod[:, 1:2, :]) + mod[:, 0:1, :]
    h = h.reshape(tm, d).astype(BF16)
    c_xbc = SSD_WIDTH
    c_qkv = c_xbc + SSD_CONV_DIM
    c_gate = c_qkv + GDN_CONV_DIM
    z_ref[...] = _dot(h, wbig_ref[:, :c_xbc])
    gate_ref[...] = _dot(h, wbig_ref[:, c_gate:])
    sm_ref[...] = _dot(h, wsm_ref[...])
    _proj_conv_silu(h, wbig_ref, c_xbc, SSD_CONV_DIM, bb, l, stail_ref, sconv_ref, scw_ref, scb_ref, xbc_ref, 0, ())
    qk_scale = (GDN_DIM ** -0.5,) * GDN_HEADS + (1.0,) * GDN_HEADS
    _proj_conv_silu(h, wbig_ref, c_qkv, GDN_CONV_DIM, bb, l, gtail_ref, gconv_ref, gcw_ref, None, qkv_ref,
                    2 * GDN_HEADS, qk_scale)


def _inproj_call(x, mod6, norm_w, w_big, w_small, sconv0, gconv0, scw, scb, gcw, bb, lb):
    b, l, d = x.shape
    nl = l // lb
    tm = bb * lb
    t = b * l
    row = lambda i, j: (i * nl + j, 0)
    seq = lambda i, j: (i, 0, 0)
    widths = (SSD_WIDTH, SSD_CONV_DIM, GDN_CONV_DIM, GDN_WIDTH, LANES)
    hist = CONV_WIDTH - 1
    return pl.pallas_call(
        _inproj_kernel,
        grid=(b // bb, nl),
        in_specs=[pl.BlockSpec((bb, lb, d), lambda i, j: (i, j, 0)),
                  pl.BlockSpec((bb,) + mod6.shape[1:], lambda i, j: (i, 0, 0)),
                  _const_spec((1, d)),
                  _const_spec(w_big.shape),
                  _const_spec(w_small.shape),
                  pl.BlockSpec((bb, hist, SSD_CONV_DIM), seq),
                  pl.BlockSpec((bb, hist, GDN_CONV_DIM), seq),
                  _const_spec(scw.shape), _const_spec(scb.shape), _const_spec(gcw.shape)],
        out_specs=[pl.BlockSpec((tm, w), row) for w in widths]
                  + [pl.BlockSpec((bb, hist, SSD_CONV_DIM), seq), pl.BlockSpec((bb, hist, GDN_CONV_DIM), seq)],
        out_shape=[jax.ShapeDtypeStruct((t, w), F32) for w in widths]
                  + [jax.ShapeDtypeStruct((b, hist, SSD_CONV_DIM), F32),
                     jax.ShapeDtypeStruct((b, hist, GDN_CONV_DIM), F32)],
        scratch_shapes=[pltpu.VMEM((bb, SUBLANES, SSD_CONV_DIM), F32),
                        pltpu.VMEM((bb, SUBLANES, GDN_CONV_DIM), F32)],
        compiler_params=_params2(),
        name="inproj",
    )(x, mod6, norm_w, w_big, w_small, sconv0, gconv0, scw, scb, gcw)


def _ssd_constants(q, nu):
    hq = SSD_HEADS * q
    hp = LANES // q
    w = hp * SSD_HEADDIM
    e_p = np.zeros((LANES, SSD_WIDTH), np.float32)
    e_q = np.zeros((LANES, hq), np.float32)
    for r in range(SSD_HEADS):
        e_p[DT_LANE0 + r, r * SSD_HEADDIM:(r + 1) * SSD_HEADDIM] = 1.0
        e_q[DT_LANE0 + r, r * q:(r + 1) * q] = 1.0
    tri = np.tril(np.ones((q, q), np.float32))
    gmask = np.zeros((hq, 2 * SSD_STATE), np.float32)
    half = hq // SSD_GROUPS
    gmask[:half, :SSD_STATE] = 1.0
    gmask[half:, SSD_STATE:] = 1.0
    xmask = np.kron(np.eye(hp, dtype=np.float32), np.ones((q, SSD_HEADDIM), np.float32))
    unit_eye = np.eye(nu, dtype=np.float32)
    return dict(
        l3=jnp.asarray(np.kron(unit_eye, tri), BF16),
        ones3=jnp.asarray(np.kron(unit_eye, np.ones((q, q), np.float32)), BF16),
        e3p=jnp.asarray(np.tile(e_p, (3, 1)), BF16),
        e3q=jnp.asarray(np.tile(e_q, (3, 1)), BF16),
        dmask=jnp.asarray(np.tile(np.eye(q, dtype=np.float32), (nu, SSD_HEADS))),
        causal=jnp.asarray(np.tile(tri, (nu, SSD_HEADS))),
        gmask=jnp.asarray(gmask),
        xmask=jnp.asarray(xmask),
    ), hp, w


def _ssd_kernel(xbc_ref, z_ref, sm_ref, h0_ref, dtb_ref, a_ref, d_ref, nw_ref,
                l3_ref, ones3_ref, e3p_ref, e3q_ref, dmask_ref, causal_ref, gmask_ref, xmask_ref,
                y_ref, hout_ref, ht_ref, *, q, hp, w, nseq, cps):
    c = pl.program_id(1)
    nc = pl.num_programs(1)
    nu = nseq * cps
    rows = nu * q
    gw = SSD_WIDTH // SSD_GROUPS

    @pl.when(c == 0)
    def _():
        for sq in range(nseq):
            ht_ref[sq] = h0_ref[sq].T

    def unit_rows(u):
        return slice(u * q, (u + 1) * q)

    xs = xbc_ref[:, :SSD_WIDTH]
    bcat = xbc_ref[:, SSD_WIDTH:SSD_WIDTH + 2 * SSD_STATE]
    ccat = xbc_ref[:, SSD_WIDTH + 2 * SSD_STATE:].astype(BF16)

    dt = _softplus(sm_ref[...] + dtb_ref[...])
    acs = _dot_exact_rhs(l3_ref[...], dt * a_ref[...])
    dec_ends = [jnp.exp(acs[(u + 1) * q - 1:(u + 1) * q, :] - acs[unit_rows(u)]) for u in range(nu)]
    dec_end = dec_ends[0] if nu == 1 else jnp.concatenate(dec_ends, axis=0)
    per_head = jnp.concatenate([dt, dt * dec_end, jnp.exp(acs)], axis=0)
    exp3 = _dot(_split3_lanes(per_head), e3p_ref[...])
    dt_e, dtd_e, eacs_e = exp3[:rows], exp3[rows:2 * rows], exp3[2 * rows:]

    col_all = _dot(_split3_lanes(acs), e3q_ref[...])
    row_all = _dot_exact_rhs(ones3_ref[...], col_all * dmask_ref[...])
    decay = jnp.exp(jnp.minimum(col_all - row_all, 0.0)) * causal_ref[...]
    xdt = xs * dt_e
    xd = (xs * dtd_e).astype(BF16)
    dxs = d_ref[...] * xs
    gmask = gmask_ref[...]
    xmask = xmask_ref[...]

    def cb_all(u):
        if 2 * q == LANES:
            per_group = []
            for g in range(SSD_GROUPS):
                gl = slice(g * SSD_STATE, (g + 1) * SSD_STATE)
                cb = _dot_nt(ccat[unit_rows(u), gl], bcat[unit_rows(u), gl].astype(BF16))
                per_group += [jnp.concatenate([cb, cb], axis=1)] * (SSD_HEADS // SSD_GROUPS // 2)
            return jnp.concatenate(per_group, axis=1)
        brep = (jnp.concatenate([bcat[unit_rows(u)]] * SSD_HEADS, axis=0) * gmask).astype(BF16)
        return _dot_nt(ccat[unit_rows(u)], brep)

    cbs = [cb_all(u) for u in range(nu)]
    m_alls = [(cbs[u] * decay[unit_rows(u)]).astype(BF16) for u in range(nu)]
    y_diags, upds = [], []
    for u in range(nu):
        blocks = []
        for j in range(SSD_WIDTH // w):
            xj = (jnp.concatenate([xdt[unit_rows(u), j * w:(j + 1) * w]] * hp, axis=0) * xmask).astype(BF16)
            blocks.append(_dot(m_alls[u][:, j * LANES:(j + 1) * LANES], xj))
        y_diags.append(blocks[0] if len(blocks) == 1 else jnp.concatenate(blocks, axis=1))
        upds.append(jnp.concatenate(
            [_dot_tn(bcat[unit_rows(u), g * SSD_STATE:(g + 1) * SSD_STATE].astype(BF16),
                     xd[unit_rows(u), g * gw:(g + 1) * gw]) for g in range(SSD_GROUPS)], axis=1))

    nw = nw_ref[...]
    for sq in range(nseq):
        ht = ht_ref[sq]
        for ci in range(cps):
            u = sq * cps + ci
            ht_bf = ht.astype(BF16)
            y_off = jnp.concatenate([_dot(ccat[unit_rows(u), g * SSD_STATE:(g + 1) * SSD_STATE],
                                          ht_bf[:, g * gw:(g + 1) * gw]) for g in range(SSD_GROUPS)], axis=1)
            e_u = eacs_e[unit_rows(u)]
            y = y_diags[u] + y_off * e_u + dxs[unit_rows(u)]
            ht = ht * e_u[q - 1:q, :] + upds[u]
            yz = y * _silu(z_ref[unit_rows(u)])
            outs = [_rms(yz[:, g * gw:(g + 1) * gw], nw[:, g * gw:(g + 1) * gw]) for g in range(SSD_GROUPS)]
            y_ref[unit_rows(u)] = jnp.concatenate(outs, axis=1).astype(y_ref.dtype)
        ht_ref[sq] = ht

    @pl.when(c == nc - 1)
    def _():
        for sq in range(nseq):
            hout_ref[sq] = ht_ref[sq].T


def _scan_units(b, l, q):
    nchunks = l // q
    cps = SCAN_UNITS_PER_STEP if nchunks % SCAN_UNITS_PER_STEP == 0 else 1
    nseq = SCAN_UNITS_PER_STEP if nchunks == 1 and b % SCAN_UNITS_PER_STEP == 0 else 1
    return nseq, cps


def _ssd_call(xbc, z, small, h0, dtb, a_pad, d_e, nw, b, l, q):
    nseq, cps = _scan_units(b, l, q)
    consts, hp, w = _ssd_constants(q, nseq * cps)
    nc = l // (q * cps)
    t = b * l
    rows = nseq * cps * q
    row = lambda i, c: (i * nc + c, 0)
    seq = lambda i, c: (i, 0, 0)
    params = [dtb, a_pad, d_e, nw] + [consts[k] for k in
                                      ("l3", "ones3", "e3p", "e3q", "dmask", "causal", "gmask", "xmask")]
    hp_shape = (SSD_WIDTH, SSD_STATE)
    return pl.pallas_call(
        functools.partial(_ssd_kernel, q=q, hp=hp, w=w, nseq=nseq, cps=cps),
        grid=(b // nseq, nc),
        in_specs=[pl.BlockSpec((rows, SSD_CONV_DIM), row),
                  pl.BlockSpec((rows, SSD_WIDTH), row),
                  pl.BlockSpec((rows, LANES), row),
                  pl.BlockSpec((nseq,) + hp_shape, seq)]
                 + [_const_spec(p.shape) for p in params],
        out_specs=[pl.BlockSpec((rows, SSD_WIDTH), row),
                   pl.BlockSpec((nseq,) + hp_shape, seq)],
        out_shape=[jax.ShapeDtypeStruct((t, SSD_WIDTH), F32),
                   jax.ShapeDtypeStruct((b,) + hp_shape, F32)],
        scratch_shapes=[pltpu.VMEM((nseq, SSD_STATE, SSD_WIDTH), F32)],
        compiler_params=_params2(),
        name="ssd_scan",
    )(xbc, z, small, h0, *params)


def _gdn_constants(q, nseq, hpb, cps):
    assert nseq * hpb * q == MXU_DIM
    nhb = GDN_HEADS // hpb
    rows = nseq * q
    e_beta = np.zeros((LANES, GDN_WIDTH), np.float32)
    e_alpha = np.zeros((LANES, GDN_WIDTH), np.float32)
    e_c = np.zeros((LANES, nhb * MXU_DIM), np.float32)
    for h in range(GDN_HEADS):
        e_beta[BETA_LANE0 + h, h * GDN_DIM:(h + 1) * GDN_DIM] = 1.0
        e_alpha[ALPHA_LANE0 + h, h * GDN_DIM:(h + 1) * GDN_DIM] = 1.0
    jh, seq_c, hh_c, s_c = np.unravel_index(np.arange(nhb * MXU_DIM), (nhb, nseq, hpb, q))
    e_c[ALPHA_LANE0 + jh * hpb + hh_c, np.arange(nhb * MXU_DIM)] = 1.0
    seq_r, l_r = np.unravel_index(np.arange(rows), (nseq, q))
    same_seq = seq_r[:, None] == seq_c[None, :]
    dmask = same_seq & (l_r[:, None] == s_c[None, :])
    incl = same_seq & (l_r[:, None] >= s_c[None, :])
    strict = same_seq & (l_r[:, None] > s_c[None, :])
    seq_s, hh_s, _ = np.unravel_index(np.arange(MXU_DIM), (nseq, hpb, q))
    bdmask = (seq_s[:, None] == seq_c[None, :MXU_DIM]) & (hh_s[:, None] == hh_c[None, :MXU_DIM])
    kmask = hh_s[:, None] == (np.arange(hpb * GDN_DIM) // GDN_DIM)[None, :]
    f = lambda m: jnp.asarray(m.astype(np.float32))
    per_chunk = lambda m: jnp.asarray(np.tile(m.astype(np.float32), (cps, 1)))
    chunk_eye = np.eye(cps)
    mask_dtype = BF16 if q % (2 * SUBLANES) == 0 else F32
    return dict(
        l3=jnp.asarray(np.kron(np.eye(cps * nseq), np.tril(np.ones((q, q)))), BF16),
        ones3=jnp.asarray(np.kron(chunk_eye, np.ones((rows, rows))), BF16),
        e3b=jnp.asarray(np.tile(e_beta, (3, 1)), BF16),
        e3a=jnp.asarray(np.tile(e_alpha, (3, 1)), BF16),
        e3c=jnp.asarray(np.tile(e_c, (3, 1)), BF16),
        dmask=per_chunk(dmask), incl=per_chunk(incl), strict=f(strict), eye=f(dmask[:, :MXU_DIM]),
        bdmask=f(bdmask).astype(mask_dtype), kmask=f(kmask).astype(mask_dtype),
    )


def _gdn_kernel(qkv_ref, gate_ref, sm_ref, s0_ref, dtb_ref, a_ref, nw_ref,
                l3_ref, ones3_ref, e3b_ref, e3a_ref, e3c_ref, dmask_ref, incl_ref, strict_ref, eye_ref,
                bdmask_ref, kmask_ref,
                o_ref, sout_ref, s_ref, *, q, nseq, hpb, cps):
    c = pl.program_id(1)
    nc = pl.num_programs(1)
    nhb = GDN_HEADS // hpb
    rows = nseq * q
    bw = hpb * GDN_DIM
    n_sq = q.bit_length() - 1
    assert q == 1 << n_sq and n_sq >= 2

    @pl.when(c == 0)
    def _():
        s_ref[...] = s0_ref[...]

    bdmask = bdmask_ref[...]
    kmask = kmask_ref[...]
    mask_dtype = bdmask.dtype

    def head_lanes(h):
        return slice(h * GDN_DIM, (h + 1) * GDN_DIM)

    def seq_rows(sq):
        return slice(sq * q, (sq + 1) * q)

    def chunk_rows(ci):
        return slice(ci * rows, (ci + 1) * rows)

    def stack_rows(x):
        return jnp.concatenate([x[seq_rows(sq)] for sq in range(nseq) for _ in range(hpb)], axis=0)

    def masked_bf16(x, mask):
        return (x.astype(mask_dtype) * mask).astype(BF16)

    def block_diag(m_bf16):
        return masked_bf16(stack_rows(m_bf16.astype(mask_dtype)), bdmask)

    def block_diag_pieces(m):
        hi = m.astype(BF16)
        lo = (m - hi.astype(F32)).astype(BF16)
        return block_diag(hi), block_diag(lo)

    def lhs_pieces(m):
        hi = m.astype(BF16)
        return hi, (m - hi.astype(F32)).astype(BF16)

    sm = sm_ref[...]
    beta = jax.nn.sigmoid(sm)
    g = a_ref[...] * _softplus(sm + dtb_ref[...])
    gc = _dot_exact_rhs(l3_ref[...], g)
    gc3 = _split3_lanes(gc)
    beta_e = _dot(_split3_lanes(beta), e3b_ref[...])
    gc_e = _dot(gc3, e3a_ref[...])
    col_c = _dot(gc3, e3c_ref[...])
    row_c = _dot_exact_rhs(ones3_ref[...], col_c * dmask_ref[...])
    decay_c = jnp.exp(jnp.minimum(col_c - row_c, 0.0)) * incl_ref[...]
    egc_e = jnp.exp(gc_e)
    qn = qkv_ref[:, :GDN_WIDTH]
    kn = qkv_ref[:, GDN_WIDTH:2 * GDN_WIDTH]
    kb = kn * beta_e
    vb = qkv_ref[:, 2 * GDN_WIDTH:] * beta_e
    kbg = kb * egc_e
    qin = qn * egc_e

    def chunk_local(ci):
        r0 = ci * rows
        lasts = [gc_e[r0 + (sq + 1) * q - 1:r0 + (sq + 1) * q, :] for sq in range(nseq)]
        eouts = [jnp.exp(lasts[sq] - gc_e[r0 + sq * q:r0 + (sq + 1) * q]) for sq in range(nseq)]
        cr = chunk_rows(ci)
        return dict(
            qn=qn[cr], kn=kn[cr], kb=kb[cr], vb=vb[cr], kbg=kbg[cr], qin=qin[cr], decay_c=decay_c[cr],
            kout=(kn[cr] * (eouts[0] if nseq == 1 else jnp.concatenate(eouts, axis=0))).astype(BF16),
            ge=[jnp.exp(last) for last in lasts])

    def chain_init(loc, jb):
        nl = slice(jb * bw, (jb + 1) * bw)
        cl = slice(jb * MXU_DIM, (jb + 1) * MXU_DIM)
        k_bd = masked_bf16(stack_rows(loc["kn"][:, nl].astype(mask_dtype)), kmask)
        prod = _dot_nt(jnp.concatenate([loc["kb"][:, nl], loc["qn"][:, nl]], axis=0).astype(BF16), k_bd)
        dec = loc["decay_c"][:, cl]
        x = -(prod[:rows] * dec * strict_ref[:, cl])
        return dict(inv=eye_ref[...] + x, ypow=x, attn=(prod[rows:] * dec).astype(BF16))

    def double_step(st, k):
        first, last = k == 0, k == n_sq - 1
        inv, ypow = st["inv"], st["ypow"]
        b_hi, b_lo = block_diag_pieces(ypow)
        lhs = ypow if first else (inv if last else jnp.concatenate([inv, ypow], axis=0))
        l_hi, l_lo = lhs_pieces(lhs)
        res = _dot(jnp.concatenate([l_hi, l_hi, l_lo], axis=1), jnp.concatenate([b_hi, b_lo, b_hi], axis=0))
        if first:
            return dict(st, ypow=res)
        if last:
            return dict(st, inv=inv + res)
        return dict(st, inv=inv + res[:rows], ypow=res[rows:])

    def solve(loc, jb, inv):
        i_hi, i_lo = block_diag_pieces(inv)
        segs = [(sq, jb * hpb + hh) for sq in range(nseq) for hh in range(hpb)]
        rhs = jnp.concatenate([jnp.concatenate([loc["vb"][seq_rows(sq), head_lanes(h)],
                                                loc["kbg"][seq_rows(sq), head_lanes(h)]], axis=1)
                               for sq, h in segs], axis=0).astype(BF16)
        return _dot(jnp.concatenate([i_hi, i_lo], axis=1), jnp.concatenate([rhs, rhs], axis=0))

    def state_step(loc, jb, uw, attn):
        segs = [(sq, jb * hpb + hh) for sq in range(nseq) for hh in range(hpb)]
        ws, qs, states = [], [], []
        for i, (sq, h) in enumerate(segs):
            s = s_ref[sq, h]
            states.append(s)
            wq = jnp.concatenate([uw[i * q:(i + 1) * q, GDN_DIM:], loc["qin"][seq_rows(sq), head_lanes(h)]], axis=0)
            r = _dot(wq.astype(BF16), s.astype(BF16))
            ws.append(r[:q])
            qs.append(r[q:])
        v_bf = (uw[:, :GDN_DIM] - jnp.concatenate(ws, axis=0)).astype(BF16)
        v_rep = masked_bf16(jnp.concatenate([v_bf.astype(mask_dtype)] * hpb, axis=1), kmask)
        qs_rows = [jnp.concatenate(qs[sq * hpb:(sq + 1) * hpb], axis=1) for sq in range(nseq)]
        qs_nat = qs_rows[0] if nseq == 1 else jnp.concatenate(qs_rows, axis=0)
        for i, (sq, h) in enumerate(segs):
            s_ref[sq, h] = (states[i] * loc["ge"][sq][:, head_lanes(h)]
                            + _dot_tn(loc["kout"][seq_rows(sq), head_lanes(h)], v_bf[i * q:(i + 1) * q]))
        return qs_nat + _dot(attn, v_rep)

    local = [chunk_local(ci) for ci in range(cps)]
    chains = [(ci, jb) for ci in range(cps) for jb in range(nhb)]
    st = [chain_init(local[ci], jb) for ci, jb in chains]
    for k in range(n_sq):
        st = [double_step(s, k) for s in st]
    uws =[solve(local[ci], jb, s["inv"]) for (ci, jb), s in zip(chains, st)]
    nw = nw_ref[...]
    for ci in range(cps):
        o_blocks = [state_step(local[ci], jb, uws[ci * nhb + jb], st[ci * nhb + jb]["attn"]) for jb in range(nhb)]
        o = o_blocks[0] if nhb == 1 else jnp.concatenate(o_blocks, axis=1)
        gate = gate_ref[chunk_rows(ci)]
        outs = [_rms(o[:, head_lanes(h)], nw) * _silu(gate[:, head_lanes(h)]) for h in range(GDN_HEADS)]
        o_ref[chunk_rows(ci)] = jnp.concatenate(outs, axis=1).astype(o_ref.dtype)

    @pl.when(c == nc - 1)
    def _():
        sout_ref[...] = s_ref[...]


def _gdn_call(qkv, gate, small, s0, dtb, a_pad, nw, b, l, q):
    nseq = max(1, MXU_DIM // (GDN_HEADS * q))
    hpb = MXU_DIM // (nseq * q)
    cps = SCAN_UNITS_PER_STEP if nseq == 1 and (l // q) % SCAN_UNITS_PER_STEP == 0 else 1
    consts = _gdn_constants(q, nseq, hpb, cps)
    nc = l // (q * cps)
    assert b % nseq == 0 and (nseq == 1 or nc == 1)
    t = b * l
    rows = nseq * q * cps
    row = lambda i, c: (i * nc + c, 0)
    seq = lambda i, c: (i, 0, 0, 0)
    params = [dtb, a_pad, nw] + [consts[k] for k in
                                 ("l3", "ones3", "e3b", "e3a", "e3c", "dmask", "incl", "strict", "eye",
                                  "bdmask", "kmask")]
    s_shape = (GDN_HEADS, GDN_DIM, GDN_DIM)
    return pl.pallas_call(
        functools.partial(_gdn_kernel, q=q, nseq=nseq, hpb=hpb, cps=cps),
        grid=(b // nseq, nc),
        in_specs=[pl.BlockSpec((rows, GDN_CONV_DIM), row),
                  pl.BlockSpec((rows, GDN_WIDTH), row),
                  pl.BlockSpec((rows, LANES), row),
                  pl.BlockSpec((nseq,) + s_shape, seq)]
                 + [_const_spec(p.shape) for p in params],
        out_specs=[pl.BlockSpec((rows, GDN_WIDTH), row),
                   pl.BlockSpec((nseq,) + s_shape, seq)],
        out_shape=[jax.ShapeDtypeStruct((t, GDN_WIDTH), F32),
                   jax.ShapeDtypeStruct((b,) + s_shape, F32)],
        scratch_shapes=[pltpu.VMEM((nseq,) + s_shape, F32)],
        compiler_params=_params2(),
        name="gdn_scan",
    )(qkv, gate, small, s0, *params)


def _outffn_kernel(x_ref, y_ref, o_ref, mod_ref, nwm_ref, nwf_ref, nwp_ref,
                   wout_ref, wgu_ref, wdown_ref, out_ref, *, ff_chunk):
    x = x_ref[...]
    shape3 = x.shape
    tm = shape3[0] * shape3[1]
    mod = mod_ref[...]
    g1, sh2, sc2, g2 = (mod[:, k:k + 1, :] for k in range(2, 6))
    m = (_dot(y_ref[...].astype(BF16), wout_ref[:SSD_WIDTH, :])
         + _dot(o_ref[...].astype(BF16), wout_ref[SSD_WIDTH:, :]))
    x1 = x + g1 * _rms(m, nwm_ref[...]).reshape(shape3)
    h2 = _rms(x1, nwf_ref[...]) * (1.0 + sc2) + sh2
    h2 = h2.reshape(tm, shape3[2]).astype(BF16)
    f = jnp.zeros((tm, shape3[2]), F32)
    for j in range(D_FF // ff_chunk):
        gj = _dot(h2, wgu_ref[:, j * ff_chunk:(j + 1) * ff_chunk])
        uj = _dot(h2, wgu_ref[:, D_FF + j * ff_chunk:D_FF + (j + 1) * ff_chunk])
        f = f + _dot((_silu(gj) * uj).astype(BF16), wdown_ref[j * ff_chunk:(j + 1) * ff_chunk, :])
    out_ref[...] = x1 + g2 * _rms(f, nwp_ref[...]).reshape(shape3)


def _outffn_call(x, y, o, mod6, nw_mix_post, nw_ffn_pre, nw_ffn_post, w_out, w_gu, w_down, bb, lb):
    b, l, d = x.shape
    nl = l // lb
    tm = bb * lb
    row = lambda i, j: (i * nl + j, 0)
    return pl.pallas_call(
        functools.partial(_outffn_kernel, ff_chunk=D_FF // 2),
        grid=(b // bb, nl),
        in_specs=[pl.BlockSpec((bb, lb, d), lambda i, j: (i, j, 0)),
                  pl.BlockSpec((tm, SSD_WIDTH), row),
                  pl.BlockSpec((tm, GDN_WIDTH), row),
                  pl.BlockSpec((bb,) + mod6.shape[1:], lambda i, j: (i, 0, 0)),
                  _const_spec((1, d)), _const_spec((1, d)), _const_spec((1, d)),
                  _const_spec(w_out.shape), _const_spec(w_gu.shape), _const_spec(w_down.shape)],
        out_specs=pl.BlockSpec((bb, lb, d), lambda i, j: (i, j, 0)),
        out_shape=jax.ShapeDtypeStruct((b, l, d), F32),
        compiler_params=_params2(),
        name="outproj_ffn",
    )(x, y, o, mod6, nw_mix_post, nw_ffn_pre, nw_ffn_post, w_out, w_gu, w_down)


_O_DT = SSD_WIDTH + SSD_CONV_DIM
_O_QKV = _O_DT + SSD_HEADS
_O_BETA = _O_QKV + GDN_CONV_DIM + GDN_WIDTH
_IN_DIM = _O_BETA + 2 * GDN_HEADS


def _repack_kernel(w_ref, big_ref, small_ref):
    w = w_ref[...]
    big_ref[:, :_O_DT] = w[:, :_O_DT].astype(BF16)
    big_ref[:, _O_DT:] = w[:, _O_QKV:_O_BETA].astype(BF16)
    pad = jnp.zeros((w.shape[0], LANES - SSD_HEADS - 2 * GDN_HEADS), F32)
    small_ref[...] = jnp.concatenate([w[:, _O_DT:_O_QKV], w[:, _O_BETA:], pad], axis=1).astype(BF16)


def _repack_w_in(w):
    k, n = w.shape
    assert n == _IN_DIM
    big = n - SSD_HEADS - 2 * GDN_HEADS
    tr = 128
    return pl.pallas_call(
        _repack_kernel,
        grid=(k // tr,),
        in_specs=[pl.BlockSpec((tr, n), lambda i: (i, 0))],
        out_specs=[pl.BlockSpec((tr, big), lambda i: (i, 0)), pl.BlockSpec((tr, LANES), lambda i: (i, 0))],
        out_shape=[jax.ShapeDtypeStruct((k, big), BF16), jax.ShapeDtypeStruct((k, LANES), BF16)],
        compiler_params=_params2(("arbitrary",)),
        name="repack_w_in",
    )(w)


def _pad_lanes(v, lane0):
    out = jnp.zeros((1, LANES), F32)
    return out.at[0, lane0:lane0 + v.shape[0]].set(v.astype(F32))


def _layer(x, mod, ssd_h0, ssd_conv0, gdn_s0, gdn_conv0, p, q, bb, lb):
    b, l, d = x.shape
    mod6 = mod.reshape(b, 6, d)
    z, xbc, qkv, gate, small, ssd_conv, gdn_conv = _inproj_call(
        x, mod6, p["norm_mix_pre"], p["w_big"], p["w_small"], ssd_conv0, gdn_conv0,
        p["ssd_conv_w"], p["ssd_conv_b"], p["gdn_conv_w"], bb, lb)
    y, ssd_h = _ssd_call(xbc, z, small, ssd_h0.reshape(b, SSD_WIDTH, SSD_STATE),
                         p["ssd_dtb"], p["ssd_a"], p["ssd_d"], p["ssd_norm_w"], b, l, q)
    o, gdn_s = _gdn_call(qkv, gate, small, gdn_s0, p["gdn_dtb"], p["gdn_a"], p["gdn_norm_w"], b, l, q)
    out = _outffn_call(x, y, o, mod6, p["norm_mix_post"], p["norm_ffn_pre"], p["norm_ffn_post"],
                       p["w_out"], p["w_gu"], p["w_down"], bb, lb)
    return out, ssd_h.reshape(ssd_h0.shape), ssd_conv, gdn_s, gdn_conv


def _layer_params(l, w_ada, b_ada, norm_mix_pre, norm_mix_post, norm_ffn_pre, norm_ffn_post, w_in, ssd_conv_w,
                  ssd_conv_b, ssd_dt_bias, ssd_A_log, ssd_D, ssd_norm_w, gdn_conv_w, gdn_dt_bias, gdn_A_log,
                  gdn_norm_w, w_out, w_gate_up, w_down):
    w_big, w_small = _repack_w_in(w_in[l])
    row = lambda v: v.reshape(1, -1).astype(F32)
    return dict(
        w_ada=w_ada[l], b_ada=b_ada[l],
        norm_mix_pre=row(norm_mix_pre[l]), norm_mix_post=row(norm_mix_post[l]),
        norm_ffn_pre=row(norm_ffn_pre[l]), norm_ffn_post=row(norm_ffn_post[l]),
        w_big=w_big, w_small=w_small,
        ssd_conv_w=ssd_conv_w[l], ssd_conv_b=row(ssd_conv_b[l]),
        ssd_dtb=_pad_lanes(ssd_dt_bias[l], DT_LANE0),
        ssd_a=_pad_lanes(-jnp.exp(ssd_A_log[l].astype(F32)), DT_LANE0),
        ssd_d=row(jnp.repeat(ssd_D[l], SSD_HEADDIM)),
        ssd_norm_w=row(ssd_norm_w[l]),
        gdn_conv_w=gdn_conv_w[l],
        gdn_dtb=_pad_lanes(gdn_dt_bias[l], ALPHA_LANE0),
        gdn_a=_pad_lanes(-jnp.exp(gdn_A_log[l].astype(F32)), ALPHA_LANE0),
        gdn_norm_w=row(gdn_norm_w[l]),
        w_out=w_out[l].astype(BF16), w_gu=w_gate_up[l].astype(BF16), w_down=w_down[l].astype(BF16),
    )


def kernel(x_prompt, x_sample, c_prompt, c_sample, state_ssd, state_ssd_conv, state_gdn, state_gdn_conv, w_ada, b_ada, norm_mix_pre, norm_mix_post, norm_ffn_pre, norm_ffn_post, w_in, ssd_conv_w, ssd_conv_b, ssd_dt_bias, ssd_A_log, ssd_D, ssd_norm_w, gdn_conv_w, gdn_dt_bias, gdn_A_log, gdn_norm_w, w_out, w_gate_up, w_down):
    depth = w_in.shape[0]
    bp, lp, _ = x_prompt.shape
    bs, ls, _ = x_sample.shape
    yp, ys = x_prompt, x_sample
    outs = [[] for _ in range(8)]
    for l in range(depth):
        p = _layer_params(l, w_ada, b_ada, norm_mix_pre, norm_mix_post, norm_ffn_pre, norm_ffn_post, w_in,
                          ssd_conv_w, ssd_conv_b, ssd_dt_bias, ssd_A_log, ssd_D, ssd_norm_w, gdn_conv_w,
                          gdn_dt_bias, gdn_A_log, gdn_norm_w, w_out, w_gate_up, w_down)
        mod = _mod_call(jnp.concatenate([c_prompt, c_sample], axis=0), p["w_ada"], p["b_ada"])
        zeros = lambda a: jnp.zeros((bp,) + a.shape[2:], a.dtype)
        yp, a0, a1, a2, a3 = _layer(yp, mod[:bp], zeros(state_ssd), zeros(state_ssd_conv), zeros(state_gdn),
                                    zeros(state_gdn_conv), p, q=min(CHUNK, lp), bb=1, lb=min(ROW_TILE, lp))
        ys, b0, b1, b2, b3 = _layer(ys, mod[bp:], state_ssd[l], state_ssd_conv[l], state_gdn[l],
                                    state_gdn_conv[l], p, q=min(CHUNK, ls), bb=min(ROW_TILE // ls, bs), lb=ls)
        for lst, v in zip(outs, (a0, a1, a2, a3, b0, b1, b2, b3)):
            lst.append(v)
    return (yp, ys) + tuple(v[0][None] if depth == 1 else jnp.stack(v) for v in outs)
```

```python
import functools

import numpy as np
import jax
import jax.numpy as jnp
from jax import lax
from jax.experimental import pallas as pl
from jax.experimental.pallas import tpu as pltpu

F32 = jnp.float32
BF16 = jnp.bfloat16

D_MODEL = 1024
SSD_HEADS = 16
SSD_HEADDIM = 64
SSD_GROUPS = 2
SSD_STATE = 128
SSD_WIDTH = SSD_HEADS * SSD_HEADDIM
SSD_CONV_DIM = SSD_WIDTH + 2 * SSD_GROUPS * SSD_STATE
GDN_HEADS = 8
GDN_DIM = 128
GDN_WIDTH = GDN_HEADS * GDN_DIM
GDN_CONV_DIM = 3 * GDN_WIDTH
CONV_WIDTH = 4
CHUNK = 64
D_FF = 2816
NORM_EPS = 1e-6
L2_EPS = 1e-6
LANES = 128
SUBLANES = 8
MXU_DIM = 256
DT_LANE0, BETA_LANE0, ALPHA_LANE0 = 0, SSD_HEADS, SSD_HEADS + GDN_HEADS
VMEM_LIMIT = 56 * 1024 * 1024
ROW_TILE = 256
SCAN_UNITS_PER_STEP = 4
CONV_COLS = 512


def _dot(a, b):
    return jnp.dot(a, b, preferred_element_type=F32)


def _dot_nt(a, b):
    return lax.dot_general(a, b, (((1,), (1,)), ((), ())), preferred_element_type=F32)


def _dot_tn(a, b):
    return lax.dot_general(a, b, (((0,), (0,)), ((), ())), preferred_element_type=F32)


def _bf16_pieces(x):
    hi = x.astype(BF16)
    r1 = x - hi.astype(F32)
    mid = r1.astype(BF16)
    lo = (r1 - mid.astype(F32)).astype(BF16)
    return hi, mid, lo


def _split3_lanes(x):
    hi, mid, lo = _bf16_pieces(x)
    return jnp.concatenate([hi.astype(F32), mid.astype(F32), lo.astype(F32)], axis=1).astype(BF16)


def _dot_exact_rhs(a_bf16, x):
    hi, mid, lo = _bf16_pieces(x)
    return _dot(a_bf16, hi) + _dot(a_bf16, mid) + _dot(a_bf16, lo)


def _silu(x):
    return x * jax.nn.sigmoid(x)


def _softplus(x):
    return jnp.maximum(x, 0.0) + jnp.log1p(jnp.exp(-jnp.abs(x)))


def _rms(x, w):
    return x * lax.rsqrt(jnp.mean(x * x, axis=-1, keepdims=True) + NORM_EPS) * w


def _const_spec(shape):
    nd = len(shape)
    return pl.BlockSpec(shape, lambda *_: (0,) * nd, pipeline_mode=pl.Buffered(1))


def _params2(sem=("arbitrary", "arbitrary")):
    return pltpu.CompilerParams(dimension_semantics=sem, vmem_limit_bytes=VMEM_LIMIT)


def _mod_kernel(c_ref, w_ref, b_ref, o_ref):
    a = _silu(c_ref[...]).astype(BF16)
    o_ref[...] = _dot(a, w_ref[...].astype(BF16)) + b_ref[...]


def _mod_call(c_all, w_ada, b_ada):
    m, d = c_all.shape
    n = w_ada.shape[1]
    tn = 512
    return pl.pallas_call(
        _mod_kernel,
        grid=(n // tn,),
        in_specs=[pl.BlockSpec((m, d), lambda j: (0, 0)),
                  pl.BlockSpec((d, tn), lambda j: (0, j)),
                  pl.BlockSpec((1, tn), lambda j: (0, j))],
        out_specs=pl.BlockSpec((m, tn), lambda j: (0, j)),
        out_shape=jax.ShapeDtypeStruct((m, n), F32),
        compiler_params=_params2(("arbitrary",)),
        name="adaln_mod",
    )(c_all, w_ada, b_ada.reshape(1, n))


def _proj_conv_silu(h, w_ref, w_col0, width, bb, l, tail_ref, convout_ref, cw_ref, cb_ref, out_ref, n_norm, norm_scale):
    tm = bb * l
    for k in range(width // CONV_COLS):
        cols = slice(k * CONV_COLS, (k + 1) * CONV_COLS)
        raw = _dot(h, w_ref[:, w_col0 + k * CONV_COLS:w_col0 + (k + 1) * CONV_COLS]).reshape(bb, l, CONV_COLS)
        ext = jnp.concatenate([tail_ref[:, :, cols], raw], axis=1)
        acc = ext[:, SUBLANES:SUBLANES + l] * cw_ref[CONV_WIDTH - 1:CONV_WIDTH, cols]
        for j in range(1, CONV_WIDTH):
            acc = acc + ext[:, SUBLANES - j:SUBLANES - j + l] * cw_ref[CONV_WIDTH - 1 - j:CONV_WIDTH - j, cols]
        tail_ref[:, :, cols] = ext[:, l:l + SUBLANES]
        convout_ref[:, :, cols] = ext[:, l + SUBLANES - (CONV_WIDTH - 1):l + SUBLANES]
        if cb_ref is not None:
            acc = acc + cb_ref[:, cols]
        act = _silu(acc).reshape(tm, CONV_COLS)
        for hh in range(CONV_COLS // LANES):
            head = k * (CONV_COLS // LANES) + hh
            xh = act[:, hh * LANES:(hh + 1) * LANES]
            if head < n_norm:
                xh = xh * (lax.rsqrt(jnp.sum(xh * xh, axis=-1, keepdims=True) + L2_EPS) * norm_scale[head])
            out_ref[:, head * LANES:(head + 1) * LANES] = xh


def _inproj_kernel(x_ref, mod_ref, nw_ref, wbig_ref, wsm_ref, sconv0_ref, gconv0_ref,
                   scw_ref, scb_ref, gcw_ref,
                   z_ref, xbc_ref, qkv_ref, gate_ref, sm_ref, sconv_ref, gconv_ref, stail_ref, gtail_ref):
    @pl.when(pl.program_id(1) == 0)
    def _():
        for tail_ref, conv0_ref in ((stail_ref, sconv0_ref), (gtail_ref, gconv0_ref)):
            tail_ref[...] = jnp.zeros(tail_ref.shape, F32)
            tail_ref[:, SUBLANES - (CONV_WIDTH - 1):, :] = conv0_ref[...]

    x = x_ref[...]
    bb, l, d = x.shape
    tm = bb * l
    mod = mod_ref[...]
    h = _rms(x, nw_ref[...]) * (1.0 + mod[:, 1:2, :]) + mod[:, 0:1, :]
    h = h.reshape(tm, d).astype(BF16)
    c_xbc = SSD_WIDTH
    c_qkv = c_xbc + SSD_CONV_DIM
    c_gate = c_qkv + GDN_CONV_DIM
    z_ref[...] = _dot(h, wbig_ref[:, :c_xbc])
    gate_ref[...] = _dot(h, wbig_ref[:, c_gate:])
    sm_ref[...] = _dot(h, wsm_ref[...])
    _proj_conv_silu(h, wbig_ref, c_xbc, SSD_CONV_DIM, bb, l, stail_ref, sconv_ref, scw_ref, scb_ref, xbc_ref, 0, ())
    qk_scale = (GDN_DIM ** -0.5,) * GDN_HEADS + (1.0,) * GDN_HEADS
    _proj_conv_silu(h, wbig_ref, c_qkv, GDN_CONV_DIM, bb, l, gtail_ref, gconv_ref, gcw_ref, None, qkv_ref,
                    2 * GDN_HEADS, qk_scale)


def _inproj_call(x, mod6, norm_w, w_big, w_small, sconv0, gconv0, scw, scb, gcw, bb, lb):
    b, l, d = x.shape
    nl = l // lb
    tm = bb * lb
    t = b * l
    row = lambda i, j: (i * nl + j, 0)
    seq = lambda i, j: (i, 0, 0)
    widths = (SSD_WIDTH, SSD_CONV_DIM, GDN_CONV_DIM, GDN_WIDTH, LANES)
    hist = CONV_WIDTH - 1
    return pl.pallas_call(
        _inproj_kernel,
        grid=(b // bb, nl),
        in_specs=[pl.BlockSpec((bb, lb, d), lambda i, j: (i, j, 0)),
                  pl.BlockSpec((bb,) + mod6.shape[1:], lambda i, j: (i, 0, 0)),
                  _const_spec((1, d)),
                  _const_spec(w_big.shape),
                  _const_spec(w_small.shape),
                  pl.BlockSpec((bb, hist, SSD_CONV_DIM), seq),
                  pl.BlockSpec((bb, hist, GDN_CONV_DIM), seq),
                  _const_spec(scw.shape), _const_spec(scb.shape), _const_spec(gcw.shape)],
        out_specs=[pl.BlockSpec((tm, w), row) for w in widths]
                  + [pl.BlockSpec((bb, hist, SSD_CONV_DIM), seq), pl.BlockSpec((bb, hist, GDN_CONV_DIM), seq)],
        out_shape=[jax.ShapeDtypeStruct((t, w), F32) for w in widths]
                  + [jax.ShapeDtypeStruct((b, hist, SSD_CONV_DIM), F32),
                     jax.ShapeDtypeStruct((b, hist, GDN_CONV_DIM), F32)],
        scratch_shapes=[pltpu.VMEM((bb, SUBLANES, SSD_CONV_DIM), F32),
                        pltpu.VMEM((bb, SUBLANES, GDN_CONV_DIM), F32)],
        compiler_params=_params2(),
        name="inproj",
    )(x, mod6, norm_w, w_big, w_small, sconv0, gconv0, scw, scb, gcw)


def _ssd_constants(q, nu):
    hq = SSD_HEADS * q
    hp = LANES // q
    w = hp * SSD_HEADDIM
    e_p = np.zeros((LANES, SSD_WIDTH), np.float32)
    e_q = np.zeros((LANES, hq), np.float32)
    for r in range(SSD_HEADS):
        e_p[DT_LANE0 + r, r * SSD_HEADDIM:(r + 1) * SSD_HEADDIM] = 1.0
        e_q[DT_LANE0 + r, r * q:(r + 1) * q] = 1.0
    tri = np.tril(np.ones((q, q), np.float32))
    gmask = np.zeros((hq, 2 * SSD_STATE), np.float32)
    half = hq // SSD_GROUPS
    gmask[:half, :SSD_STATE] = 1.0
    gmask[half:, SSD_STATE:] = 1.0
    xmask = np.kron(np.eye(hp, dtype=np.float32), np.ones((q, SSD_HEADDIM), np.float32))
    unit_eye = np.eye(nu, dtype=np.float32)
    return dict(
        l3=jnp.asarray(np.kron(unit_eye, tri), BF16),
        ones3=jnp.asarray(np.kron(unit_eye, np.ones((q, q), np.float32)), BF16),
        e3p=jnp.asarray(np.tile(e_p, (3, 1)), BF16),
        e3q=jnp.asarray(np.tile(e_q, (3, 1)), BF16),
        dmask=jnp.asarray(np.tile(np.eye(q, dtype=np.float32), (nu, SSD_HEADS))),
        causal=jnp.asarray(np.tile(tri, (nu, SSD_HEADS))),
        gmask=jnp.asarray(gmask),
        xmask=jnp.asarray(xmask),
    ), hp, w


def _ssd_kernel(xbc_ref, z_ref, sm_ref, h0_ref, dtb_ref, a_ref, d_ref, nw_ref,
                l3_ref, ones3_ref, e3p_ref, e3q_ref, dmask_ref, causal_ref, gmask_ref, xmask_ref,
                y_ref, hout_ref, ht_ref, *, q, hp, w, nseq, cps):
    c = pl.program_id(1)
    nc = pl.num_programs(1)
    nu = nseq * cps
    rows = nu * q
    gw = SSD_WIDTH // SSD_GROUPS

    @pl.when(c == 0)
    def _():
        for sq in range(nseq):
            ht_ref[sq] = h0_ref[sq].T

    def unit_rows(u):
        return slice(u * q, (u + 1) * q)

    xs = xbc_ref[:, :SSD_WIDTH]
    bcat = xbc_ref[:, SSD_WIDTH:SSD_WIDTH + 2 * SSD_STATE]
    ccat = xbc_ref[:, SSD_WIDTH + 2 * SSD_STATE:].astype(BF16)

    dt = _softplus(sm_ref[...] + dtb_ref[...])
    acs = _dot_exact_rhs(l3_ref[...], dt * a_ref[...])
    dec_ends = [jnp.exp(acs[(u + 1) * q - 1:(u + 1) * q, :] - acs[unit_rows(u)]) for u in range(nu)]
    dec_end = dec_ends[0] if nu == 1 else jnp.concatenate(dec_ends, axis=0)
    per_head = jnp.concatenate([dt, dt * dec_end, jnp.exp(acs)], axis=0)
    exp3 = _dot(_split3_lanes(per_head), e3p_ref[...])
    dt_e, dtd_e, eacs_e = exp3[:rows], exp3[rows:2 * rows], exp3[2 * rows:]

    col_all = _dot(_split3_lanes(acs), e3q_ref[...])
    row_all = _dot_exact_rhs(ones3_ref[...], col_all * dmask_ref[...])
    decay = jnp.exp(jnp.minimum(col_all - row_all, 0.0)) * causal_ref[...]
    xdt = xs * dt_e
    xd = (xs * dtd_e).astype(BF16)
    dxs = d_ref[...] * xs
    gmask = gmask_ref[...]
    xmask = xmask_ref[...]

    def cb_all(u):
        if 2 * q == LANES:
            per_group = []
            for g in range(SSD_GROUPS):
                gl = slice(g * SSD_STATE, (g + 1) * SSD_STATE)
                cb = _dot_nt(ccat[unit_rows(u), gl], bcat[unit_rows(u), gl].astype(BF16))
                per_group += [jnp.concatenate([cb, cb], axis=1)] * (SSD_HEADS // SSD_GROUPS // 2)
            return jnp.concatenate(per_group, axis=1)
        brep = (jnp.concatenate([bcat[unit_rows(u)]] * SSD_HEADS, axis=0) * gmask).astype(BF16)
        return _dot_nt(ccat[unit_rows(u)], brep)

    cbs = [cb_all(u) for u in range(nu)]
    m_alls = [(cbs[u] * decay[unit_rows(u)]).astype(BF16) for u in range(nu)]
    y_diags, upds = [], []
    for u in range(nu):
        blocks = []
        for j in range(SSD_WIDTH // w):
            xj = (jnp.concatenate([xdt[unit_rows(u), j * w:(j + 1) * w]] * hp, axis=0) * xmask).astype(BF16)
            blocks.append(_dot(m_alls[u][:, j * LANES:(j + 1) * LANES], xj))
        y_diags.append(blocks[0] if len(blocks) == 1 else jnp.concatenate(blocks, axis=1))
        upds.append(jnp.concatenate(
            [_dot_tn(bcat[unit_rows(u), g * SSD_STATE:(g + 1) * SSD_STATE].astype(BF16),
                     xd[unit_rows(u), g * gw:(g + 1) * gw]) for g in range(SSD_GROUPS)], axis=1))

    nw = nw_ref[...]
    for sq in range(nseq):
        ht = ht_ref[sq]
        for ci in range(cps):
            u = sq * cps + ci
            ht_bf = ht.astype(BF16)
            y_off = jnp.concatenate([_dot(ccat[unit_rows(u), g * SSD_STATE:(g + 1) * SSD_STATE],
                                          ht_bf[:, g * gw:(g + 1) * gw]) for g in range(SSD_GROUPS)], axis=1)
            e_u = eacs_e[unit_rows(u)]
            y = y_diags[u] + y_off * e_u + dxs[unit_rows(u)]
            ht = ht * e_u[q - 1:q, :] + upds[u]
            yz = y * _silu(z_ref[unit_rows(u)])
            outs = [_rms(yz[:, g * gw:(g + 1) * gw], nw[:, g * gw:(g + 1) * gw]) for g in range(SSD_GROUPS)]
            y_ref[unit_rows(u)] = jnp.concatenate(outs, axis=1).astype(y_ref.dtype)
        ht_ref[sq] = ht

    @pl.when(c == nc - 1)
    def _():
        for sq in range(nseq):
            hout_ref[sq] = ht_ref[sq].T


def _scan_units(b, l, q):
    nchunks = l // q
    cps = SCAN_UNITS_PER_STEP if nchunks % SCAN_UNITS_PER_STEP == 0 else 1
    nseq = SCAN_UNITS_PER_STEP if nchunks == 1 and b % SCAN_UNITS_PER_STEP == 0 else 1
    return nseq, cps


def _ssd_call(xbc, z, small, h0, dtb, a_pad, d_e, nw, b, l, q):
    nseq, cps = _scan_units(b, l, q)
    consts, hp, w = _ssd_constants(q, nseq * cps)
    nc = l // (q * cps)
    t = b * l
    rows = nseq * cps * q
    row = lambda i, c: (i * nc + c, 0)
    seq = lambda i, c: (i, 0, 0)
    params = [dtb, a_pad, d_e, nw] + [consts[k] for k in
                                      ("l3", "ones3", "e3p", "e3q", "dmask", "causal", "gmask", "xmask")]
    hp_shape = (SSD_WIDTH, SSD_STATE)
    return pl.pallas_call(
        functools.partial(_ssd_kernel, q=q, hp=hp, w=w, nseq=nseq, cps=cps),
        grid=(b // nseq, nc),
        in_specs=[pl.BlockSpec((rows, SSD_CONV_DIM), row),
                  pl.BlockSpec((rows, SSD_WIDTH), row),
                  pl.BlockSpec((rows, LANES), row),
                  pl.BlockSpec((nseq,) + hp_shape, seq)]
                 + [_const_spec(p.shape) for p in params],
        out_specs=[pl.BlockSpec((rows, SSD_WIDTH), row),
                   pl.BlockSpec((nseq,) + hp_shape, seq)],
        out_shape=[jax.ShapeDtypeStruct((t, SSD_WIDTH), F32),
                   jax.ShapeDtypeStruct((b,) + hp_shape, F32)],
        scratch_shapes=[pltpu.VMEM((nseq, SSD_STATE, SSD_WIDTH), F32)],
        compiler_params=_params2(),
        name="ssd_scan",
    )(xbc, z, small, h0, *params)


def _gdn_constants(q, nseq, hpb, cps):
    assert nseq * hpb * q == MXU_DIM
    nhb = GDN_HEADS // hpb
    rows = nseq * q
    e_beta = np.zeros((LANES, GDN_WIDTH), np.float32)
    e_alpha = np.zeros((LANES, GDN_WIDTH), np.float32)
    e_c = np.zeros((LANES, nhb * MXU_DIM), np.float32)
    for h in range(GDN_HEADS):
        e_beta[BETA_LANE0 + h, h * GDN_DIM:(h + 1) * GDN_DIM] = 1.0
        e_alpha[ALPHA_LANE0 + h, h * GDN_DIM:(h + 1) * GDN_DIM] = 1.0
    jh, seq_c, hh_c, s_c = np.unravel_index(np.arange(nhb * MXU_DIM), (nhb, nseq, hpb, q))
    e_c[ALPHA_LANE0 + jh * hpb + hh_c, np.arange(nhb * MXU_DIM)] = 1.0
    seq_r, l_r = np.unravel_index(np.arange(rows), (nseq, q))
    same_seq = seq_r[:, None] == seq_c[None, :]
    dmask = same_seq & (l_r[:, None] == s_c[None, :])
    incl = same_seq & (l_r[:, None] >= s_c[None, :])
    strict = same_seq & (l_r[:, None] > s_c[None, :])
    seq_s, hh_s, _ = np.unravel_index(np.arange(MXU_DIM), (nseq, hpb, q))
    bdmask = (seq_s[:, None] == seq_c[None, :MXU_DIM]) & (hh_s[:, None] == hh_c[None, :MXU_DIM])
    kmask = hh_s[:, None] == (np.arange(hpb * GDN_DIM) // GDN_DIM)[None, :]
    f = lambda m: jnp.asarray(m.astype(np.float32))
    per_chunk = lambda m: jnp.asarray(np.tile(m.astype(np.float32), (cps, 1)))
    chunk_eye = np.eye(cps)
    mask_dtype = BF16 if q % (2 * SUBLANES) == 0 else F32
    return dict(
        l3=jnp.asarray(np.kron(np.eye(cps * nseq), np.tril(np.ones((q, q)))), BF16),
        ones3=jnp.asarray(np.kron(chunk_eye, np.ones((rows, rows))), BF16),
        e3b=jnp.asarray(np.tile(e_beta, (3, 1)), BF16),
        e3a=jnp.asarray(np.tile(e_alpha, (3, 1)), BF16),
        e3c=jnp.asarray(np.tile(e_c, (3, 1)), BF16),
        dmask=per_chunk(dmask), incl=per_chunk(incl), strict=f(strict), eye=f(dmask[:, :MXU_DIM]),
        bdmask=f(bdmask).astype(mask_dtype), kmask=f(kmask).astype(mask_dtype),
    )


def _gdn_kernel(qkv_ref, gate_ref, sm_ref, s0_ref, dtb_ref, a_ref, nw_ref,
                l3_ref, ones3_ref, e3b_ref, e3a_ref, e3c_ref, dmask_ref, incl_ref, strict_ref, eye_ref,
                bdmask_ref, kmask_ref,
                o_ref, sout_ref, s_ref, *, q, nseq, hpb, cps):
    c = pl.program_id(1)
    nc = pl.num_programs(1)
    nhb = GDN_HEADS // hpb
    rows = nseq * q
    bw = hpb * GDN_DIM
    n_sq = q.bit_length() - 1
    assert q == 1 << n_sq and n_sq >= 2

    @pl.when(c == 0)
    def _():
        s_ref[...] = s0_ref[...]

    bdmask = bdmask_ref[...]
    kmask = kmask_ref[...]
    mask_dtype = bdmask.dtype

    def head_lanes(h):
        return slice(h * GDN_DIM, (h + 1) * GDN_DIM)

    def seq_rows(sq):
        return slice(sq * q, (sq + 1) * q)

    def chunk_rows(ci):
        return slice(ci * rows, (ci + 1) * rows)

    def stack_rows(x):
        return jnp.concatenate([x[seq_rows(sq)] for sq in range(nseq) for _ in range(hpb)], axis=0)

    def masked_bf16(x, mask):
        return (x.astype(mask_dtype) * mask).astype(BF16)

    def block_diag(m_bf16):
        return masked_bf16(stack_rows(m_bf16.astype(mask_dtype)), bdmask)

    def block_diag_pieces(m):
        hi = m.astype(BF16)
        lo = (m - hi.astype(F32)).astype(BF16)
        return block_diag(hi), block_diag(lo)

    def lhs_pieces(m):
        hi = m.astype(BF16)
        return hi, (m - hi.astype(F32)).astype(BF16)

    sm = sm_ref[...]
    beta = jax.nn.sigmoid(sm)
    g = a_ref[...] * _softplus(sm + dtb_ref[...])
    gc = _dot_exact_rhs(l3_ref[...], g)
    gc3 = _split3_lanes(gc)
    beta_e = _dot(_split3_lanes(beta), e3b_ref[...])
    gc_e = _dot(gc3, e3a_ref[...])
    col_c = _dot(gc3, e3c_ref[...])
    row_c = _dot_exact_rhs(ones3_ref[...], col_c * dmask_ref[...])
    decay_c = jnp.exp(jnp.minimum(col_c - row_c, 0.0)) * incl_ref[...]
    egc_e = jnp.exp(gc_e)
    qn = qkv_ref[:, :GDN_WIDTH]
    kn = qkv_ref[:, GDN_WIDTH:2 * GDN_WIDTH]
    kb = kn * beta_e
    vb = qkv_ref[:, 2 * GDN_WIDTH:] * beta_e
    kbg = kb * egc_e
    qin = qn * egc_e

    def chunk_local(ci):
        r0 = ci * rows
        lasts = [gc_e[r0 + (sq + 1) * q - 1:r0 + (sq + 1) * q, :] for sq in range(nseq)]
        eouts = [jnp.exp(lasts[sq] - gc_e[r0 + sq * q:r0 + (sq + 1) * q]) for sq in range(nseq)]
        cr = chunk_rows(ci)
        return dict(
            qn=qn[cr], kn=kn[cr], kb=kb[cr], vb=vb[cr], kbg=kbg[cr], qin=qin[cr], decay_c=decay_c[cr],
            kout=(kn[cr] * (eouts[0] if nseq == 1 else jnp.concatenate(eouts, axis=0))).astype(BF16),
            ge=[jnp.exp(last) for last in lasts])

    def chain_init(loc, jb):
        nl = slice(jb * bw, (jb + 1) * bw)
        cl = slice(jb * MXU_DIM, (jb + 1) * MXU_DIM)
        k_bd = masked_bf16(stack_rows(loc["kn"][:, nl].astype(mask_dtype)), kmask)
        prod = _dot_nt(jnp.concatenate([loc["kb"][:, nl], loc["qn"][:, nl]], axis=0).astype(BF16), k_bd)
        dec = loc["decay_c"][:, cl]
        x = -(prod[:rows] * dec * strict_ref[:, cl])
        return dict(inv=eye_ref[...] + x, ypow=x, attn=(prod[rows:] * dec).astype(BF16))

    def double_step(st, k):
        first, last = k == 0, k == n_sq - 1
        inv, ypow = st["inv"], st["ypow"]
        b_hi, b_lo = block_diag_pieces(ypow)
        lhs = ypow if first else (inv if last else jnp.concatenate([inv, ypow], axis=0))
        l_hi, l_lo = lhs_pieces(lhs)
        res = _dot(jnp.concatenate([l_hi, l_hi, l_lo], axis=1), jnp.concatenate([b_hi, b_lo, b_hi], axis=0))
        if first:
            return dict(st, ypow=res)
        if last:
            return dict(st, inv=inv + res)
        return dict(st, inv=inv + res[:rows], ypow=res[rows:])

    def solve(loc, jb, inv):
        i_hi, i_lo = block_diag_pieces(inv)
        segs = [(sq, jb * hpb + hh) for sq in range(nseq) for hh in range(hpb)]
        rhs = jnp.concatenate([jnp.concatenate([loc["vb"][seq_rows(sq), head_lanes(h)],
                                                loc["kbg"][seq_rows(sq), head_lanes(h)]], axis=1)
                               for sq, h in segs], axis=0).astype(BF16)
        return _dot(jnp.concatenate([i_hi, i_lo], axis=1), jnp.concatenate([rhs, rhs], axis=0))

    def state_step(loc, jb, uw, attn):
        segs = [(sq, jb * hpb + hh) for sq in range(nseq) for hh in range(hpb)]
        ws, qs, states = [], [], []
        for i, (sq, h) in enumerate(segs):
            s = s_ref[sq, h]
            states.append(s)
            wq = jnp.concatenate([uw[i * q:(i + 1) * q, GDN_DIM:], loc["qin"][seq_rows(sq), head_lanes(h)]], axis=0)
            r = _dot(wq.astype(BF16), s.astype(BF16))
            ws.append(r[:q])
            qs.append(r[q:])
        v_bf = (uw[:, :GDN_DIM] - jnp.concatenate(ws, axis=0)).astype(BF16)
        v_rep = masked_bf16(jnp.concatenate([v_bf.astype(mask_dtype)] * hpb, axis=1), kmask)
        qs_rows = [jnp.concatenate(qs[sq * hpb:(sq + 1) * hpb], axis=1) for sq in range(nseq)]
        qs_nat = qs_rows[0] if nseq == 1 else jnp.concatenate(qs_rows, axis=0)
        for i, (sq, h) in enumerate(segs):
            s_ref[sq, h] = (states[i] * loc["ge"][sq][:, head_lanes(h)]
                            + _dot_tn(loc["kout"][seq_rows(sq), head_lanes(h)], v_bf[i * q:(i + 1) * q]))
        return qs_nat + _dot(attn, v_rep)

    local = [chunk_local(ci) for ci in range(cps)]
    chains = [(ci, jb) for ci in range(cps) for jb in range(nhb)]
    st = [chain_init(local[ci], jb) for ci, jb in chains]
    for k in range(n_sq):
        st = [double_step(s, k) for s in st]
    uws =[solve(local[ci], jb, s["inv"]) for (ci, jb), s in zip(chains, st)]
    nw = nw_ref[...]
    for ci in range(cps):
        o_blocks = [state_step(local[ci], jb, uws[ci * nhb + jb], st[ci * nhb + jb]["attn"]) for jb in range(nhb)]
        o = o_blocks[0] if nhb == 1 else jnp.concatenate(o_blocks, axis=1)
        gate = gate_ref[chunk_rows(ci)]
        outs = [_rms(o[:, head_lanes(h)], nw) * _silu(gate[:, head_lanes(h)]) for h in range(GDN_HEADS)]
        o_ref[chunk_rows(ci)] = jnp.concatenate(outs, axis=1).astype(o_ref.dtype)

    @pl.when(c == nc - 1)
    def _():
        sout_ref[...] = s_ref[...]


def _gdn_call(qkv, gate, small, s0, dtb, a_pad, nw, b, l, q):
    nseq = max(1, MXU_DIM // (GDN_HEADS * q))
    hpb = MXU_DIM // (nseq * q)
    cps = SCAN_UNITS_PER_STEP if nseq == 1 and (l // q) % SCAN_UNITS_PER_STEP == 0 else 1
    consts = _gdn_constants(q, nseq, hpb, cps)
    nc = l // (q * cps)
    assert b % nseq == 0 and (nseq == 1 or nc == 1)
    t = b * l
    rows = nseq * q * cps
    row = lambda i, c: (i * nc + c, 0)
    seq = lambda i, c: (i, 0, 0, 0)
    params = [dtb, a_pad, nw] + [consts[k] for k in
                                 ("l3", "ones3", "e3b", "e3a", "e3c", "dmask", "incl", "strict", "eye",
                                  "bdmask", "kmask")]
    s_shape = (GDN_HEADS, GDN_DIM, GDN_DIM)
    return pl.pallas_call(
        functools.partial(_gdn_kernel, q=q, nseq=nseq, hpb=hpb, cps=cps),
        grid=(b // nseq, nc),
        in_specs=[pl.BlockSpec((rows, GDN_CONV_DIM), row),
                  pl.BlockSpec((rows, GDN_WIDTH), row),
                  pl.BlockSpec((rows, LANES), row),
                  pl.BlockSpec((nseq,) + s_shape, seq)]
                 + [_const_spec(p.shape) for p in params],
        out_specs=[pl.BlockSpec((rows, GDN_WIDTH), row),
                   pl.BlockSpec((nseq,) + s_shape, seq)],
        out_shape=[jax.ShapeDtypeStruct((t, GDN_WIDTH), F32),
                   jax.ShapeDtypeStruct((b,) + s_shape, F32)],
        scratch_shapes=[pltpu.VMEM((nseq,) + s_shape, F32)],
        compiler_params=_params2(),
        name="gdn_scan",
    )(qkv, gate, small, s0, *params)


def _outffn_kernel(x_ref, y_ref, o_ref, mod_ref, nwm_ref, nwf_ref, nwp_ref,
                   wout_ref, wgu_ref, wdown_ref, out_ref, *, ff_chunk):
    x = x_ref[...]
    shape3 = x.shape
    tm = shape3[0] * shape3[1]
    mod = mod_ref[...]
    g1, sh2, sc2, g2 = (mod[:, k:k + 1, :] for k in range(2, 6))
    m = (_dot(y_ref[...].astype(BF16), wout_ref[:SSD_WIDTH, :])
         + _dot(o_ref[...].astype(BF16), wout_ref[SSD_WIDTH:, :]))
    x1 = x + g1 * _rms(m, nwm_ref[...]).reshape(shape3)
    h2 = _rms(x1, nwf_ref[...]) * (1.0 + sc2) + sh2
    h2 = h2.reshape(tm, shape3[2]).astype(BF16)
    f = jnp.zeros((tm, shape3[2]), F32)
    for j in range(D_FF // ff_chunk):
        gj = _dot(h2, wgu_ref[:, j * ff_chunk:(j + 1) * ff_chunk])
        uj = _dot(h2, wgu_ref[:, D_FF + j * ff_chunk:D_FF + (j + 1) * ff_chunk])
        f = f + _dot((_silu(gj) * uj).astype(BF16), wdown_ref[j * ff_chunk:(j + 1) * ff_chunk, :])
    out_ref[...] = x1 + g2 * _rms(f, nwp_ref[...]).reshape(shape3)


def _outffn_call(x, y, o, mod6, nw_mix_post, nw_ffn_pre, nw_ffn_post, w_out, w_gu, w_down, bb, lb):
    b, l, d = x.shape
    nl = l // lb
    tm = bb * lb
    row = lambda i, j: (i * nl + j, 0)
    return pl.pallas_call(
        functools.partial(_outffn_kernel, ff_chunk=D_FF // 2),
        grid=(b // bb, nl),
        in_specs=[pl.BlockSpec((bb, lb, d), lambda i, j: (i, j, 0)),
                  pl.BlockSpec((tm, SSD_WIDTH), row),
                  pl.BlockSpec((tm, GDN_WIDTH), row),
                  pl.BlockSpec((bb,) + mod6.shape[1:], lambda i, j: (i, 0, 0)),
                  _const_spec((1, d)), _const_spec((1, d)), _const_spec((1, d)),
                  _const_spec(w_out.shape), _const_spec(w_gu.shape), _const_spec(w_down.shape)],
        out_specs=pl.BlockSpec((bb, lb, d), lambda i, j: (i, j, 0)),
        out_shape=jax.ShapeDtypeStruct((b, l, d), F32),
        compiler_params=_params2(),
        name="outproj_ffn",
    )(x, y, o, mod6, nw_mix_post, nw_ffn_pre, nw_ffn_post, w_out, w_gu, w_down)


_O_DT = SSD_WIDTH + SSD_CONV_DIM
_O_QKV = _O_DT + SSD_HEADS
_O_BETA = _O_QKV + GDN_CONV_DIM + GDN_WIDTH
_IN_DIM = _O_BETA + 2 * GDN_HEADS


def _repack_kernel(wt_ref, big_ref):
    big_ref[...] = wt_ref[...].T.astype(BF16)


def _repack_w_in(w):
    k, n = w.shape
    assert n == _IN_DIM and _O_DT % MXU_DIM == 0
    wt = w.T
    big = n - SSD_HEADS - 2 * GDN_HEADS
    skip = _O_QKV - _O_DT
    src_row = lambda j: (pl.multiple_of(jnp.where(j * MXU_DIM < _O_DT, j * MXU_DIM, j * MXU_DIM + skip), skip), 0)
    w_big = pl.pallas_call(
        _repack_kernel,
        grid=(big // MXU_DIM,),
        in_specs=[pl.BlockSpec((pl.Element(MXU_DIM), pl.Element(k)), src_row)],
        out_specs=pl.BlockSpec((k, MXU_DIM), lambda j: (0, j)),
        out_shape=jax.ShapeDtypeStruct((k, big), BF16),
        compiler_params=_params2(("arbitrary",)),
        name="repack_w_in",
    )(wt)
    narrow = jnp.concatenate([wt[_O_DT:_O_QKV], wt[_O_BETA:]], axis=0).T
    w_small = jnp.pad(narrow, ((0, 0), (0, LANES - narrow.shape[1]))).astype(BF16)
    return w_big, w_small


def _pad_lanes(v, lane0):
    out = jnp.zeros((1, LANES), F32)
    return out.at[0, lane0:lane0 + v.shape[0]].set(v.astype(F32))


def _layer(x, mod, ssd_h0, ssd_conv0, gdn_s0, gdn_conv0, p, q, bb, lb):
    b, l, d = x.shape
    mod6 = mod.reshape(b, 6, d)
    z, xbc, qkv, gate, small, ssd_conv, gdn_conv = _inproj_call(
        x, mod6, p["norm_mix_pre"], p["w_big"], p["w_small"], ssd_conv0, gdn_conv0,
        p["ssd_conv_w"], p["ssd_conv_b"], p["gdn_conv_w"], bb, lb)
    y, ssd_h = _ssd_call(xbc, z, small, ssd_h0.reshape(b, SSD_WIDTH, SSD_STATE),
                         p["ssd_dtb"], p["ssd_a"], p["ssd_d"], p["ssd_norm_w"], b, l, q)
    o, gdn_s = _gdn_call(qkv, gate, small, gdn_s0, p["gdn_dtb"], p["gdn_a"], p["gdn_norm_w"], b, l, q)
    out = _outffn_call(x, y, o, mod6, p["norm_mix_post"], p["norm_ffn_pre"], p["norm_ffn_post"],
                       p["w_out"], p["w_gu"], p["w_down"], bb, lb)
    return out, ssd_h.reshape(ssd_h0.shape), ssd_conv, gdn_s, gdn_conv


def _layer_params(l, w_ada, b_ada, norm_mix_pre, norm_mix_post, norm_ffn_pre, norm_ffn_post, w_in, ssd_conv_w,
                  ssd_conv_b, ssd_dt_bias, ssd_A_log, ssd_D, ssd_norm_w, gdn_conv_w, gdn_dt_bias, gdn_A_log,
                  gdn_norm_w, w_out, w_gate_up, w_down):
    w_big, w_small = _repack_w_in(w_in[l])
    row = lambda v: v.reshape(1, -1).astype(F32)
    return dict(
        w_ada=w_ada[l], b_ada=b_ada[l],
        norm_mix_pre=row(norm_mix_pre[l]), norm_mix_post=row(norm_mix_post[l]),
        norm_ffn_pre=row(norm_ffn_pre[l]), norm_ffn_post=row(norm_ffn_post[l]),
        w_big=w_big, w_small=w_small,
        ssd_conv_w=ssd_conv_w[l], ssd_conv_b=row(ssd_conv_b[l]),
        ssd_dtb=_pad_lanes(ssd_dt_bias[l], DT_LANE0),
        ssd_a=_pad_lanes(-jnp.exp(ssd_A_log[l].astype(F32)), DT_LANE0),
        ssd_d=row(jnp.repeat(ssd_D[l], SSD_HEADDIM)),
        ssd_norm_w=row(ssd_norm_w[l]),
        gdn_conv_w=gdn_conv_w[l],
        gdn_dtb=_pad_lanes(gdn_dt_bias[l], ALPHA_LANE0),
        gdn_a=_pad_lanes(-jnp.exp(gdn_A_log[l].astype(F32)), ALPHA_LANE0),
        gdn_norm_w=row(gdn_norm_w[l]),
        w_out=w_out[l].astype(BF16), w_gu=w_gate_up[l].astype(BF16), w_down=w_down[l].astype(BF16),
    )


def kernel(x_prompt, x_sample, c_prompt, c_sample, state_ssd, state_ssd_conv, state_gdn, state_gdn_conv, w_ada, b_ada, norm_mix_pre, norm_mix_post, norm_ffn_pre, norm_ffn_post, w_in, ssd_conv_w, ssd_conv_b, ssd_dt_bias, ssd_A_log, ssd_D, ssd_norm_w, gdn_conv_w, gdn_dt_bias, gdn_A_log, gdn_norm_w, w_out, w_gate_up, w_down):
    depth = w_in.shape[0]
    bp, lp, _ = x_prompt.shape
    bs, ls, _ = x_sample.shape
    yp, ys = x_prompt, x_sample
    outs = [[] for _ in range(8)]
    for l in range(depth):
        p = _layer_params(l, w_ada, b_ada, norm_mix_pre, norm_mix_post, norm_ffn_pre, norm_ffn_post, w_in,
                          ssd_conv_w, ssd_conv_b, ssd_dt_bias, ssd_A_log, ssd_D, ssd_norm_w, gdn_conv_w,
                          gdn_dt_bias, gdn_A_log, gdn_norm_w, w_out, w_gate_up, w_down)
        mod = _mod_call(jnp.concatenate([c_prompt, c_sample], axis=0), p["w_ada"], p["b_ada"])
        zeros = lambda a: jnp.zeros((bp,) + a.shape[2:], a.dtype)
        yp, a0, a1, a2, a3 = _layer(yp, mod[:bp], zeros(state_ssd), zeros(state_ssd_conv), zeros(state_gdn),
                                    zeros(state_gdn_conv), p, q=min(CHUNK, lp), bb=1, lb=min(ROW_TILE, lp))
        ys, b0, b1, b2, b3 = _layer(ys, mod[bp:], state_ssd[l], state_ssd_conv[l], state_gdn[l],
                                    state_gdn_conv[l], p, q=min(CHUNK, ls), bb=min(ROW_TILE // ls, bs), lb=ls)
        for lst, v in zip(outs, (a0, a1, a2, a3, b0, b1, b2, b3)):
            lst.append(v)
    return (yp, ys) + tuple(v[0][None] if depth == 1 else jnp.stack(v) for v in outs)
```

```python
import functools

import numpy as np
import jax
import jax.numpy as jnp
from jax import lax
from jax.experimental import pallas as pl
from jax.experimental.pallas import tpu as pltpu

F32 = jnp.float32
BF16 = jnp.bfloat16

D_MODEL = 1024
SSD_HEADS = 16
SSD_HEADDIM = 64
SSD_GROUPS = 2
SSD_STATE = 128
SSD_WIDTH = SSD_HEADS * SSD_HEADDIM
SSD_CONV_DIM = SSD_WIDTH + 2 * SSD_GROUPS * SSD_STATE
GDN_HEADS = 8
GDN_DIM = 128
GDN_WIDTH = GDN_HEADS * GDN_DIM
GDN_CONV_DIM = 3 * GDN_WIDTH
CONV_WIDTH = 4
CHUNK = 64
D_FF = 2816
NORM_EPS = 1e-6
L2_EPS = 1e-6
LANES = 128
SUBLANES = 8
MXU_DIM = 256
DT_LANE0, BETA_LANE0, ALPHA_LANE0 = 0, SSD_HEADS, SSD_HEADS + GDN_HEADS
VMEM_LIMIT = 56 * 1024 * 1024
ROW_TILE = 256
SCAN_UNITS_PER_STEP = 4
CONV_COLS = 512


def _dot(a, b):
    return jnp.dot(a, b, preferred_element_type=F32)


def _dot_nt(a, b):
    return lax.dot_general(a, b, (((1,), (1,)), ((), ())), preferred_element_type=F32)


def _dot_tn(a, b):
    return lax.dot_general(a, b, (((0,), (0,)), ((), ())), preferred_element_type=F32)


def _bf16_pieces(x):
    hi = x.astype(BF16)
    r1 = x - hi.astype(F32)
    mid = r1.astype(BF16)
    lo = (r1 - mid.astype(F32)).astype(BF16)
    return hi, mid, lo


def _split3_lanes(x):
    hi, mid, lo = _bf16_pieces(x)
    return jnp.concatenate([hi.astype(F32), mid.astype(F32), lo.astype(F32)], axis=1).astype(BF16)


def _dot_exact_rhs(a_bf16, x):
    hi, mid, lo = _bf16_pieces(x)
    return _dot(a_bf16, hi) + _dot(a_bf16, mid) + _dot(a_bf16, lo)


def _silu(x):
    return x * jax.nn.sigmoid(x)


def _softplus(x):
    return jnp.maximum(x, 0.0) + jnp.log1p(jnp.exp(-jnp.abs(x)))


def _rms(x, w):
    return x * lax.rsqrt(jnp.mean(x * x, axis=-1, keepdims=True) + NORM_EPS) * w


def _const_spec(shape):
    nd = len(shape)
    return pl.BlockSpec(shape, lambda *_: (0,) * nd, pipeline_mode=pl.Buffered(1))


def _params2(sem=("arbitrary", "arbitrary")):
    return pltpu.CompilerParams(dimension_semantics=sem, vmem_limit_bytes=VMEM_LIMIT)


def _mod_kernel(c_ref, w_ref, b_ref, o_ref):
    a = _silu(c_ref[...]).astype(BF16)
    o_ref[...] = _dot(a, w_ref[...].astype(BF16)) + b_ref[...]


def _mod_call(c_all, w_ada, b_ada):
    m, d = c_all.shape
    n = w_ada.shape[1]
    tn = 512
    return pl.pallas_call(
        _mod_kernel,
        grid=(n // tn,),
        in_specs=[pl.BlockSpec((m, d), lambda j: (0, 0)),
                  pl.BlockSpec((d, tn), lambda j: (0, j)),
                  pl.BlockSpec((1, tn), lambda j: (0, j))],
        out_specs=pl.BlockSpec((m, tn), lambda j: (0, j)),
        out_shape=jax.ShapeDtypeStruct((m, n), F32),
        compiler_params=_params2(("arbitrary",)),
        name="adaln_mod",
    )(c_all, w_ada, b_ada.reshape(1, n))


def _proj_conv_silu(h, w_ref, w_col0, width, bb, l, tail_ref, convout_ref, cw_ref, cb_ref, out_ref, n_norm, norm_scale):
    tm = bb * l
    for k in range(width // CONV_COLS):
        cols = slice(k * CONV_COLS, (k + 1) * CONV_COLS)
        raw = _dot(h, w_ref[:, w_col0 + k * CONV_COLS:w_col0 + (k + 1) * CONV_COLS]).reshape(bb, l, CONV_COLS)
        ext = jnp.concatenate([tail_ref[:, :, cols], raw], axis=1)
        acc = ext[:, SUBLANES:SUBLANES + l] * cw_ref[CONV_WIDTH - 1:CONV_WIDTH, cols]
        for j in range(1, CONV_WIDTH):
            acc = acc + ext[:, SUBLANES - j:SUBLANES - j + l] * cw_ref[CONV_WIDTH - 1 - j:CONV_WIDTH - j, cols]
        tail_ref[:, :, cols] = ext[:, l:l + SUBLANES]
        convout_ref[:, :, cols] = ext[:, l + SUBLANES - (CONV_WIDTH - 1):l + SUBLANES]
        if cb_ref is not None:
            acc = acc + cb_ref[:, cols]
        act = _silu(acc).reshape(tm, CONV_COLS)
        for hh in range(CONV_COLS // LANES):
            head = k * (CONV_COLS // LANES) + hh
            xh = act[:, hh * LANES:(hh + 1) * LANES]
            if head < n_norm:
                xh = xh * (lax.rsqrt(jnp.sum(xh * xh, axis=-1, keepdims=True) + L2_EPS) * norm_scale[head])
            out_ref[:, head * LANES:(head + 1) * LANES] = xh


def _inproj_kernel(x_ref, mod_ref, nw_ref, wbig_ref, wsm_ref, sconv0_ref, gconv0_ref,
                   scw_ref, scb_ref, gcw_ref,
                   z_ref, xbc_ref, qkv_ref, gate_ref, sm_ref, sconv_ref, gconv_ref, stail_ref, gtail_ref):
    @pl.when(pl.program_id(1) == 0)
    def _():
        for tail_ref, conv0_ref in ((stail_ref, sconv0_ref), (gtail_ref, gconv0_ref)):
            tail_ref[...] = jnp.zeros(tail_ref.shape, F32)
            tail_ref[:, SUBLANES - (CONV_WIDTH - 1):, :] = conv0_ref[...]

    x = x_ref[...]
    bb, l, d = x.shape
    tm = bb * l
    mod = mod_ref[...]
    h = _rms(x, nw_ref[...]) * (1.0 + mod[:, 1:2, :]) + mod[:, 0:1, :]
    h = h.reshape(tm, d).astype(BF16)
    c_xbc = SSD_WIDTH
    c_qkv = c_xbc + SSD_CONV_DIM
    c_gate = c_qkv + GDN_CONV_DIM
    z_ref[...] = _dot(h, wbig_ref[:, :c_xbc])
    gate_ref[...] = _dot(h, wbig_ref[:, c_gate:])
    sm_ref[...] = _dot(h, wsm_ref[...])
    _proj_conv_silu(h, wbig_ref, c_xbc, SSD_CONV_DIM, bb, l, stail_ref, sconv_ref, scw_ref, scb_ref, xbc_ref, 0, ())
    qk_scale = (GDN_DIM ** -0.5,) * GDN_HEADS + (1.0,) * GDN_HEADS
    _proj_conv_silu(h, wbig_ref, c_qkv, GDN_CONV_DIM, bb, l, gtail_ref, gconv_ref, gcw_ref, None, qkv_ref,
                    2 * GDN_HEADS, qk_scale)


def _inproj_call(x, mod6, norm_w, w_big, w_small, sconv0, gconv0, scw, scb, gcw, bb, lb):
    b, l, d = x.shape
    nl = l // lb
    tm = bb * lb
    t = b * l
    row = lambda i, j: (i * nl + j, 0)
    seq = lambda i, j: (i, 0, 0)
    widths = (SSD_WIDTH, SSD_CONV_DIM, GDN_CONV_DIM, GDN_WIDTH, LANES)
    hist = CONV_WIDTH - 1
    return pl.pallas_call(
        _inproj_kernel,
        grid=(b // bb, nl),
        in_specs=[pl.BlockSpec((bb, lb, d), lambda i, j: (i, j, 0)),
                  pl.BlockSpec((bb,) + mod6.shape[1:], lambda i, j: (i, 0, 0)),
                  _const_spec((1, d)),
                  _const_spec(w_big.shape),
                  _const_spec(w_small.shape),
                  pl.BlockSpec((bb, hist, SSD_CONV_DIM), seq),
                  pl.BlockSpec((bb, hist, GDN_CONV_DIM), seq),
                  _const_spec(scw.shape), _const_spec(scb.shape), _const_spec(gcw.shape)],
        out_specs=[pl.BlockSpec((tm, w), row) for w in widths]
                  + [pl.BlockSpec((bb, hist, SSD_CONV_DIM), seq), pl.BlockSpec((bb, hist, GDN_CONV_DIM), seq)],
        out_shape=[jax.ShapeDtypeStruct((t, w), F32) for w in widths]
                  + [jax.ShapeDtypeStruct((b, hist, SSD_CONV_DIM), F32),
                     jax.ShapeDtypeStruct((b, hist, GDN_CONV_DIM), F32)],
        scratch_shapes=[pltpu.VMEM((bb, SUBLANES, SSD_CONV_DIM), F32),
                        pltpu.VMEM((bb, SUBLANES, GDN_CONV_DIM), F32)],
        compiler_params=_params2(),
        name="inproj",
    )(x, mod6, norm_w, w_big, w_small, sconv0, gconv0, scw, scb, gcw)


def _ssd_constants(q, nu):
    hq = SSD_HEADS * q
    hp = LANES // q
    w = hp * SSD_HEADDIM
    e_p = np.zeros((LANES, SSD_WIDTH), np.float32)
    e_q = np.zeros((LANES, hq), np.float32)
    for r in range(SSD_HEADS):
        e_p[DT_LANE0 + r, r * SSD_HEADDIM:(r + 1) * SSD_HEADDIM] = 1.0
        e_q[DT_LANE0 + r, r * q:(r + 1) * q] = 1.0
    tri = np.tril(np.ones((q, q), np.float32))
    gmask = np.zeros((hq, 2 * SSD_STATE), np.float32)
    half = hq // SSD_GROUPS
    gmask[:half, :SSD_STATE] = 1.0
    gmask[half:, SSD_STATE:] = 1.0
    xmask = np.kron(np.eye(hp, dtype=np.float32), np.ones((q, SSD_HEADDIM), np.float32))
    unit_eye = np.eye(nu, dtype=np.float32)
    return dict(
        l3=jnp.asarray(np.kron(unit_eye, tri), BF16),
        ones3=jnp.asarray(np.kron(unit_eye, np.ones((q, q), np.float32)), BF16),
        e3p=jnp.asarray(np.tile(e_p, (3, 1)), BF16),
        e3q=jnp.asarray(np.tile(e_q, (3, 1)), BF16),
        dmask=jnp.asarray(np.tile(np.eye(q, dtype=np.float32), (nu, SSD_HEADS))),
        causal=jnp.asarray(np.tile(tri, (nu, SSD_HEADS))),
        gmask=jnp.asarray(gmask),
        xmask=jnp.asarray(xmask),
    ), hp, w


def _ssd_kernel(xbc_ref, z_ref, sm_ref, h0_ref, dtb_ref, a_ref, d_ref, nw_ref,
                l3_ref, ones3_ref, e3p_ref, e3q_ref, dmask_ref, causal_ref, gmask_ref, xmask_ref,
                y_ref, hout_ref, ht_ref, *, q, hp, w, nseq, cps):
    c = pl.program_id(1)
    nc = pl.num_programs(1)
    nu = nseq * cps
    rows = nu * q
    gw = SSD_WIDTH // SSD_GROUPS

    @pl.when(c == 0)
    def _():
        for sq in range(nseq):
            ht_ref[sq] = h0_ref[sq].T

    def unit_rows(u):
        return slice(u * q, (u + 1) * q)

    xs = xbc_ref[:, :SSD_WIDTH]
    bcat = xbc_ref[:, SSD_WIDTH:SSD_WIDTH + 2 * SSD_STATE]
    ccat = xbc_ref[:, SSD_WIDTH + 2 * SSD_STATE:].astype(BF16)

    dt = _softplus(sm_ref[...] + dtb_ref[...])
    acs = _dot_exact_rhs(l3_ref[...], dt * a_ref[...])
    dec_ends = [jnp.exp(acs[(u + 1) * q - 1:(u + 1) * q, :] - acs[unit_rows(u)]) for u in range(nu)]
    dec_end = dec_ends[0] if nu == 1 else jnp.concatenate(dec_ends, axis=0)
    per_head = jnp.concatenate([dt, dt * dec_end, jnp.exp(acs)], axis=0)
    exp3 = _dot(_split3_lanes(per_head), e3p_ref[...])
    dt_e, dtd_e, eacs_e = exp3[:rows], exp3[rows:2 * rows], exp3[2 * rows:]

    col_all = _dot(_split3_lanes(acs), e3q_ref[...])
    row_all = _dot_exact_rhs(ones3_ref[...], col_all * dmask_ref[...])
    decay = jnp.exp(jnp.minimum(col_all - row_all, 0.0)) * causal_ref[...]
    xdt = xs * dt_e
    xd = (xs * dtd_e).astype(BF16)
    dxs = d_ref[...] * xs
    gmask = gmask_ref[...]
    xmask = xmask_ref[...]

    def cb_all(u):
        if 2 * q == LANES:
            per_group = []
            for g in range(SSD_GROUPS):
                gl = slice(g * SSD_STATE, (g + 1) * SSD_STATE)
                cb = _dot_nt(ccat[unit_rows(u), gl], bcat[unit_rows(u), gl].astype(BF16))
                per_group += [jnp.concatenate([cb, cb], axis=1)] * (SSD_HEADS // SSD_GROUPS // 2)
            return jnp.concatenate(per_group, axis=1)
        brep = (jnp.concatenate([bcat[unit_rows(u)]] * SSD_HEADS, axis=0) * gmask).astype(BF16)
        return _dot_nt(ccat[unit_rows(u)], brep)

    cbs = [cb_all(u) for u in range(nu)]
    m_alls = [(cbs[u] * decay[unit_rows(u)]).astype(BF16) for u in range(nu)]
    y_diags, upds = [], []
    for u in range(nu):
        blocks = []
        for j in range(SSD_WIDTH // w):
            xj = (jnp.concatenate([xdt[unit_rows(u), j * w:(j + 1) * w]] * hp, axis=0) * xmask).astype(BF16)
            blocks.append(_dot(m_alls[u][:, j * LANES:(j + 1) * LANES], xj))
        y_diags.append(blocks[0] if len(blocks) == 1 else jnp.concatenate(blocks, axis=1))
        upds.append(jnp.concatenate(
            [_dot_tn(bcat[unit_rows(u), g * SSD_STATE:(g + 1) * SSD_STATE].astype(BF16),
                     xd[unit_rows(u), g * gw:(g + 1) * gw]) for g in range(SSD_GROUPS)], axis=1))

    nw = nw_ref[...]
    for sq in range(nseq):
        ht = ht_ref[sq]
        for ci in range(cps):
            u = sq * cps + ci
            ht_bf = ht.astype(BF16)
            y_off = jnp.concatenate([_dot(ccat[unit_rows(u), g * SSD_STATE:(g + 1) * SSD_STATE],
                                          ht_bf[:, g * gw:(g + 1) * gw]) for g in range(SSD_GROUPS)], axis=1)
            e_u = eacs_e[unit_rows(u)]
            y = y_diags[u] + y_off * e_u + dxs[unit_rows(u)]
            ht = ht * e_u[q - 1:q, :] + upds[u]
            yz = y * _silu(z_ref[unit_rows(u)])
            outs = [_rms(yz[:, g * gw:(g + 1) * gw], nw[:, g * gw:(g + 1) * gw]) for g in range(SSD_GROUPS)]
            y_ref[unit_rows(u)] = jnp.concatenate(outs, axis=1).astype(y_ref.dtype)
        ht_ref[sq] = ht

    @pl.when(c == nc - 1)
    def _():
        for sq in range(nseq):
            hout_ref[sq] = ht_ref[sq].T


def _scan_units(b, l, q):
    nchunks = l // q
    cps = SCAN_UNITS_PER_STEP if nchunks % SCAN_UNITS_PER_STEP == 0 else 1
    nseq = SCAN_UNITS_PER_STEP if nchunks == 1 and b % SCAN_UNITS_PER_STEP == 0 else 1
    return nseq, cps


def _ssd_call(xbc, z, small, h0, dtb, a_pad, d_e, nw, b, l, q):
    nseq, cps = _scan_units(b, l, q)
    consts, hp, w = _ssd_constants(q, nseq * cps)
    nc = l // (q * cps)
    t = b * l
    rows = nseq * cps * q
    row = lambda i, c: (i * nc + c, 0)
    seq = lambda i, c: (i, 0, 0)
    params = [dtb, a_pad, d_e, nw] + [consts[k] for k in
                                      ("l3", "ones3", "e3p", "e3q", "dmask", "causal", "gmask", "xmask")]
    hp_shape = (SSD_WIDTH, SSD_STATE)
    return pl.pallas_call(
        functools.partial(_ssd_kernel, q=q, hp=hp, w=w, nseq=nseq, cps=cps),
        grid=(b // nseq, nc),
        in_specs=[pl.BlockSpec((rows, SSD_CONV_DIM), row),
                  pl.BlockSpec((rows, SSD_WIDTH), row),
                  pl.BlockSpec((rows, LANES), row),
                  pl.BlockSpec((nseq,) + hp_shape, seq)]
                 + [_const_spec(p.shape) for p in params],
        out_specs=[pl.BlockSpec((rows, SSD_WIDTH), row),
                   pl.BlockSpec((nseq,) + hp_shape, seq)],
        out_shape=[jax.ShapeDtypeStruct((t, SSD_WIDTH), F32),
                   jax.ShapeDtypeStruct((b,) + hp_shape, F32)],
        scratch_shapes=[pltpu.VMEM((nseq, SSD_STATE, SSD_WIDTH), F32)],
        compiler_params=_params2(),
        name="ssd_scan",
    )(xbc, z, small, h0, *params)


def _gdn_constants(q, nseq, hpb, cps):
    assert nseq * hpb * q == MXU_DIM
    nhb = GDN_HEADS // hpb
    rows = nseq * q
    e_beta = np.zeros((LANES, GDN_WIDTH), np.float32)
    e_alpha = np.zeros((LANES, GDN_WIDTH), np.float32)
    e_c = np.zeros((LANES, nhb * MXU_DIM), np.float32)
    for h in range(GDN_HEADS):
        e_beta[BETA_LANE0 + h, h * GDN_DIM:(h + 1) * GDN_DIM] = 1.0
        e_alpha[ALPHA_LANE0 + h, h * GDN_DIM:(h + 1) * GDN_DIM] = 1.0
    jh, seq_c, hh_c, s_c = np.unravel_index(np.arange(nhb * MXU_DIM), (nhb, nseq, hpb, q))
    e_c[ALPHA_LANE0 + jh * hpb + hh_c, np.arange(nhb * MXU_DIM)] = 1.0
    seq_r, l_r = np.unravel_index(np.arange(rows), (nseq, q))
    same_seq = seq_r[:, None] == seq_c[None, :]
    dmask = same_seq & (l_r[:, None] == s_c[None, :])
    incl = same_seq & (l_r[:, None] >= s_c[None, :])
    strict = same_seq & (l_r[:, None] > s_c[None, :])
    seq_s, hh_s, _ = np.unravel_index(np.arange(MXU_DIM), (nseq, hpb, q))
    bdmask = (seq_s[:, None] == seq_c[None, :MXU_DIM]) & (hh_s[:, None] == hh_c[None, :MXU_DIM])
    kmask = hh_s[:, None] == (np.arange(hpb * GDN_DIM) // GDN_DIM)[None, :]
    f = lambda m: jnp.asarray(m.astype(np.float32))
    per_chunk = lambda m: jnp.asarray(np.tile(m.astype(np.float32), (cps, 1)))
    chunk_eye = np.eye(cps)
    mask_dtype = BF16 if q % (2 * SUBLANES) == 0 else F32
    return dict(
        l3=jnp.asarray(np.kron(np.eye(cps * nseq), np.tril(np.ones((q, q)))), BF16),
        ones3=jnp.asarray(np.kron(chunk_eye, np.ones((rows, rows))), BF16),
        e3b=jnp.asarray(np.tile(e_beta, (3, 1)), BF16),
        e3a=jnp.asarray(np.tile(e_alpha, (3, 1)), BF16),
        e3c=jnp.asarray(np.tile(e_c, (3, 1)), BF16),
        dmask=per_chunk(dmask), incl=per_chunk(incl), strict=f(strict), eye=f(dmask[:, :MXU_DIM]),
        bdmask=f(bdmask).astype(mask_dtype), kmask=f(kmask).astype(mask_dtype),
    )


def _gdn_kernel(qkv_ref, gate_ref, sm_ref, s0_ref, dtb_ref, a_ref, nw_ref,
                l3_ref, ones3_ref, e3b_ref, e3a_ref, e3c_ref, dmask_ref, incl_ref, strict_ref, eye_ref,
                bdmask_ref, kmask_ref,
                o_ref, sout_ref, s_ref, *, q, nseq, hpb, cps):
    c = pl.program_id(1)
    nc = pl.num_programs(1)
    nhb = GDN_HEADS // hpb
    rows = nseq * q
    bw = hpb * GDN_DIM
    n_sq = q.bit_length() - 1
    assert q == 1 << n_sq and n_sq >= 2

    @pl.when(c == 0)
    def _():
        s_ref[...] = s0_ref[...]

    bdmask = bdmask_ref[...]
    kmask = kmask_ref[...]
    mask_dtype = bdmask.dtype

    def head_lanes(h):
        return slice(h * GDN_DIM, (h + 1) * GDN_DIM)

    def seq_rows(sq):
        return slice(sq * q, (sq + 1) * q)

    def chunk_rows(ci):
        return slice(ci * rows, (ci + 1) * rows)

    def stack_rows(x):
        return jnp.concatenate([x[seq_rows(sq)] for sq in range(nseq) for _ in range(hpb)], axis=0)

    def masked_bf16(x, mask):
        return (x.astype(mask_dtype) * mask).astype(BF16)

    def block_diag(m_bf16):
        return masked_bf16(stack_rows(m_bf16.astype(mask_dtype)), bdmask)

    def block_diag_pieces(m):
        hi = m.astype(BF16)
        lo = (m - hi.astype(F32)).astype(BF16)
        return block_diag(hi), block_diag(lo)

    def lhs_pieces(m):
        hi = m.astype(BF16)
        return hi, (m - hi.astype(F32)).astype(BF16)

    sm = sm_ref[...]
    beta = jax.nn.sigmoid(sm)
    g = a_ref[...] * _softplus(sm + dtb_ref[...])
    gc = _dot_exact_rhs(l3_ref[...], g)
    gc3 = _split3_lanes(gc)
    beta_e = _dot(_split3_lanes(beta), e3b_ref[...])
    gc_e = _dot(gc3, e3a_ref[...])
    col_c = _dot(gc3, e3c_ref[...])
    row_c = _dot_exact_rhs(ones3_ref[...], col_c * dmask_ref[...])
    decay_c = jnp.exp(jnp.minimum(col_c - row_c, 0.0)) * incl_ref[...]
    egc_e = jnp.exp(gc_e)
    qn = qkv_ref[:, :GDN_WIDTH]
    kn = qkv_ref[:, GDN_WIDTH:2 * GDN_WIDTH]
    kb = kn * beta_e
    vb = qkv_ref[:, 2 * GDN_WIDTH:] * beta_e
    kbg = kb * egc_e
    qin = qn * egc_e

    def chunk_local(ci):
        r0 = ci * rows
        lasts = [gc_e[r0 + (sq + 1) * q - 1:r0 + (sq + 1) * q, :] for sq in range(nseq)]
        eouts = [jnp.exp(lasts[sq] - gc_e[r0 + sq * q:r0 + (sq + 1) * q]) for sq in range(nseq)]
        cr = chunk_rows(ci)
        return dict(
            qn=qn[cr], kn=kn[cr], kb=kb[cr], vb=vb[cr], kbg=kbg[cr], qin=qin[cr], decay_c=decay_c[cr],
            kout=(kn[cr] * (eouts[0] if nseq == 1 else jnp.concatenate(eouts, axis=0))).astype(BF16),
            ge=[jnp.exp(last) for last in lasts])

    def chain_init(loc, jb):
        nl = slice(jb * bw, (jb + 1) * bw)
        cl = slice(jb * MXU_DIM, (jb + 1) * MXU_DIM)
        k_bd = masked_bf16(stack_rows(loc["kn"][:, nl].astype(mask_dtype)), kmask)
        prod = _dot_nt(jnp.concatenate([loc["kb"][:, nl], loc["qn"][:, nl]], axis=0).astype(BF16), k_bd)
        dec = loc["decay_c"][:, cl]
        x = -(prod[:rows] * dec * strict_ref[:, cl])
        return dict(inv=eye_ref[...] + x, ypow=x, attn=(prod[rows:] * dec).astype(BF16))

    def double_step(st, k):
        first, last = k == 0, k == n_sq - 1
        inv, ypow = st["inv"], st["ypow"]
        b_hi, b_lo = block_diag_pieces(ypow)
        lhs = ypow if first else (inv if last else jnp.concatenate([inv, ypow], axis=0))
        l_hi, l_lo = lhs_pieces(lhs)
        res = _dot(jnp.concatenate([l_hi, l_hi, l_lo], axis=1), jnp.concatenate([b_hi, b_lo, b_hi], axis=0))
        if first:
            return dict(st, ypow=res)
        if last:
            return dict(st, inv=inv + res)
        return dict(st, inv=inv + res[:rows], ypow=res[rows:])

    def solve(loc, jb, inv):
        i_hi, i_lo = block_diag_pieces(inv)
        segs = [(sq, jb * hpb + hh) for sq in range(nseq) for hh in range(hpb)]
        rhs = jnp.concatenate([jnp.concatenate([loc["vb"][seq_rows(sq), head_lanes(h)],
                                                loc["kbg"][seq_rows(sq), head_lanes(h)]], axis=1)
                               for sq, h in segs], axis=0).astype(BF16)
        return _dot(jnp.concatenate([i_hi, i_lo], axis=1), jnp.concatenate([rhs, rhs], axis=0))

    def state_step(loc, jb, uw, attn):
        segs = [(sq, jb * hpb + hh) for sq in range(nseq) for hh in range(hpb)]
        ws, qs, states = [], [], []
        for i, (sq, h) in enumerate(segs):
            s = s_ref[sq, h]
            states.append(s)
            wq = jnp.concatenate([uw[i * q:(i + 1) * q, GDN_DIM:], loc["qin"][seq_rows(sq), head_lanes(h)]], axis=0)
            r = _dot(wq.astype(BF16), s.astype(BF16))
            ws.append(r[:q])
            qs.append(r[q:])
        v_bf = (uw[:, :GDN_DIM] - jnp.concatenate(ws, axis=0)).astype(BF16)
        v_rep = masked_bf16(jnp.concatenate([v_bf.astype(mask_dtype)] * hpb, axis=1), kmask)
        qs_rows = [jnp.concatenate(qs[sq * hpb:(sq + 1) * hpb], axis=1) for sq in range(nseq)]
        qs_nat = qs_rows[0] if nseq == 1 else jnp.concatenate(qs_rows, axis=0)
        for i, (sq, h) in enumerate(segs):
            s_ref[sq, h] = (states[i] * loc["ge"][sq][:, head_lanes(h)]
                            + _dot_tn(loc["kout"][seq_rows(sq), head_lanes(h)], v_bf[i * q:(i + 1) * q]))
        return qs_nat + _dot(attn, v_rep)

    local = [chunk_local(ci) for ci in range(cps)]
    chains = [(ci, jb) for ci in range(cps) for jb in range(nhb)]
    st = [chain_init(local[ci], jb) for ci, jb in chains]
    for k in range(n_sq):
        st = [double_step(s, k) for s in st]
    uws =[solve(local[ci], jb, s["inv"]) for (ci, jb), s in zip(chains, st)]
    nw = nw_ref[...]
    for ci in range(cps):
        o_blocks = [state_step(local[ci], jb, uws[ci * nhb + jb], st[ci * nhb + jb]["attn"]) for jb in range(nhb)]
        o = o_blocks[0] if nhb == 1 else jnp.concatenate(o_blocks, axis=1)
        gate = gate_ref[chunk_rows(ci)]
        outs = [_rms(o[:, head_lanes(h)], nw) * _silu(gate[:, head_lanes(h)]) for h in range(GDN_HEADS)]
        o_ref[chunk_rows(ci)] = jnp.concatenate(outs, axis=1).astype(o_ref.dtype)

    @pl.when(c == nc - 1)
    def _():
        sout_ref[...] = s_ref[...]


def _gdn_call(qkv, gate, small, s0, dtb, a_pad, nw, b, l, q):
    nseq = max(1, MXU_DIM // (GDN_HEADS * q))
    hpb = MXU_DIM // (nseq * q)
    cps = SCAN_UNITS_PER_STEP if nseq == 1 and (l // q) % SCAN_UNITS_PER_STEP == 0 else 1
    consts = _gdn_constants(q, nseq, hpb, cps)
    nc = l // (q * cps)
    assert b % nseq == 0 and (nseq == 1 or nc == 1)
    t = b * l
    rows = nseq * q * cps
    row = lambda i, c: (i * nc + c, 0)
    seq = lambda i, c: (i, 0, 0, 0)
    params = [dtb, a_pad, nw] + [consts[k] for k in
                                 ("l3", "ones3", "e3b", "e3a", "e3c", "dmask", "incl", "strict", "eye",
                                  "bdmask", "kmask")]
    s_shape = (GDN_HEADS, GDN_DIM, GDN_DIM)
    return pl.pallas_call(
        functools.partial(_gdn_kernel, q=q, nseq=nseq, hpb=hpb, cps=cps),
        grid=(b // nseq, nc),
        in_specs=[pl.BlockSpec((rows, GDN_CONV_DIM), row),
                  pl.BlockSpec((rows, GDN_WIDTH), row),
                  pl.BlockSpec((rows, LANES), row),
                  pl.BlockSpec((nseq,) + s_shape, seq)]
                 + [_const_spec(p.shape) for p in params],
        out_specs=[pl.BlockSpec((rows, GDN_WIDTH), row),
                   pl.BlockSpec((nseq,) + s_shape, seq)],
        out_shape=[jax.ShapeDtypeStruct((t, GDN_WIDTH), F32),
                   jax.ShapeDtypeStruct((b,) + s_shape, F32)],
        scratch_shapes=[pltpu.VMEM((nseq,) + s_shape, F32)],
        compiler_params=_params2(),
        name="gdn_scan",
    )(qkv, gate, small, s0, *params)


def _outffn_kernel(x_ref, y_ref, o_ref, mod_ref, nwm_ref, nwf_ref, nwp_ref,
                   wout_ref, wgu_ref, wdown_ref, out_ref, *, ff_chunk):
    x = x_ref[...]
    shape3 = x.shape
    tm = shape3[0] * shape3[1]
    mod = mod_ref[...]
    g1, sh2, sc2, g2 = (mod[:, k:k + 1, :] for k in range(2, 6))
    m = (_dot(y_ref[...].astype(BF16), wout_ref[:SSD_WIDTH, :])
         + _dot(o_ref[...].astype(BF16), wout_ref[SSD_WIDTH:, :]))
    x1 = x + g1 * _rms(m, nwm_ref[...]).reshape(shape3)
    h2 = _rms(x1, nwf_ref[...]) * (1.0 + sc2) + sh2
    h2 = h2.reshape(tm, shape3[2]).astype(BF16)
    f = jnp.zeros((tm, shape3[2]), F32)
    for j in range(D_FF // ff_chunk):
        gj = _dot(h2, wgu_ref[:, j * ff_chunk:(j + 1) * ff_chunk])
        uj = _dot(h2, wgu_ref[:, D_FF + j * ff_chunk:D_FF + (j + 1) * ff_chunk])
        f = f + _dot((_silu(gj) * uj).astype(BF16), wdown_ref[j * ff_chunk:(j + 1) * ff_chunk, :])
    out_ref[...] = x1 + g2 * _rms(f, nwp_ref[...]).reshape(shape3)


def _outffn_call(x, y, o, mod6, nw_mix_post, nw_ffn_pre, nw_ffn_post, w_out, w_gu, w_down, bb, lb):
    b, l, d = x.shape
    nl = l // lb
    tm = bb * lb
    row = lambda i, j: (i * nl + j, 0)
    return pl.pallas_call(
        functools.partial(_outffn_kernel, ff_chunk=D_FF // 2),
        grid=(b // bb, nl),
        in_specs=[pl.BlockSpec((bb, lb, d), lambda i, j: (i, j, 0)),
                  pl.BlockSpec((tm, SSD_WIDTH), row),
                  pl.BlockSpec((tm, GDN_WIDTH), row),
                  pl.BlockSpec((bb,) + mod6.shape[1:], lambda i, j: (i, 0, 0)),
                  _const_spec((1, d)), _const_spec((1, d)), _const_spec((1, d)),
                  _const_spec(w_out.shape), _const_spec(w_gu.shape), _const_spec(w_down.shape)],
        out_specs=pl.BlockSpec((bb, lb, d), lambda i, j: (i, j, 0)),
        out_shape=jax.ShapeDtypeStruct((b, l, d), F32),
        compiler_params=_params2(),
        name="outproj_ffn",
    )(x, y, o, mod6, nw_mix_post, nw_ffn_pre, nw_ffn_post, w_out, w_gu, w_down)


_O_DT = SSD_WIDTH + SSD_CONV_DIM
_O_QKV = _O_DT + SSD_HEADS
_O_BETA = _O_QKV + GDN_CONV_DIM + GDN_WIDTH
_IN_DIM = _O_BETA + 2 * GDN_HEADS


def _repack_kernel(wt_ref, dt_ref, ba_ref, big_ref, small_ref):
    big_ref[...] = wt_ref[...].T.astype(BF16)

    @pl.when(pl.program_id(0) == 0)
    def _():
        k = dt_ref.shape[1]
        pad = jnp.zeros((LANES - dt_ref.shape[0] - ba_ref.shape[0], k), F32)
        small_ref[...] = jnp.concatenate([dt_ref[...], ba_ref[...], pad], axis=0).T.astype(BF16)


def _repack_w_in(w):
    k, n = w.shape
    tc = 2 * MXU_DIM
    assert n == _IN_DIM and _O_DT % tc == 0
    wt = w.T
    big = n - SSD_HEADS - 2 * GDN_HEADS
    skip = _O_QKV - _O_DT
    src_row = lambda j: (pl.multiple_of(jnp.where(j * tc < _O_DT, j * tc, j * tc + skip), skip), 0)
    window = lambda rows, index: pl.BlockSpec((pl.Element(rows), pl.Element(k)), index)
    return pl.pallas_call(
        _repack_kernel,
        grid=(big // tc,),
        in_specs=[window(tc, src_row),
                  window(SSD_HEADS, lambda j: (_O_DT, 0)),
                  window(2 * GDN_HEADS, lambda j: (_O_BETA, 0))],
        out_specs=[pl.BlockSpec((k, tc), lambda j: (0, j)), pl.BlockSpec((k, LANES), lambda j: (0, 0))],
        out_shape=[jax.ShapeDtypeStruct((k, big), BF16), jax.ShapeDtypeStruct((k, LANES), BF16)],
        compiler_params=_params2(("arbitrary",)),
        name="repack_w_in",
    )(wt, wt, wt)


def _pad_lanes(v, lane0):
    out = jnp.zeros((1, LANES), F32)
    return out.at[0, lane0:lane0 + v.shape[0]].set(v.astype(F32))


def _layer(x, mod, ssd_h0, ssd_conv0, gdn_s0, gdn_conv0, p, q, bb, lb):
    b, l, d = x.shape
    mod6 = mod.reshape(b, 6, d)
    z, xbc, qkv, gate, small, ssd_conv, gdn_conv = _inproj_call(
        x, mod6, p["norm_mix_pre"], p["w_big"], p["w_small"], ssd_conv0, gdn_conv0,
        p["ssd_conv_w"], p["ssd_conv_b"], p["gdn_conv_w"], bb, lb)
    y, ssd_h = _ssd_call(xbc, z, small, ssd_h0.reshape(b, SSD_WIDTH, SSD_STATE),
                         p["ssd_dtb"], p["ssd_a"], p["ssd_d"], p["ssd_norm_w"], b, l, q)
    o, gdn_s = _gdn_call(qkv, gate, small, gdn_s0, p["gdn_dtb"], p["gdn_a"], p["gdn_norm_w"], b, l, q)
    out = _outffn_call(x, y, o, mod6, p["norm_mix_post"], p["norm_ffn_pre"], p["norm_ffn_post"],
                       p["w_out"], p["w_gu"], p["w_down"], bb, lb)
    return out, ssd_h.reshape(ssd_h0.shape), ssd_conv, gdn_s, gdn_conv


def _layer_params(l, w_ada, b_ada, norm_mix_pre, norm_mix_post, norm_ffn_pre, norm_ffn_post, w_in, ssd_conv_w,
                  ssd_conv_b, ssd_dt_bias, ssd_A_log, ssd_D, ssd_norm_w, gdn_conv_w, gdn_dt_bias, gdn_A_log,
                  gdn_norm_w, w_out, w_gate_up, w_down):
    w_big, w_small = _repack_w_in(w_in[l])
    row = lambda v: v.reshape(1, -1).astype(F32)
    return dict(
        w_ada=w_ada[l], b_ada=b_ada[l],
        norm_mix_pre=row(norm_mix_pre[l]), norm_mix_post=row(norm_mix_post[l]),
        norm_ffn_pre=row(norm_ffn_pre[l]), norm_ffn_post=row(norm_ffn_post[l]),
        w_big=w_big, w_small=w_small,
        ssd_conv_w=ssd_conv_w[l], ssd_conv_b=row(ssd_conv_b[l]),
        ssd_dtb=_pad_lanes(ssd_dt_bias[l], DT_LANE0),
        ssd_a=_pad_lanes(-jnp.exp(ssd_A_log[l].astype(F32)), DT_LANE0),
        ssd_d=row(jnp.repeat(ssd_D[l], SSD_HEADDIM)),
        ssd_norm_w=row(ssd_norm_w[l]),
        gdn_conv_w=gdn_conv_w[l],
        gdn_dtb=_pad_lanes(gdn_dt_bias[l], ALPHA_LANE0),
        gdn_a=_pad_lanes(-jnp.exp(gdn_A_log[l].astype(F32)), ALPHA_LANE0),
        gdn_norm_w=row(gdn_norm_w[l]),
        w_out=w_out[l].astype(BF16), w_gu=w_gate_up[l].astype(BF16), w_down=w_down[l].astype(BF16),
    )


def kernel(x_prompt, x_sample, c_prompt, c_sample, state_ssd, state_ssd_conv, state_gdn, state_gdn_conv, w_ada, b_ada, norm_mix_pre, norm_mix_post, norm_ffn_pre, norm_ffn_post, w_in, ssd_conv_w, ssd_conv_b, ssd_dt_bias, ssd_A_log, ssd_D, ssd_norm_w, gdn_conv_w, gdn_dt_bias, gdn_A_log, gdn_norm_w, w_out, w_gate_up, w_down):
    depth = w_in.shape[0]
    bp, lp, _ = x_prompt.shape
    bs, ls, _ = x_sample.shape
    yp, ys = x_prompt, x_sample
    outs = [[] for _ in range(8)]
    for l in range(depth):
        p = _layer_params(l, w_ada, b_ada, norm_mix_pre, norm_mix_post, norm_ffn_pre, norm_ffn_post, w_in,
                          ssd_conv_w, ssd_conv_b, ssd_dt_bias, ssd_A_log, ssd_D, ssd_norm_w, gdn_conv_w,
                          gdn_dt_bias, gdn_A_log, gdn_norm_w, w_out, w_gate_up, w_down)
        mod = _mod_call(jnp.concatenate([c_prompt, c_sample], axis=0), p["w_ada"], p["b_ada"])
        zeros = lambda a: jnp.zeros((bp,) + a.shape[2:], a.dtype)
        yp, a0, a1, a2, a3 = _layer(yp, mod[:bp], zeros(state_ssd), zeros(state_ssd_conv), zeros(state_gdn),
                                    zeros(state_gdn_conv), p, q=min(CHUNK, lp), bb=1, lb=min(ROW_TILE, lp))
        ys, b0, b1, b2, b3 = _layer(ys, mod[bp:], state_ssd[l], state_ssd_conv[l], state_gdn[l],
                                    state_gdn_conv[l], p, q=min(CHUNK, ls), bb=min(ROW_TILE // ls, bs), lb=ls)
        for lst, v in zip(outs, (a0, a1, a2, a3, b0, b1, b2, b3)):
            lst.append(v)
    return (yp, ys) + tuple(v[0][None] if depth == 1 else jnp.stack(v) for v in outs)
```

```python
import functools

import numpy as np
import jax
import jax.numpy as jnp
from jax import lax
from jax.experimental import pallas as pl
from jax.experimental.pallas import tpu as pltpu

F32 = jnp.float32
BF16 = jnp.bfloat16

D_MODEL = 1024
SSD_HEADS = 16
SSD_HEADDIM = 64
SSD_GROUPS = 2
SSD_STATE = 128
SSD_WIDTH = SSD_HEADS * SSD_HEADDIM
SSD_CONV_DIM = SSD_WIDTH + 2 * SSD_GROUPS * SSD_STATE
GDN_HEADS = 8
GDN_DIM = 128
GDN_WIDTH = GDN_HEADS * GDN_DIM
GDN_CONV_DIM = 3 * GDN_WIDTH
CONV_WIDTH = 4
CHUNK = 64
D_FF = 2816
NORM_EPS = 1e-6
L2_EPS = 1e-6
LANES = 128
SUBLANES = 8
MXU_DIM = 256
DT_LANE0, BETA_LANE0, ALPHA_LANE0 = 0, SSD_HEADS, SSD_HEADS + GDN_HEADS
VMEM_LIMIT = 56 * 1024 * 1024
ROW_TILE = 256
SCAN_UNITS_PER_STEP = 4
CONV_COLS = 512


def _dot(a, b):
    return jnp.dot(a, b, preferred_element_type=F32)


def _dot_nt(a, b):
    return lax.dot_general(a, b, (((1,), (1,)), ((), ())), preferred_element_type=F32)


def _dot_tn(a, b):
    return lax.dot_general(a, b, (((0,), (0,)), ((), ())), preferred_element_type=F32)


def _bf16_pieces(x):
    hi = x.astype(BF16)
    r1 = x - hi.astype(F32)
    mid = r1.astype(BF16)
    lo = (r1 - mid.astype(F32)).astype(BF16)
    return hi, mid, lo


def _split3_lanes(x):
    hi, mid, lo = _bf16_pieces(x)
    return jnp.concatenate([hi.astype(F32), mid.astype(F32), lo.astype(F32)], axis=1).astype(BF16)


def _dot_exact_rhs(a_bf16, x):
    hi, mid, lo = _bf16_pieces(x)
    return _dot(a_bf16, hi) + _dot(a_bf16, mid) + _dot(a_bf16, lo)


def _silu(x):
    return x * jax.nn.sigmoid(x)


def _softplus(x):
    return jnp.maximum(x, 0.0) + jnp.log1p(jnp.exp(-jnp.abs(x)))


def _rms(x, w):
    return x * lax.rsqrt(jnp.mean(x * x, axis=-1, keepdims=True) + NORM_EPS) * w


def _const_spec(shape):
    nd = len(shape)
    return pl.BlockSpec(shape, lambda *_: (0,) * nd, pipeline_mode=pl.Buffered(1))


def _params2(sem=("arbitrary", "arbitrary")):
    return pltpu.CompilerParams(dimension_semantics=sem, vmem_limit_bytes=VMEM_LIMIT)


def _mod_kernel(c_ref, w_ref, b_ref, o_ref):
    a = _silu(c_ref[...]).astype(BF16)
    o_ref[...] = _dot(a, w_ref[...].astype(BF16)) + b_ref[...]


def _mod_call(c_all, w_ada, b_ada):
    m, d = c_all.shape
    n = w_ada.shape[1]
    tn = 1536
    return pl.pallas_call(
        _mod_kernel,
        grid=(n // tn,),
        in_specs=[pl.BlockSpec((m, d), lambda j: (0, 0)),
                  pl.BlockSpec((d, tn), lambda j: (0, j)),
                  pl.BlockSpec((1, tn), lambda j: (0, j))],
        out_specs=pl.BlockSpec((m, tn), lambda j: (0, j)),
        out_shape=jax.ShapeDtypeStruct((m, n), F32),
        compiler_params=_params2(("arbitrary",)),
        name="adaln_mod",
    )(c_all, w_ada, b_ada.reshape(1, n))


def _proj_conv_silu(h, w_ref, w_col0, width, bb, l, tail_ref, convout_ref, cw_ref, cb_ref, out_ref, n_norm, norm_scale):
    tm = bb * l
    for k in range(width // CONV_COLS):
        cols = slice(k * CONV_COLS, (k + 1) * CONV_COLS)
        raw = _dot(h, w_ref[:, w_col0 + k * CONV_COLS:w_col0 + (k + 1) * CONV_COLS]).reshape(bb, l, CONV_COLS)
        ext = jnp.concatenate([tail_ref[:, :, cols], raw], axis=1)
        acc = ext[:, SUBLANES:SUBLANES + l] * cw_ref[CONV_WIDTH - 1:CONV_WIDTH, cols]
        for j in range(1, CONV_WIDTH):
            acc = acc + ext[:, SUBLANES - j:SUBLANES - j + l] * cw_ref[CONV_WIDTH - 1 - j:CONV_WIDTH - j, cols]
        tail_ref[:, :, cols] = ext[:, l:l + SUBLANES]
        convout_ref[:, :, cols] = ext[:, l + SUBLANES - (CONV_WIDTH - 1):l + SUBLANES]
        if cb_ref is not None:
            acc = acc + cb_ref[:, cols]
        act = _silu(acc).reshape(tm, CONV_COLS)
        for hh in range(CONV_COLS // LANES):
            head = k * (CONV_COLS // LANES) + hh
            xh = act[:, hh * LANES:(hh + 1) * LANES]
            if head < n_norm:
                xh = xh * (lax.rsqrt(jnp.sum(xh * xh, axis=-1, keepdims=True) + L2_EPS) * norm_scale[head])
            out_ref[:, head * LANES:(head + 1) * LANES] = xh


def _inproj_kernel(x_ref, mod_ref, nw_ref, wbig_ref, wsm_ref, sconv0_ref, gconv0_ref,
                   scw_ref, scb_ref, gcw_ref,
                   z_ref, xbc_ref, qkv_ref, gate_ref, sm_ref, sconv_ref, gconv_ref, stail_ref, gtail_ref):
    @pl.when(pl.program_id(1) == 0)
    def _():
        for tail_ref, conv0_ref in ((stail_ref, sconv0_ref), (gtail_ref, gconv0_ref)):
            tail_ref[...] = jnp.zeros(tail_ref.shape, F32)
            tail_ref[:, SUBLANES - (CONV_WIDTH - 1):, :] = conv0_ref[...]

    x = x_ref[...]
    bb, l, d = x.shape
    tm = bb * l
    mod = mod_ref[...]
    h = _rms(x, nw_ref[...]) * (1.0 + mod[:, 1:2, :]) + mod[:, 0:1, :]
    h = h.reshape(tm, d).astype(BF16)
    c_xbc = SSD_WIDTH
    c_qkv = c_xbc + SSD_CONV_DIM
    c_gate = c_qkv + GDN_CONV_DIM
    z_ref[...] = _dot(h, wbig_ref[:, :c_xbc])
    gate_ref[...] = _dot(h, wbig_ref[:, c_gate:])
    sm_ref[...] = _dot(h, wsm_ref[...])
    _proj_conv_silu(h, wbig_ref, c_xbc, SSD_CONV_DIM, bb, l, stail_ref, sconv_ref, scw_ref, scb_ref, xbc_ref, 0, ())
    qk_scale = (GDN_DIM ** -0.5,) * GDN_HEADS + (1.0,) * GDN_HEADS
    _proj_conv_silu(h, wbig_ref, c_qkv, GDN_CONV_DIM, bb, l, gtail_ref, gconv_ref, gcw_ref, None, qkv_ref,
                    2 * GDN_HEADS, qk_scale)


def _inproj_call(x, mod6, norm_w, w_big, w_small, sconv0, gconv0, scw, scb, gcw, bb, lb):
    b, l, d = x.shape
    nl = l // lb
    tm = bb * lb
    t = b * l
    row = lambda i, j: (i * nl + j, 0)
    seq = lambda i, j: (i, 0, 0)
    widths = (SSD_WIDTH, SSD_CONV_DIM, GDN_CONV_DIM, GDN_WIDTH, LANES)
    hist = CONV_WIDTH - 1
    return pl.pallas_call(
        _inproj_kernel,
        grid=(b // bb, nl),
        in_specs=[pl.BlockSpec((bb, lb, d), lambda i, j: (i, j, 0)),
                  pl.BlockSpec((bb,) + mod6.shape[1:], lambda i, j: (i, 0, 0)),
                  _const_spec((1, d)),
                  _const_spec(w_big.shape),
                  _const_spec(w_small.shape),
                  pl.BlockSpec((bb, hist, SSD_CONV_DIM), seq),
                  pl.BlockSpec((bb, hist, GDN_CONV_DIM), seq),
                  _const_spec(scw.shape), _const_spec(scb.shape), _const_spec(gcw.shape)],
        out_specs=[pl.BlockSpec((tm, w), row) for w in widths]
                  + [pl.BlockSpec((bb, hist, SSD_CONV_DIM), seq), pl.BlockSpec((bb, hist, GDN_CONV_DIM), seq)],
        out_shape=[jax.ShapeDtypeStruct((t, w), F32) for w in widths]
                  + [jax.ShapeDtypeStruct((b, hist, SSD_CONV_DIM), F32),
                     jax.ShapeDtypeStruct((b, hist, GDN_CONV_DIM), F32)],
        scratch_shapes=[pltpu.VMEM((bb, SUBLANES, SSD_CONV_DIM), F32),
                        pltpu.VMEM((bb, SUBLANES, GDN_CONV_DIM), F32)],
        compiler_params=_params2(),
        name="inproj",
    )(x, mod6, norm_w, w_big, w_small, sconv0, gconv0, scw, scb, gcw)


def _ssd_constants(q, nu):
    hq = SSD_HEADS * q
    hp = LANES // q
    w = hp * SSD_HEADDIM
    e_p = np.zeros((LANES, SSD_WIDTH), np.float32)
    e_q = np.zeros((LANES, hq), np.float32)
    for r in range(SSD_HEADS):
        e_p[DT_LANE0 + r, r * SSD_HEADDIM:(r + 1) * SSD_HEADDIM] = 1.0
        e_q[DT_LANE0 + r, r * q:(r + 1) * q] = 1.0
    tri = np.tril(np.ones((q, q), np.float32))
    gmask = np.zeros((hq, 2 * SSD_STATE), np.float32)
    half = hq // SSD_GROUPS
    gmask[:half, :SSD_STATE] = 1.0
    gmask[half:, SSD_STATE:] = 1.0
    xmask = np.kron(np.eye(hp, dtype=np.float32), np.ones((q, SSD_HEADDIM), np.float32))
    unit_eye = np.eye(nu, dtype=np.float32)
    return dict(
        l3=jnp.asarray(np.kron(unit_eye, tri), BF16),
        ones3=jnp.asarray(np.kron(unit_eye, np.ones((q, q), np.float32)), BF16),
        e3p=jnp.asarray(np.tile(e_p, (3, 1)), BF16),
        e3q=jnp.asarray(np.tile(e_q, (3, 1)), BF16),
        dmask=jnp.asarray(np.tile(np.eye(q, dtype=np.float32), (nu, SSD_HEADS))),
        causal=jnp.asarray(np.tile(tri, (nu, SSD_HEADS))),
        gmask=jnp.asarray(gmask),
        xmask=jnp.asarray(xmask),
    ), hp, w


def _ssd_kernel(xbc_ref, z_ref, sm_ref, h0_ref, dtb_ref, a_ref, d_ref, nw_ref,
                l3_ref, ones3_ref, e3p_ref, e3q_ref, dmask_ref, causal_ref, gmask_ref, xmask_ref,
                y_ref, hout_ref, ht_ref, *, q, hp, w, nseq, cps):
    c = pl.program_id(1)
    nc = pl.num_programs(1)
    nu = nseq * cps
    rows = nu * q
    gw = SSD_WIDTH // SSD_GROUPS

    @pl.when(c == 0)
    def _():
        for sq in range(nseq):
            ht_ref[sq] = h0_ref[sq].T

    def unit_rows(u):
        return slice(u * q, (u + 1) * q)

    xs = xbc_ref[:, :SSD_WIDTH]
    bcat = xbc_ref[:, SSD_WIDTH:SSD_WIDTH + 2 * SSD_STATE]
    ccat = xbc_ref[:, SSD_WIDTH + 2 * SSD_STATE:].astype(BF16)

    dt = _softplus(sm_ref[...] + dtb_ref[...])
    acs = _dot_exact_rhs(l3_ref[...], dt * a_ref[...])
    dec_ends = [jnp.exp(acs[(u + 1) * q - 1:(u + 1) * q, :] - acs[unit_rows(u)]) for u in range(nu)]
    dec_end = dec_ends[0] if nu == 1 else jnp.concatenate(dec_ends, axis=0)
    per_head = jnp.concatenate([dt, dt * dec_end, jnp.exp(acs)], axis=0)
    exp3 = _dot(_split3_lanes(per_head), e3p_ref[...])
    dt_e, dtd_e, eacs_e = exp3[:rows], exp3[rows:2 * rows], exp3[2 * rows:]

    col_all = _dot(_split3_lanes(acs), e3q_ref[...])
    row_all = _dot_exact_rhs(ones3_ref[...], col_all * dmask_ref[...])
    decay = jnp.exp(jnp.minimum(col_all - row_all, 0.0)) * causal_ref[...]
    xdt = xs * dt_e
    xd = (xs * dtd_e).astype(BF16)
    dxs = d_ref[...] * xs
    gmask = gmask_ref[...]
    xmask = xmask_ref[...]

    def cb_all(u):
        if 2 * q == LANES:
            per_group = []
            for g in range(SSD_GROUPS):
                gl = slice(g * SSD_STATE, (g + 1) * SSD_STATE)
                cb = _dot_nt(ccat[unit_rows(u), gl], bcat[unit_rows(u), gl].astype(BF16))
                per_group += [jnp.concatenate([cb, cb], axis=1)] * (SSD_HEADS // SSD_GROUPS // 2)
            return jnp.concatenate(per_group, axis=1)
        brep = (jnp.concatenate([bcat[unit_rows(u)]] * SSD_HEADS, axis=0) * gmask).astype(BF16)
        return _dot_nt(ccat[unit_rows(u)], brep)

    cbs = [cb_all(u) for u in range(nu)]
    m_alls = [(cbs[u] * decay[unit_rows(u)]).astype(BF16) for u in range(nu)]
    y_diags, upds = [], []
    for u in range(nu):
        blocks = []
        for j in range(SSD_WIDTH // w):
            xj = (jnp.concatenate([xdt[unit_rows(u), j * w:(j + 1) * w]] * hp, axis=0) * xmask).astype(BF16)
            blocks.append(_dot(m_alls[u][:, j * LANES:(j + 1) * LANES], xj))
        y_diags.append(blocks[0] if len(blocks) == 1 else jnp.concatenate(blocks, axis=1))
        upds.append(jnp.concatenate(
            [_dot_tn(bcat[unit_rows(u), g * SSD_STATE:(g + 1) * SSD_STATE].astype(BF16),
                     xd[unit_rows(u), g * gw:(g + 1) * gw]) for g in range(SSD_GROUPS)], axis=1))

    nw = nw_ref[...]
    for sq in range(nseq):
        ht = ht_ref[sq]
        for ci in range(cps):
            u = sq * cps + ci
            ht_bf = ht.astype(BF16)
            y_off = jnp.concatenate([_dot(ccat[unit_rows(u), g * SSD_STATE:(g + 1) * SSD_STATE],
                                          ht_bf[:, g * gw:(g + 1) * gw]) for g in range(SSD_GROUPS)], axis=1)
            e_u = eacs_e[unit_rows(u)]
            y = y_diags[u] + y_off * e_u + dxs[unit_rows(u)]
            ht = ht * e_u[q - 1:q, :] + upds[u]
            yz = y * _silu(z_ref[unit_rows(u)])
            outs = [_rms(yz[:, g * gw:(g + 1) * gw], nw[:, g * gw:(g + 1) * gw]) for g in range(SSD_GROUPS)]
            y_ref[unit_rows(u)] = jnp.concatenate(outs, axis=1).astype(y_ref.dtype)
        ht_ref[sq] = ht

    @pl.when(c == nc - 1)
    def _():
        for sq in range(nseq):
            hout_ref[sq] = ht_ref[sq].T


def _scan_units(b, l, q):
    nchunks = l // q
    cps = SCAN_UNITS_PER_STEP if nchunks % SCAN_UNITS_PER_STEP == 0 else 1
    nseq = SCAN_UNITS_PER_STEP if nchunks == 1 and b % SCAN_UNITS_PER_STEP == 0 else 1
    return nseq, cps


def _ssd_call(xbc, z, small, h0, dtb, a_pad, d_e, nw, b, l, q):
    nseq, cps = _scan_units(b, l, q)
    consts, hp, w = _ssd_constants(q, nseq * cps)
    nc = l // (q * cps)
    t = b * l
    rows = nseq * cps * q
    row = lambda i, c: (i * nc + c, 0)
    seq = lambda i, c: (i, 0, 0)
    params = [dtb, a_pad, d_e, nw] + [consts[k] for k in
                                      ("l3", "ones3", "e3p", "e3q", "dmask", "causal", "gmask", "xmask")]
    hp_shape = (SSD_WIDTH, SSD_STATE)
    return pl.pallas_call(
        functools.partial(_ssd_kernel, q=q, hp=hp, w=w, nseq=nseq, cps=cps),
        grid=(b // nseq, nc),
        in_specs=[pl.BlockSpec((rows, SSD_CONV_DIM), row),
                  pl.BlockSpec((rows, SSD_WIDTH), row),
                  pl.BlockSpec((rows, LANES), row),
                  pl.BlockSpec((nseq,) + hp_shape, seq)]
                 + [_const_spec(p.shape) for p in params],
        out_specs=[pl.BlockSpec((rows, SSD_WIDTH), row),
                   pl.BlockSpec((nseq,) + hp_shape, seq)],
        out_shape=[jax.ShapeDtypeStruct((t, SSD_WIDTH), F32),
                   jax.ShapeDtypeStruct((b,) + hp_shape, F32)],
        scratch_shapes=[pltpu.VMEM((nseq, SSD_STATE, SSD_WIDTH), F32)],
        compiler_params=_params2(),
        name="ssd_scan",
    )(xbc, z, small, h0, *params)


def _gdn_constants(q, nseq, hpb, cps):
    assert nseq * hpb * q == MXU_DIM
    nhb = GDN_HEADS // hpb
    rows = nseq * q
    e_beta = np.zeros((LANES, GDN_WIDTH), np.float32)
    e_alpha = np.zeros((LANES, GDN_WIDTH), np.float32)
    e_c = np.zeros((LANES, nhb * MXU_DIM), np.float32)
    for h in range(GDN_HEADS):
        e_beta[BETA_LANE0 + h, h * GDN_DIM:(h + 1) * GDN_DIM] = 1.0
        e_alpha[ALPHA_LANE0 + h, h * GDN_DIM:(h + 1) * GDN_DIM] = 1.0
    jh, seq_c, hh_c, s_c = np.unravel_index(np.arange(nhb * MXU_DIM), (nhb, nseq, hpb, q))
    e_c[ALPHA_LANE0 + jh * hpb + hh_c, np.arange(nhb * MXU_DIM)] = 1.0
    seq_r, l_r = np.unravel_index(np.arange(rows), (nseq, q))
    same_seq = seq_r[:, None] == seq_c[None, :]
    dmask = same_seq & (l_r[:, None] == s_c[None, :])
    incl = same_seq & (l_r[:, None] >= s_c[None, :])
    strict = same_seq & (l_r[:, None] > s_c[None, :])
    seq_s, hh_s, _ = np.unravel_index(np.arange(MXU_DIM), (nseq, hpb, q))
    bdmask = (seq_s[:, None] == seq_c[None, :MXU_DIM]) & (hh_s[:, None] == hh_c[None, :MXU_DIM])
    kmask = hh_s[:, None] == (np.arange(hpb * GDN_DIM) // GDN_DIM)[None, :]
    f = lambda m: jnp.asarray(m.astype(np.float32))
    per_chunk = lambda m: jnp.asarray(np.tile(m.astype(np.float32), (cps, 1)))
    chunk_eye = np.eye(cps)
    mask_dtype = BF16 if q % (2 * SUBLANES) == 0 else F32
    return dict(
        l3=jnp.asarray(np.kron(np.eye(cps * nseq), np.tril(np.ones((q, q)))), BF16),
        ones3=jnp.asarray(np.kron(chunk_eye, np.ones((rows, rows))), BF16),
        e3b=jnp.asarray(np.tile(e_beta, (3, 1)), BF16),
        e3a=jnp.asarray(np.tile(e_alpha, (3, 1)), BF16),
        e3c=jnp.asarray(np.tile(e_c, (3, 1)), BF16),
        dmask=per_chunk(dmask), incl=per_chunk(incl), strict=f(strict), eye=f(dmask[:, :MXU_DIM]),
        bdmask=f(bdmask).astype(mask_dtype), kmask=f(kmask).astype(mask_dtype),
    )


def _gdn_kernel(qkv_ref, gate_ref, sm_ref, s0_ref, dtb_ref, a_ref, nw_ref,
                l3_ref, ones3_ref, e3b_ref, e3a_ref, e3c_ref, dmask_ref, incl_ref, strict_ref, eye_ref,
                bdmask_ref, kmask_ref,
                o_ref, sout_ref, s_ref, *, q, nseq, hpb, cps):
    c = pl.program_id(1)
    nc = pl.num_programs(1)
    nhb = GDN_HEADS // hpb
    rows = nseq * q
    bw = hpb * GDN_DIM
    n_sq = q.bit_length() - 1
    assert q == 1 << n_sq and n_sq >= 2

    @pl.when(c == 0)
    def _():
        s_ref[...] = s0_ref[...]

    bdmask = bdmask_ref[...]
    kmask = kmask_ref[...]
    mask_dtype = bdmask.dtype

    def head_lanes(h):
        return slice(h * GDN_DIM, (h + 1) * GDN_DIM)

    def seq_rows(sq):
        return slice(sq * q, (sq + 1) * q)

    def chunk_rows(ci):
        return slice(ci * rows, (ci + 1) * rows)

    def stack_rows(x):
        return jnp.concatenate([x[seq_rows(sq)] for sq in range(nseq) for _ in range(hpb)], axis=0)

    def masked_bf16(x, mask):
        return (x.astype(mask_dtype) * mask).astype(BF16)

    def block_diag(m_bf16):
        return masked_bf16(stack_rows(m_bf16.astype(mask_dtype)), bdmask)

    def block_diag_pieces(m):
        hi = m.astype(BF16)
        lo = (m - hi.astype(F32)).astype(BF16)
        return block_diag(hi), block_diag(lo)

    def lhs_pieces(m):
        hi = m.astype(BF16)
        return hi, (m - hi.astype(F32)).astype(BF16)

    sm = sm_ref[...]
    beta = jax.nn.sigmoid(sm)
    g = a_ref[...] * _softplus(sm + dtb_ref[...])
    gc = _dot_exact_rhs(l3_ref[...], g)
    gc3 = _split3_lanes(gc)
    beta_e = _dot(_split3_lanes(beta), e3b_ref[...])
    gc_e = _dot(gc3, e3a_ref[...])
    col_c = _dot(gc3, e3c_ref[...])
    row_c = _dot_exact_rhs(ones3_ref[...], col_c * dmask_ref[...])
    decay_c = jnp.exp(jnp.minimum(col_c - row_c, 0.0)) * incl_ref[...]
    egc_e = jnp.exp(gc_e)
    qn = qkv_ref[:, :GDN_WIDTH]
    kn = qkv_ref[:, GDN_WIDTH:2 * GDN_WIDTH]
    kb = kn * beta_e
    vb = qkv_ref[:, 2 * GDN_WIDTH:] * beta_e
    kbg = kb * egc_e
    qin = qn * egc_e

    def chunk_local(ci):
        r0 = ci * rows
        lasts = [gc_e[r0 + (sq + 1) * q - 1:r0 + (sq + 1) * q, :] for sq in range(nseq)]
        eouts = [jnp.exp(lasts[sq] - gc_e[r0 + sq * q:r0 + (sq + 1) * q]) for sq in range(nseq)]
        cr = chunk_rows(ci)
        return dict(
            qn=qn[cr], kn=kn[cr], kb=kb[cr], vb=vb[cr], kbg=kbg[cr], qin=qin[cr], decay_c=decay_c[cr],
            kout=(kn[cr] * (eouts[0] if nseq == 1 else jnp.concatenate(eouts, axis=0))).astype(BF16),
            ge=[jnp.exp(last) for last in lasts])

    def chain_init(loc, jb):
        nl = slice(jb * bw, (jb + 1) * bw)
        cl = slice(jb * MXU_DIM, (jb + 1) * MXU_DIM)
        k_bd = masked_bf16(stack_rows(loc["kn"][:, nl].astype(mask_dtype)), kmask)
        prod = _dot_nt(jnp.concatenate([loc["kb"][:, nl], loc["qn"][:, nl]], axis=0).astype(BF16), k_bd)
        dec = loc["decay_c"][:, cl]
        x = -(prod[:rows] * dec * strict_ref[:, cl])
        return dict(inv=eye_ref[...] + x, ypow=x, attn=(prod[rows:] * dec).astype(BF16))

    def double_step(st, k):
        first, last = k == 0, k == n_sq - 1
        inv, ypow = st["inv"], st["ypow"]
        b_hi, b_lo = block_diag_pieces(ypow)
        lhs = ypow if first else (inv if last else jnp.concatenate([inv, ypow], axis=0))
        l_hi, l_lo = lhs_pieces(lhs)
        res = _dot(jnp.concatenate([l_hi, l_hi, l_lo], axis=1), jnp.concatenate([b_hi, b_lo, b_hi], axis=0))
        if first:
            return dict(st, ypow=res)
        if last:
            return dict(st, inv=inv + res)
        return dict(st, inv=inv + res[:rows], ypow=res[rows:])

    def solve(loc, jb, inv):
        i_hi, i_lo = block_diag_pieces(inv)
        segs = [(sq, jb * hpb + hh) for sq in range(nseq) for hh in range(hpb)]
        rhs = jnp.concatenate([jnp.concatenate([loc["vb"][seq_rows(sq), head_lanes(h)],
                                                loc["kbg"][seq_rows(sq), head_lanes(h)]], axis=1)
                               for sq, h in segs], axis=0).astype(BF16)
        return _dot(jnp.concatenate([i_hi, i_lo], axis=1), jnp.concatenate([rhs, rhs], axis=0))

    def state_step(loc, jb, uw, attn):
        segs = [(sq, jb * hpb + hh) for sq in range(nseq) for hh in range(hpb)]
        ws, qs, states = [], [], []
        for i, (sq, h) in enumerate(segs):
            s = s_ref[sq, h]
            states.append(s)
            wq = jnp.concatenate([uw[i * q:(i + 1) * q, GDN_DIM:], loc["qin"][seq_rows(sq), head_lanes(h)]], axis=0)
            r = _dot(wq.astype(BF16), s.astype(BF16))
            ws.append(r[:q])
            qs.append(r[q:])
        v_bf = (uw[:, :GDN_DIM] - jnp.concatenate(ws, axis=0)).astype(BF16)
        v_rep = masked_bf16(jnp.concatenate([v_bf.astype(mask_dtype)] * hpb, axis=1), kmask)
        qs_rows = [jnp.concatenate(qs[sq * hpb:(sq + 1) * hpb], axis=1) for sq in range(nseq)]
        qs_nat = qs_rows[0] if nseq == 1 else jnp.concatenate(qs_rows, axis=0)
        for i, (sq, h) in enumerate(segs):
            s_ref[sq, h] = (states[i] * loc["ge"][sq][:, head_lanes(h)]
                            + _dot_tn(loc["kout"][seq_rows(sq), head_lanes(h)], v_bf[i * q:(i + 1) * q]))
        return qs_nat + _dot(attn, v_rep)

    local = [chunk_local(ci) for ci in range(cps)]
    chains = [(ci, jb) for ci in range(cps) for jb in range(nhb)]
    st = [chain_init(local[ci], jb) for ci, jb in chains]
    for k in range(n_sq):
        st = [double_step(s, k) for s in st]
    uws =[solve(local[ci], jb, s["inv"]) for (ci, jb), s in zip(chains, st)]
    nw = nw_ref[...]
    for ci in range(cps):
        o_blocks = [state_step(local[ci], jb, uws[ci * nhb + jb], st[ci * nhb + jb]["attn"]) for jb in range(nhb)]
        o = o_blocks[0] if nhb == 1 else jnp.concatenate(o_blocks, axis=1)
        gate = gate_ref[chunk_rows(ci)]
        outs = [_rms(o[:, head_lanes(h)], nw) * _silu(gate[:, head_lanes(h)]) for h in range(GDN_HEADS)]
        o_ref[chunk_rows(ci)] = jnp.concatenate(outs, axis=1).astype(o_ref.dtype)

    @pl.when(c == nc - 1)
    def _():
        sout_ref[...] = s_ref[...]


def _gdn_call(qkv, gate, small, s0, dtb, a_pad, nw, b, l, q):
    nseq = max(1, MXU_DIM // (GDN_HEADS * q))
    hpb = MXU_DIM // (nseq * q)
    cps = SCAN_UNITS_PER_STEP if nseq == 1 and (l // q) % SCAN_UNITS_PER_STEP == 0 else 1
    consts = _gdn_constants(q, nseq, hpb, cps)
    nc = l // (q * cps)
    assert b % nseq == 0 and (nseq == 1 or nc == 1)
    t = b * l
    rows = nseq * q * cps
    row = lambda i, c: (i * nc + c, 0)
    seq = lambda i, c: (i, 0, 0, 0)
    params = [dtb, a_pad, nw] + [consts[k] for k in
                                 ("l3", "ones3", "e3b", "e3a", "e3c", "dmask", "incl", "strict", "eye",
                                  "bdmask", "kmask")]
    s_shape = (GDN_HEADS, GDN_DIM, GDN_DIM)
    return pl.pallas_call(
        functools.partial(_gdn_kernel, q=q, nseq=nseq, hpb=hpb, cps=cps),
        grid=(b // nseq, nc),
        in_specs=[pl.BlockSpec((rows, GDN_CONV_DIM), row),
                  pl.BlockSpec((rows, GDN_WIDTH), row),
                  pl.BlockSpec((rows, LANES), row),
                  pl.BlockSpec((nseq,) + s_shape, seq)]
                 + [_const_spec(p.shape) for p in params],
        out_specs=[pl.BlockSpec((rows, GDN_WIDTH), row),
                   pl.BlockSpec((nseq,) + s_shape, seq)],
        out_shape=[jax.ShapeDtypeStruct((t, GDN_WIDTH), F32),
                   jax.ShapeDtypeStruct((b,) + s_shape, F32)],
        scratch_shapes=[pltpu.VMEM((nseq,) + s_shape, F32)],
        compiler_params=_params2(),
        name="gdn_scan",
    )(qkv, gate, small, s0, *params)


def _outffn_kernel(x_ref, y_ref, o_ref, mod_ref, nwm_ref, nwf_ref, nwp_ref,
                   wout_ref, wgu_ref, wdown_ref, out_ref, *, ff_chunk):
    x = x_ref[...]
    shape3 = x.shape
    tm = shape3[0] * shape3[1]
    mod = mod_ref[...]
    g1, sh2, sc2, g2 = (mod[:, k:k + 1, :] for k in range(2, 6))
    m = (_dot(y_ref[...].astype(BF16), wout_ref[:SSD_WIDTH, :])
         + _dot(o_ref[...].astype(BF16), wout_ref[SSD_WIDTH:, :]))
    x1 = x + g1 * _rms(m, nwm_ref[...]).reshape(shape3)
    h2 = _rms(x1, nwf_ref[...]) * (1.0 + sc2) + sh2
    h2 = h2.reshape(tm, shape3[2]).astype(BF16)
    f = jnp.zeros((tm, shape3[2]), F32)
    for j in range(D_FF // ff_chunk):
        gj = _dot(h2, wgu_ref[:, j * ff_chunk:(j + 1) * ff_chunk])
        uj = _dot(h2, wgu_ref[:, D_FF + j * ff_chunk:D_FF + (j + 1) * ff_chunk])
        f = f + _dot((_silu(gj) * uj).astype(BF16), wdown_ref[j * ff_chunk:(j + 1) * ff_chunk, :])
    out_ref[...] = x1 + g2 * _rms(f, nwp_ref[...]).reshape(shape3)


def _outffn_call(x, y, o, mod6, nw_mix_post, nw_ffn_pre, nw_ffn_post, w_out, w_gu, w_down, bb, lb):
    b, l, d = x.shape
    nl = l // lb
    tm = bb * lb
    row = lambda i, j: (i * nl + j, 0)
    return pl.pallas_call(
        functools.partial(_outffn_kernel, ff_chunk=D_FF),
        grid=(b // bb, nl),
        in_specs=[pl.BlockSpec((bb, lb, d), lambda i, j: (i, j, 0)),
                  pl.BlockSpec((tm, SSD_WIDTH), row),
                  pl.BlockSpec((tm, GDN_WIDTH), row),
                  pl.BlockSpec((bb,) + mod6.shape[1:], lambda i, j: (i, 0, 0)),
                  _const_spec((1, d)), _const_spec((1, d)), _const_spec((1, d)),
                  _const_spec(w_out.shape), _const_spec(w_gu.shape), _const_spec(w_down.shape)],
        out_specs=pl.BlockSpec((bb, lb, d), lambda i, j: (i, j, 0)),
        out_shape=jax.ShapeDtypeStruct((b, l, d), F32),
        compiler_params=_params2(),
        name="outproj_ffn",
    )(x, y, o, mod6, nw_mix_post, nw_ffn_pre, nw_ffn_post, w_out, w_gu, w_down)


_O_DT = SSD_WIDTH + SSD_CONV_DIM
_O_QKV = _O_DT + SSD_HEADS
_O_BETA = _O_QKV + GDN_CONV_DIM + GDN_WIDTH
_IN_DIM = _O_BETA + 2 * GDN_HEADS


def _repack_kernel(wt_ref, dt_ref, ba_ref, big_ref, small_ref):
    big_ref[...] = wt_ref[...].T.astype(BF16)

    @pl.when(pl.program_id(0) == 0)
    def _():
        k = dt_ref.shape[1]
        pad = jnp.zeros((LANES - dt_ref.shape[0] - ba_ref.shape[0], k), F32)
        small_ref[...] = jnp.concatenate([dt_ref[...], ba_ref[...], pad], axis=0).T.astype(BF16)


def _repack_w_in(w):
    k, n = w.shape
    tc = 2 * MXU_DIM
    assert n == _IN_DIM and _O_DT % tc == 0
    wt = w.T
    big = n - SSD_HEADS - 2 * GDN_HEADS
    skip = _O_QKV - _O_DT
    src_row = lambda j: (pl.multiple_of(jnp.where(j * tc < _O_DT, j * tc, j * tc + skip), skip), 0)
    window = lambda rows, index: pl.BlockSpec((pl.Element(rows), pl.Element(k)), index)
    return pl.pallas_call(
        _repack_kernel,
        grid=(big // tc,),
        in_specs=[window(tc, src_row),
                  window(SSD_HEADS, lambda j: (_O_DT, 0)),
                  window(2 * GDN_HEADS, lambda j: (_O_BETA, 0))],
        out_specs=[pl.BlockSpec((k, tc), lambda j: (0, j)), pl.BlockSpec((k, LANES), lambda j: (0, 0))],
        out_shape=[jax.ShapeDtypeStruct((k, big), BF16), jax.ShapeDtypeStruct((k, LANES), BF16)],
        compiler_params=_params2(("arbitrary",)),
        name="repack_w_in",
    )(wt, wt, wt)


def _pad_lanes(v, lane0):
    out = jnp.zeros((1, LANES), F32)
    return out.at[0, lane0:lane0 + v.shape[0]].set(v.astype(F32))


def _layer(x, mod, ssd_h0, ssd_conv0, gdn_s0, gdn_conv0, p, q, bb, lb):
    b, l, d = x.shape
    mod6 = mod.reshape(b, 6, d)
    z, xbc, qkv, gate, small, ssd_conv, gdn_conv = _inproj_call(
        x, mod6, p["norm_mix_pre"], p["w_big"], p["w_small"], ssd_conv0, gdn_conv0,
        p["ssd_conv_w"], p["ssd_conv_b"], p["gdn_conv_w"], bb, lb)
    y, ssd_h = _ssd_call(xbc, z, small, ssd_h0.reshape(b, SSD_WIDTH, SSD_STATE),
                         p["ssd_dtb"], p["ssd_a"], p["ssd_d"], p["ssd_norm_w"], b, l, q)
    o, gdn_s = _gdn_call(qkv, gate, small, gdn_s0, p["gdn_dtb"], p["gdn_a"], p["gdn_norm_w"], b, l, q)
    out = _outffn_call(x, y, o, mod6, p["norm_mix_post"], p["norm_ffn_pre"], p["norm_ffn_post"],
                       p["w_out"], p["w_gu"], p["w_down"], bb, lb)
    return out, ssd_h.reshape(ssd_h0.shape), ssd_conv, gdn_s, gdn_conv


def _layer_params(l, w_ada, b_ada, norm_mix_pre, norm_mix_post, norm_ffn_pre, norm_ffn_post, w_in, ssd_conv_w,
                  ssd_conv_b, ssd_dt_bias, ssd_A_log, ssd_D, ssd_norm_w, gdn_conv_w, gdn_dt_bias, gdn_A_log,
                  gdn_norm_w, w_out, w_gate_up, w_down):
    w_big, w_small = _repack_w_in(w_in[l])
    row = lambda v: v.reshape(1, -1).astype(F32)
    return dict(
        w_ada=w_ada[l], b_ada=b_ada[l],
        norm_mix_pre=row(norm_mix_pre[l]), norm_mix_post=row(norm_mix_post[l]),
        norm_ffn_pre=row(norm_ffn_pre[l]), norm_ffn_post=row(norm_ffn_post[l]),
        w_big=w_big, w_small=w_small,
        ssd_conv_w=ssd_conv_w[l], ssd_conv_b=row(ssd_conv_b[l]),
        ssd_dtb=_pad_lanes(ssd_dt_bias[l], DT_LANE0),
        ssd_a=_pad_lanes(-jnp.exp(ssd_A_log[l].astype(F32)), DT_LANE0),
        ssd_d=row(jnp.repeat(ssd_D[l], SSD_HEADDIM)),
        ssd_norm_w=row(ssd_norm_w[l]),
        gdn_conv_w=gdn_conv_w[l],
        gdn_dtb=_pad_lanes(gdn_dt_bias[l], ALPHA_LANE0),
        gdn_a=_pad_lanes(-jnp.exp(gdn_A_log[l].astype(F32)), ALPHA_LANE0),
        gdn_norm_w=row(gdn_norm_w[l]),
        w_out=w_out[l].astype(BF16), w_gu=w_gate_up[l].astype(BF16), w_down=w_down[l].astype(BF16),
    )


def kernel(x_prompt, x_sample, c_prompt, c_sample, state_ssd, state_ssd_conv, state_gdn, state_gdn_conv, w_ada, b_ada, norm_mix_pre, norm_mix_post, norm_ffn_pre, norm_ffn_post, w_in, ssd_conv_w, ssd_conv_b, ssd_dt_bias, ssd_A_log, ssd_D, ssd_norm_w, gdn_conv_w, gdn_dt_bias, gdn_A_log, gdn_norm_w, w_out, w_gate_up, w_down):
    depth = w_in.shape[0]
    bp, lp, _ = x_prompt.shape
    bs, ls, _ = x_sample.shape
    yp, ys = x_prompt, x_sample
    outs = [[] for _ in range(8)]
    for l in range(depth):
        p = _layer_params(l, w_ada, b_ada, norm_mix_pre, norm_mix_post, norm_ffn_pre, norm_ffn_post, w_in,
                          ssd_conv_w, ssd_conv_b, ssd_dt_bias, ssd_A_log, ssd_D, ssd_norm_w, gdn_conv_w,
                          gdn_dt_bias, gdn_A_log, gdn_norm_w, w_out, w_gate_up, w_down)
        mod = _mod_call(jnp.concatenate([c_prompt, c_sample], axis=0), p["w_ada"], p["b_ada"])
        zeros = lambda a: jnp.zeros((bp,) + a.shape[2:], a.dtype)
        yp, a0, a1, a2, a3 = _layer(yp, mod[:bp], zeros(state_ssd), zeros(state_ssd_conv), zeros(state_gdn),
                                    zeros(state_gdn_conv), p, q=min(CHUNK, lp), bb=1, lb=min(ROW_TILE, lp))
        ys, b0, b1, b2, b3 = _layer(ys, mod[bp:], state_ssd[l], state_ssd_conv[l], state_gdn[l],
                                    state_gdn_conv[l], p, q=min(CHUNK, ls), bb=min(ROW_TILE // ls, bs), lb=ls)
        for lst, v in zip(outs, (a0, a1, a2, a3, b0, b1, b2, b3)):
            lst.append(v)
    return (yp, ys) + tuple(v[0][None] if depth == 1 else jnp.stack(v) for v in outs)
```

```python
import functools

import numpy as np
import jax
import jax.numpy as jnp
from jax import lax
from jax.experimental import pallas as pl
from jax.experimental.pallas import tpu as pltpu

F32 = jnp.float32
BF16 = jnp.bfloat16

SSD_HEADS = 16
SSD_HEADDIM = 64
SSD_GROUPS = 2
SSD_STATE = 128
SSD_WIDTH = SSD_HEADS * SSD_HEADDIM
SSD_CONV_DIM = SSD_WIDTH + 2 * SSD_GROUPS * SSD_STATE
GDN_HEADS = 8
GDN_DIM = 128
GDN_WIDTH = GDN_HEADS * GDN_DIM
GDN_CONV_DIM = 3 * GDN_WIDTH
CONV_WIDTH = 4
CHUNK = 64
D_FF = 2816
NORM_EPS = 1e-6
L2_EPS = 1e-6
LANES = 128
SUBLANES = 8
MXU_DIM = 256
DT_LANE0, BETA_LANE0, ALPHA_LANE0 = 0, SSD_HEADS, SSD_HEADS + GDN_HEADS
VMEM_LIMIT = 56 * 1024 * 1024
ROW_TILE = 256
SCAN_UNITS_PER_STEP = 4
CONV_COLS = 512
ADA_COLS = 1536


def _dot(a, b):
    return jnp.dot(a, b, preferred_element_type=F32)


def _dot_nt(a, b):
    return lax.dot_general(a, b, (((1,), (1,)), ((), ())), preferred_element_type=F32)


def _dot_tn(a, b):
    return lax.dot_general(a, b, (((0,), (0,)), ((), ())), preferred_element_type=F32)


def _bf16_pieces(x):
    hi = x.astype(BF16)
    r1 = x - hi.astype(F32)
    mid = r1.astype(BF16)
    lo = (r1 - mid.astype(F32)).astype(BF16)
    return hi, mid, lo


def _split3_lanes(x):
    hi, mid, lo = _bf16_pieces(x)
    return jnp.concatenate([hi.astype(F32), mid.astype(F32), lo.astype(F32)], axis=1).astype(BF16)


def _dot_exact_rhs(a_bf16, x):
    hi, mid, lo = _bf16_pieces(x)
    return _dot(a_bf16, hi) + _dot(a_bf16, mid) + _dot(a_bf16, lo)


def _silu(x):
    return x * jax.nn.sigmoid(x)


def _softplus(x):
    return jnp.maximum(x, 0.0) + jnp.log1p(jnp.exp(-jnp.abs(x)))


def _rms(x, w):
    return x * lax.rsqrt(jnp.mean(x * x, axis=-1, keepdims=True) + NORM_EPS) * w


def _const_spec(shape):
    nd = len(shape)
    return pl.BlockSpec(shape, lambda *_: (0,) * nd, pipeline_mode=pl.Buffered(1))


def _params2(sem=("arbitrary", "arbitrary")):
    return pltpu.CompilerParams(dimension_semantics=sem, vmem_limit_bytes=VMEM_LIMIT)


def _mod_kernel(c_ref, w_ref, b_ref, o_ref):
    a = _silu(c_ref[...]).astype(BF16)
    o_ref[...] = _dot(a, w_ref[...].astype(BF16)) + b_ref[...]


def _mod_call(c_all, w_ada, b_ada):
    m, d = c_all.shape
    n = w_ada.shape[1]
    tn = ADA_COLS
    assert n % tn == 0
    return pl.pallas_call(
        _mod_kernel,
        grid=(n // tn,),
        in_specs=[pl.BlockSpec((m, d), lambda j: (0, 0)),
                  pl.BlockSpec((d, tn), lambda j: (0, j)),
                  pl.BlockSpec((1, tn), lambda j: (0, j))],
        out_specs=pl.BlockSpec((m, tn), lambda j: (0, j)),
        out_shape=jax.ShapeDtypeStruct((m, n), F32),
        compiler_params=_params2(("arbitrary",)),
        name="adaln_mod",
    )(c_all, w_ada, b_ada.reshape(1, n))


def _proj_conv_silu(h, w_ref, w_col0, width, bb, l, tail_ref, convout_ref, cw_ref, cb_ref, out_ref, n_norm, norm_scale):
    tm = bb * l
    for k in range(width // CONV_COLS):
        cols = slice(k * CONV_COLS, (k + 1) * CONV_COLS)
        raw = _dot(h, w_ref[:, w_col0 + k * CONV_COLS:w_col0 + (k + 1) * CONV_COLS]).reshape(bb, l, CONV_COLS)
        ext = jnp.concatenate([tail_ref[:, :, cols], raw], axis=1)
        acc = ext[:, SUBLANES:SUBLANES + l] * cw_ref[CONV_WIDTH - 1:CONV_WIDTH, cols]
        for j in range(1, CONV_WIDTH):
            acc = acc + ext[:, SUBLANES - j:SUBLANES - j + l] * cw_ref[CONV_WIDTH - 1 - j:CONV_WIDTH - j, cols]
        tail_ref[:, :, cols] = ext[:, l:l + SUBLANES]
        convout_ref[:, :, cols] = ext[:, l + SUBLANES - (CONV_WIDTH - 1):l + SUBLANES]
        if cb_ref is not None:
            acc = acc + cb_ref[:, cols]
        act = _silu(acc).reshape(tm, CONV_COLS)
        for hh in range(CONV_COLS // LANES):
            head = k * (CONV_COLS // LANES) + hh
            xh = act[:, hh * LANES:(hh + 1) * LANES]
            if head < n_norm:
                xh = xh * (lax.rsqrt(jnp.sum(xh * xh, axis=-1, keepdims=True) + L2_EPS) * norm_scale[head])
            out_ref[:, head * LANES:(head + 1) * LANES] = xh


def _inproj_kernel(x_ref, mod_ref, nw_ref, wbig_ref, wsm_ref, sconv0_ref, gconv0_ref,
                   scw_ref, scb_ref, gcw_ref,
                   z_ref, xbc_ref, qkv_ref, gate_ref, sm_ref, sconv_ref, gconv_ref, stail_ref, gtail_ref):
    @pl.when(pl.program_id(1) == 0)
    def _():
        for tail_ref, conv0_ref in ((stail_ref, sconv0_ref), (gtail_ref, gconv0_ref)):
            tail_ref[...] = jnp.zeros(tail_ref.shape, F32)
            tail_ref[:, SUBLANES - (CONV_WIDTH - 1):, :] = conv0_ref[...]

    x = x_ref[...]
    bb, l, d = x.shape
    tm = bb * l
    mod = mod_ref[...]
    h = _rms(x, nw_ref[...]) * (1.0 + mod[:, 1:2, :]) + mod[:, 0:1, :]
    h = h.reshape(tm, d).astype(BF16)
    c_xbc = SSD_WIDTH
    c_qkv = c_xbc + SSD_CONV_DIM
    c_gate = c_qkv + GDN_CONV_DIM
    z_ref[...] = _dot(h, wbig_ref[:, :c_xbc])
    gate_ref[...] = _dot(h, wbig_ref[:, c_gate:])
    sm_ref[...] = _dot(h, wsm_ref[...])
    _proj_conv_silu(h, wbig_ref, c_xbc, SSD_CONV_DIM, bb, l, stail_ref, sconv_ref, scw_ref, scb_ref, xbc_ref, 0, ())
    qk_scale = (GDN_DIM ** -0.5,) * GDN_HEADS + (1.0,) * GDN_HEADS
    _proj_conv_silu(h, wbig_ref, c_qkv, GDN_CONV_DIM, bb, l, gtail_ref, gconv_ref, gcw_ref, None, qkv_ref,
                    2 * GDN_HEADS, qk_scale)


def _inproj_call(x, mod6, norm_w, w_big, w_small, sconv0, gconv0, scw, scb, gcw, bb, lb):
    b, l, d = x.shape
    nl = l // lb
    tm = bb * lb
    t = b * l
    row = lambda i, j: (i * nl + j, 0)
    seq = lambda i, j: (i, 0, 0)
    widths = (SSD_WIDTH, SSD_CONV_DIM, GDN_CONV_DIM, GDN_WIDTH, LANES)
    hist = CONV_WIDTH - 1
    return pl.pallas_call(
        _inproj_kernel,
        grid=(b // bb, nl),
        in_specs=[pl.BlockSpec((bb, lb, d), lambda i, j: (i, j, 0)),
                  pl.BlockSpec((bb,) + mod6.shape[1:], lambda i, j: (i, 0, 0)),
                  _const_spec((1, d)),
                  _const_spec(w_big.shape),
                  _const_spec(w_small.shape),
                  pl.BlockSpec((bb, hist, SSD_CONV_DIM), seq),
                  pl.BlockSpec((bb, hist, GDN_CONV_DIM), seq),
                  _const_spec(scw.shape), _const_spec(scb.shape), _const_spec(gcw.shape)],
        out_specs=[pl.BlockSpec((tm, w), row) for w in widths]
                  + [pl.BlockSpec((bb, hist, SSD_CONV_DIM), seq), pl.BlockSpec((bb, hist, GDN_CONV_DIM), seq)],
        out_shape=[jax.ShapeDtypeStruct((t, w), F32) for w in widths]
                  + [jax.ShapeDtypeStruct((b, hist, SSD_CONV_DIM), F32),
                     jax.ShapeDtypeStruct((b, hist, GDN_CONV_DIM), F32)],
        scratch_shapes=[pltpu.VMEM((bb, SUBLANES, SSD_CONV_DIM), F32),
                        pltpu.VMEM((bb, SUBLANES, GDN_CONV_DIM), F32)],
        compiler_params=_params2(),
        name="inproj",
    )(x, mod6, norm_w, w_big, w_small, sconv0, gconv0, scw, scb, gcw)


def _ssd_constants(q, nu):
    hq = SSD_HEADS * q
    hp = LANES // q
    w = hp * SSD_HEADDIM
    e_p = np.zeros((LANES, SSD_WIDTH), np.float32)
    e_q = np.zeros((LANES, hq), np.float32)
    for r in range(SSD_HEADS):
        e_p[DT_LANE0 + r, r * SSD_HEADDIM:(r + 1) * SSD_HEADDIM] = 1.0
        e_q[DT_LANE0 + r, r * q:(r + 1) * q] = 1.0
    tri = np.tril(np.ones((q, q), np.float32))
    gmask = np.zeros((hq, 2 * SSD_STATE), np.float32)
    half = hq // SSD_GROUPS
    gmask[:half, :SSD_STATE] = 1.0
    gmask[half:, SSD_STATE:] = 1.0
    xmask = np.kron(np.eye(hp, dtype=np.float32), np.ones((q, SSD_HEADDIM), np.float32))
    unit_eye = np.eye(nu, dtype=np.float32)
    return dict(
        l3=jnp.asarray(np.kron(unit_eye, tri), BF16),
        ones3=jnp.asarray(np.kron(unit_eye, np.ones((q, q), np.float32)), BF16),
        e3p=jnp.asarray(np.tile(e_p, (3, 1)), BF16),
        e3q=jnp.asarray(np.tile(e_q, (3, 1)), BF16),
        dmask=jnp.asarray(np.tile(np.eye(q, dtype=np.float32), (nu, SSD_HEADS))),
        causal=jnp.asarray(np.tile(tri, (nu, SSD_HEADS))),
        gmask=jnp.asarray(gmask),
        xmask=jnp.asarray(xmask),
    ), hp, w


def _ssd_kernel(xbc_ref, z_ref, sm_ref, h0_ref, dtb_ref, a_ref, d_ref, nw_ref,
                l3_ref, ones3_ref, e3p_ref, e3q_ref, dmask_ref, causal_ref, gmask_ref, xmask_ref,
                y_ref, hout_ref, ht_ref, *, q, hp, w, nseq, cps):
    c = pl.program_id(1)
    nc = pl.num_programs(1)
    nu = nseq * cps
    rows = nu * q
    gw = SSD_WIDTH // SSD_GROUPS

    @pl.when(c == 0)
    def _():
        for sq in range(nseq):
            ht_ref[sq] = h0_ref[sq].T

    def unit_rows(u):
        return slice(u * q, (u + 1) * q)

    xs = xbc_ref[:, :SSD_WIDTH]
    bcat = xbc_ref[:, SSD_WIDTH:SSD_WIDTH + 2 * SSD_STATE]
    ccat = xbc_ref[:, SSD_WIDTH + 2 * SSD_STATE:].astype(BF16)

    dt = _softplus(sm_ref[...] + dtb_ref[...])
    acs = _dot_exact_rhs(l3_ref[...], dt * a_ref[...])
    dec_ends = [jnp.exp(acs[(u + 1) * q - 1:(u + 1) * q, :] - acs[unit_rows(u)]) for u in range(nu)]
    dec_end = dec_ends[0] if nu == 1 else jnp.concatenate(dec_ends, axis=0)
    per_head = jnp.concatenate([dt, dt * dec_end, jnp.exp(acs)], axis=0)
    exp3 = _dot(_split3_lanes(per_head), e3p_ref[...])
    dt_e, dtd_e, eacs_e = exp3[:rows], exp3[rows:2 * rows], exp3[2 * rows:]

    col_all = _dot(_split3_lanes(acs), e3q_ref[...])
    row_all = _dot_exact_rhs(ones3_ref[...], col_all * dmask_ref[...])
    decay = jnp.exp(jnp.minimum(col_all - row_all, 0.0)) * causal_ref[...]
    xdt = xs * dt_e
    xd = (xs * dtd_e).astype(BF16)
    dxs = d_ref[...] * xs
    gmask = gmask_ref[...]
    xmask = xmask_ref[...]

    def cb_all(u):
        if 2 * q == LANES:
            per_group = []
            for g in range(SSD_GROUPS):
                gl = slice(g * SSD_STATE, (g + 1) * SSD_STATE)
                cb = _dot_nt(ccat[unit_rows(u), gl], bcat[unit_rows(u), gl].astype(BF16))
                per_group += [jnp.concatenate([cb, cb], axis=1)] * (SSD_HEADS // SSD_GROUPS // 2)
            return jnp.concatenate(per_group, axis=1)
        brep = (jnp.concatenate([bcat[unit_rows(u)]] * SSD_HEADS, axis=0) * gmask).astype(BF16)
        return _dot_nt(ccat[unit_rows(u)], brep)

    cbs = [cb_all(u) for u in range(nu)]
    m_alls = [(cbs[u] * decay[unit_rows(u)]).astype(BF16) for u in range(nu)]
    y_diags, upds = [], []
    for u in range(nu):
        blocks = []
        for j in range(SSD_WIDTH // w):
            xj = (jnp.concatenate([xdt[unit_rows(u), j * w:(j + 1) * w]] * hp, axis=0) * xmask).astype(BF16)
            blocks.append(_dot(m_alls[u][:, j * LANES:(j + 1) * LANES], xj))
        y_diags.append(blocks[0] if len(blocks) == 1 else jnp.concatenate(blocks, axis=1))
        upds.append(jnp.concatenate(
            [_dot_tn(bcat[unit_rows(u), g * SSD_STATE:(g + 1) * SSD_STATE].astype(BF16),
                     xd[unit_rows(u), g * gw:(g + 1) * gw]) for g in range(SSD_GROUPS)], axis=1))

    nw = nw_ref[...]
    for sq in range(nseq):
        ht = ht_ref[sq]
        for ci in range(cps):
            u = sq * cps + ci
            ht_bf = ht.astype(BF16)
            y_off = jnp.concatenate([_dot(ccat[unit_rows(u), g * SSD_STATE:(g + 1) * SSD_STATE],
                                          ht_bf[:, g * gw:(g + 1) * gw]) for g in range(SSD_GROUPS)], axis=1)
            e_u = eacs_e[unit_rows(u)]
            y = y_diags[u] + y_off * e_u + dxs[unit_rows(u)]
            ht = ht * e_u[q - 1:q, :] + upds[u]
            yz = y * _silu(z_ref[unit_rows(u)])
            outs = [_rms(yz[:, g * gw:(g + 1) * gw], nw[:, g * gw:(g + 1) * gw]) for g in range(SSD_GROUPS)]
            y_ref[unit_rows(u)] = jnp.concatenate(outs, axis=1).astype(y_ref.dtype)
        ht_ref[sq] = ht

    @pl.when(c == nc - 1)
    def _():
        for sq in range(nseq):
            hout_ref[sq] = ht_ref[sq].T


def _scan_units(b, l, q):
    nchunks = l // q
    cps = SCAN_UNITS_PER_STEP if nchunks % SCAN_UNITS_PER_STEP == 0 else 1
    nseq = SCAN_UNITS_PER_STEP if nchunks == 1 and b % SCAN_UNITS_PER_STEP == 0 else 1
    return nseq, cps


def _ssd_call(xbc, z, small, h0, dtb, a_pad, d_e, nw, b, l, q):
    nseq, cps = _scan_units(b, l, q)
    consts, hp, w = _ssd_constants(q, nseq * cps)
    nc = l // (q * cps)
    t = b * l
    rows = nseq * cps * q
    row = lambda i, c: (i * nc + c, 0)
    seq = lambda i, c: (i, 0, 0)
    params = [dtb, a_pad, d_e, nw] + [consts[k] for k in
                                      ("l3", "ones3", "e3p", "e3q", "dmask", "causal", "gmask", "xmask")]
    hp_shape = (SSD_WIDTH, SSD_STATE)
    return pl.pallas_call(
        functools.partial(_ssd_kernel, q=q, hp=hp, w=w, nseq=nseq, cps=cps),
        grid=(b // nseq, nc),
        in_specs=[pl.BlockSpec((rows, SSD_CONV_DIM), row),
                  pl.BlockSpec((rows, SSD_WIDTH), row),
                  pl.BlockSpec((rows, LANES), row),
                  pl.BlockSpec((nseq,) + hp_shape, seq)]
                 + [_const_spec(p.shape) for p in params],
        out_specs=[pl.BlockSpec((rows, SSD_WIDTH), row),
                   pl.BlockSpec((nseq,) + hp_shape, seq)],
        out_shape=[jax.ShapeDtypeStruct((t, SSD_WIDTH), F32),
                   jax.ShapeDtypeStruct((b,) + hp_shape, F32)],
        scratch_shapes=[pltpu.VMEM((nseq, SSD_STATE, SSD_WIDTH), F32)],
        compiler_params=_params2(),
        name="ssd_scan",
    )(xbc, z, small, h0, *params)


def _gdn_constants(q, nseq, hpb, cps):
    assert nseq * hpb * q == MXU_DIM
    nhb = GDN_HEADS // hpb
    rows = nseq * q
    e_beta = np.zeros((LANES, GDN_WIDTH), np.float32)
    e_alpha = np.zeros((LANES, GDN_WIDTH), np.float32)
    e_c = np.zeros((LANES, nhb * MXU_DIM), np.float32)
    for h in range(GDN_HEADS):
        e_beta[BETA_LANE0 + h, h * GDN_DIM:(h + 1) * GDN_DIM] = 1.0
        e_alpha[ALPHA_LANE0 + h, h * GDN_DIM:(h + 1) * GDN_DIM] = 1.0
    jh, seq_c, hh_c, s_c = np.unravel_index(np.arange(nhb * MXU_DIM), (nhb, nseq, hpb, q))
    e_c[ALPHA_LANE0 + jh * hpb + hh_c, np.arange(nhb * MXU_DIM)] = 1.0
    seq_r, l_r = np.unravel_index(np.arange(rows), (nseq, q))
    same_seq = seq_r[:, None] == seq_c[None, :]
    dmask = same_seq & (l_r[:, None] == s_c[None, :])
    incl = same_seq & (l_r[:, None] >= s_c[None, :])
    strict = same_seq & (l_r[:, None] > s_c[None, :])
    seq_s, hh_s, _ = np.unravel_index(np.arange(MXU_DIM), (nseq, hpb, q))
    bdmask = (seq_s[:, None] == seq_c[None, :MXU_DIM]) & (hh_s[:, None] == hh_c[None, :MXU_DIM])
    kmask = hh_s[:, None] == (np.arange(hpb * GDN_DIM) // GDN_DIM)[None, :]
    f = lambda m: jnp.asarray(m.astype(np.float32))
    per_chunk = lambda m: jnp.asarray(np.tile(m.astype(np.float32), (cps, 1)))
    chunk_eye = np.eye(cps)
    mask_dtype = BF16 if q % (2 * SUBLANES) == 0 else F32
    return dict(
        l3=jnp.asarray(np.kron(np.eye(cps * nseq), np.tril(np.ones((q, q)))), BF16),
        ones3=jnp.asarray(np.kron(chunk_eye, np.ones((rows, rows))), BF16),
        e3b=jnp.asarray(np.tile(e_beta, (3, 1)), BF16),
        e3a=jnp.asarray(np.tile(e_alpha, (3, 1)), BF16),
        e3c=jnp.asarray(np.tile(e_c, (3, 1)), BF16),
        dmask=per_chunk(dmask), incl=per_chunk(incl), strict=f(strict), eye=f(dmask[:, :MXU_DIM]),
        bdmask=f(bdmask).astype(mask_dtype), kmask=f(kmask).astype(mask_dtype),
    )


def _gdn_kernel(qkv_ref, gate_ref, sm_ref, s0_ref, dtb_ref, a_ref, nw_ref,
                l3_ref, ones3_ref, e3b_ref, e3a_ref, e3c_ref, dmask_ref, incl_ref, strict_ref, eye_ref,
                bdmask_ref, kmask_ref,
                o_ref, sout_ref, s_ref, *, q, nseq, hpb, cps):
    c = pl.program_id(1)
    nc = pl.num_programs(1)
    nhb = GDN_HEADS // hpb
    rows = nseq * q
    bw = hpb * GDN_DIM
    n_sq = q.bit_length() - 1
    assert q == 1 << n_sq and n_sq >= 2

    @pl.when(c == 0)
    def _():
        s_ref[...] = s0_ref[...]

    bdmask = bdmask_ref[...]
    kmask = kmask_ref[...]
    mask_dtype = bdmask.dtype

    def head_lanes(h):
        return slice(h * GDN_DIM, (h + 1) * GDN_DIM)

    def seq_rows(sq):
        return slice(sq * q, (sq + 1) * q)

    def chunk_rows(ci):
        return slice(ci * rows, (ci + 1) * rows)

    def stack_rows(x):
        return jnp.concatenate([x[seq_rows(sq)] for sq in range(nseq) for _ in range(hpb)], axis=0)

    def masked_bf16(x, mask):
        return (x.astype(mask_dtype) * mask).astype(BF16)

    def block_diag(m_bf16):
        return masked_bf16(stack_rows(m_bf16.astype(mask_dtype)), bdmask)

    def block_diag_pieces(m):
        hi = m.astype(BF16)
        lo = (m - hi.astype(F32)).astype(BF16)
        return block_diag(hi), block_diag(lo)

    def lhs_pieces(m):
        hi = m.astype(BF16)
        return hi, (m - hi.astype(F32)).astype(BF16)

    sm = sm_ref[...]
    beta = jax.nn.sigmoid(sm)
    g = a_ref[...] * _softplus(sm + dtb_ref[...])
    gc = _dot_exact_rhs(l3_ref[...], g)
    gc3 = _split3_lanes(gc)
    beta_e = _dot(_split3_lanes(beta), e3b_ref[...])
    gc_e = _dot(gc3, e3a_ref[...])
    col_c = _dot(gc3, e3c_ref[...])
    row_c = _dot_exact_rhs(ones3_ref[...], col_c * dmask_ref[...])
    decay_c = jnp.exp(jnp.minimum(col_c - row_c, 0.0)) * incl_ref[...]
    egc_e = jnp.exp(gc_e)
    qn = qkv_ref[:, :GDN_WIDTH]
    kn = qkv_ref[:, GDN_WIDTH:2 * GDN_WIDTH]
    kb = kn * beta_e
    vb = qkv_ref[:, 2 * GDN_WIDTH:] * beta_e
    kbg = kb * egc_e
    qin = qn * egc_e

    def chunk_local(ci):
        r0 = ci * rows
        lasts = [gc_e[r0 + (sq + 1) * q - 1:r0 + (sq + 1) * q, :] for sq in range(nseq)]
        eouts = [jnp.exp(lasts[sq] - gc_e[r0 + sq * q:r0 + (sq + 1) * q]) for sq in range(nseq)]
        cr = chunk_rows(ci)
        return dict(
            qn=qn[cr], kn=kn[cr], kb=kb[cr], vb=vb[cr], kbg=kbg[cr], qin=qin[cr], decay_c=decay_c[cr],
            kout=(kn[cr] * (eouts[0] if nseq == 1 else jnp.concatenate(eouts, axis=0))).astype(BF16),
            ge=[jnp.exp(last) for last in lasts])

    def chain_init(loc, jb):
        nl = slice(jb * bw, (jb + 1) * bw)
        cl = slice(jb * MXU_DIM, (jb + 1) * MXU_DIM)
        k_bd = masked_bf16(stack_rows(loc["kn"][:, nl].astype(mask_dtype)), kmask)
        prod = _dot_nt(jnp.concatenate([loc["kb"][:, nl], loc["qn"][:, nl]], axis=0).astype(BF16), k_bd)
        dec = loc["decay_c"][:, cl]
        x = -(prod[:rows] * dec * strict_ref[:, cl])
        return dict(inv=eye_ref[...] + x, ypow=x, attn=(prod[rows:] * dec).astype(BF16))

    def double_step(st, k):
        first, last = k == 0, k == n_sq - 1
        inv, ypow = st["inv"], st["ypow"]
        b_hi, b_lo = block_diag_pieces(ypow)
        lhs = ypow if first else (inv if last else jnp.concatenate([inv, ypow], axis=0))
        l_hi, l_lo = lhs_pieces(lhs)
        res = _dot(jnp.concatenate([l_hi, l_hi, l_lo], axis=1), jnp.concatenate([b_hi, b_lo, b_hi], axis=0))
        if first:
            return dict(st, ypow=res)
        if last:
            return dict(st, inv=inv + res)
        return dict(st, inv=inv + res[:rows], ypow=res[rows:])

    def solve(loc, jb, inv):
        i_hi, i_lo = block_diag_pieces(inv)
        segs = [(sq, jb * hpb + hh) for sq in range(nseq) for hh in range(hpb)]
        rhs = jnp.concatenate([jnp.concatenate([loc["vb"][seq_rows(sq), head_lanes(h)],
                                                loc["kbg"][seq_rows(sq), head_lanes(h)]], axis=1)
                               for sq, h in segs], axis=0).astype(BF16)
        return _dot(jnp.concatenate([i_hi, i_lo], axis=1), jnp.concatenate([rhs, rhs], axis=0))

    def state_step(loc, jb, uw, attn):
        segs = [(sq, jb * hpb + hh) for sq in range(nseq) for hh in range(hpb)]
        ws, qs, states = [], [], []
        for i, (sq, h) in enumerate(segs):
            s = s_ref[sq, h]
            states.append(s)
            wq = jnp.concatenate([uw[i * q:(i + 1) * q, GDN_DIM:], loc["qin"][seq_rows(sq), head_lanes(h)]], axis=0)
            r = _dot(wq.astype(BF16), s.astype(BF16))
            ws.append(r[:q])
            qs.append(r[q:])
        v_bf = (uw[:, :GDN_DIM] - jnp.concatenate(ws, axis=0)).astype(BF16)
        v_rep = masked_bf16(jnp.concatenate([v_bf.astype(mask_dtype)] * hpb, axis=1), kmask)
        qs_rows = [jnp.concatenate(qs[sq * hpb:(sq + 1) * hpb], axis=1) for sq in range(nseq)]
        qs_nat = qs_rows[0] if nseq == 1 else jnp.concatenate(qs_rows, axis=0)
        for i, (sq, h) in enumerate(segs):
            s_ref[sq, h] = (states[i] * loc["ge"][sq][:, head_lanes(h)]
                            + _dot_tn(loc["kout"][seq_rows(sq), head_lanes(h)], v_bf[i * q:(i + 1) * q]))
        return qs_nat + _dot(attn, v_rep)

    local = [chunk_local(ci) for ci in range(cps)]
    chains = [(ci, jb) for ci in range(cps) for jb in range(nhb)]
    st = [chain_init(local[ci], jb) for ci, jb in chains]
    for k in range(n_sq):
        st = [double_step(s, k) for s in st]
    uws =[solve(local[ci], jb, s["inv"]) for (ci, jb), s in zip(chains, st)]
    nw = nw_ref[...]
    for ci in range(cps):
        o_blocks = [state_step(local[ci], jb, uws[ci * nhb + jb], st[ci * nhb + jb]["attn"]) for jb in range(nhb)]
        o = o_blocks[0] if nhb == 1 else jnp.concatenate(o_blocks, axis=1)
        gate = gate_ref[chunk_rows(ci)]
        outs = [_rms(o[:, head_lanes(h)], nw) * _silu(gate[:, head_lanes(h)]) for h in range(GDN_HEADS)]
        o_ref[chunk_rows(ci)] = jnp.concatenate(outs, axis=1).astype(o_ref.dtype)

    @pl.when(c == nc - 1)
    def _():
        sout_ref[...] = s_ref[...]


def _gdn_call(qkv, gate, small, s0, dtb, a_pad, nw, b, l, q):
    nseq = max(1, MXU_DIM // (GDN_HEADS * q))
    hpb = MXU_DIM // (nseq * q)
    cps = SCAN_UNITS_PER_STEP if nseq == 1 and (l // q) % SCAN_UNITS_PER_STEP == 0 else 1
    consts = _gdn_constants(q, nseq, hpb, cps)
    nc = l // (q * cps)
    assert b % nseq == 0 and (nseq == 1 or nc == 1)
    t = b * l
    rows = nseq * q * cps
    row = lambda i, c: (i * nc + c, 0)
    seq = lambda i, c: (i, 0, 0, 0)
    params = [dtb, a_pad, nw] + [consts[k] for k in
                                 ("l3", "ones3", "e3b", "e3a", "e3c", "dmask", "incl", "strict", "eye",
                                  "bdmask", "kmask")]
    s_shape = (GDN_HEADS, GDN_DIM, GDN_DIM)
    return pl.pallas_call(
        functools.partial(_gdn_kernel, q=q, nseq=nseq, hpb=hpb, cps=cps),
        grid=(b // nseq, nc),
        in_specs=[pl.BlockSpec((rows, GDN_CONV_DIM), row),
                  pl.BlockSpec((rows, GDN_WIDTH), row),
                  pl.BlockSpec((rows, LANES), row),
                  pl.BlockSpec((nseq,) + s_shape, seq)]
                 + [_const_spec(p.shape) for p in params],
        out_specs=[pl.BlockSpec((rows, GDN_WIDTH), row),
                   pl.BlockSpec((nseq,) + s_shape, seq)],
        out_shape=[jax.ShapeDtypeStruct((t, GDN_WIDTH), F32),
                   jax.ShapeDtypeStruct((b,) + s_shape, F32)],
        scratch_shapes=[pltpu.VMEM((nseq,) + s_shape, F32)],
        compiler_params=_params2(),
        name="gdn_scan",
    )(qkv, gate, small, s0, *params)


def _outffn_kernel(x_ref, y_ref, o_ref, mod_ref, nwm_ref, nwf_ref, nwp_ref,
                   wout_ref, wgu_ref, wdown_ref, out_ref):
    x = x_ref[...]
    shape3 = x.shape
    tm = shape3[0] * shape3[1]
    mod = mod_ref[...]
    g1, sh2, sc2, g2 = (mod[:, k:k + 1, :] for k in range(2, 6))
    m = (_dot(y_ref[...].astype(BF16), wout_ref[:SSD_WIDTH, :])
         + _dot(o_ref[...].astype(BF16), wout_ref[SSD_WIDTH:, :]))
    x1 = x + g1 * _rms(m, nwm_ref[...]).reshape(shape3)
    h2 = _rms(x1, nwf_ref[...]) * (1.0 + sc2) + sh2
    h2 = h2.reshape(tm, shape3[2]).astype(BF16)
    gate = _dot(h2, wgu_ref[:, :D_FF])
    up = _dot(h2, wgu_ref[:, D_FF:])
    f = _dot((_silu(gate) * up).astype(BF16), wdown_ref[...])
    out_ref[...] = x1 + g2 * _rms(f, nwp_ref[...]).reshape(shape3)


def _outffn_call(x, y, o, mod6, nw_mix_post, nw_ffn_pre, nw_ffn_post, w_out, w_gu, w_down, bb, lb):
    b, l, d = x.shape
    nl = l // lb
    tm = bb * lb
    row = lambda i, j: (i * nl + j, 0)
    return pl.pallas_call(
        _outffn_kernel,
        grid=(b // bb, nl),
        in_specs=[pl.BlockSpec((bb, lb, d), lambda i, j: (i, j, 0)),
                  pl.BlockSpec((tm, SSD_WIDTH), row),
                  pl.BlockSpec((tm, GDN_WIDTH), row),
                  pl.BlockSpec((bb,) + mod6.shape[1:], lambda i, j: (i, 0, 0)),
                  _const_spec((1, d)), _const_spec((1, d)), _const_spec((1, d)),
                  _const_spec(w_out.shape), _const_spec(w_gu.shape), _const_spec(w_down.shape)],
        out_specs=pl.BlockSpec((bb, lb, d), lambda i, j: (i, j, 0)),
        out_shape=jax.ShapeDtypeStruct((b, l, d), F32),
        compiler_params=_params2(),
        name="outproj_ffn",
    )(x, y, o, mod6, nw_mix_post, nw_ffn_pre, nw_ffn_post, w_out, w_gu, w_down)


_O_DT = SSD_WIDTH + SSD_CONV_DIM
_O_QKV = _O_DT + SSD_HEADS
_O_BETA = _O_QKV + GDN_CONV_DIM + GDN_WIDTH
_IN_DIM = _O_BETA + 2 * GDN_HEADS


def _repack_kernel(wt_ref, dt_ref, ba_ref, big_ref, small_ref):
    big_ref[...] = wt_ref[...].T.astype(BF16)

    @pl.when(pl.program_id(0) == 0)
    def _():
        k = dt_ref.shape[1]
        pad = jnp.zeros((LANES - dt_ref.shape[0] - ba_ref.shape[0], k), F32)
        small_ref[...] = jnp.concatenate([dt_ref[...], ba_ref[...], pad], axis=0).T.astype(BF16)


def _repack_w_in(w):
    k, n = w.shape
    tc = 2 * MXU_DIM
    assert n == _IN_DIM and _O_DT % tc == 0
    wt = w.T
    big = n - SSD_HEADS - 2 * GDN_HEADS
    skip = _O_QKV - _O_DT
    src_row = lambda j: (pl.multiple_of(jnp.where(j * tc < _O_DT, j * tc, j * tc + skip), skip), 0)
    window = lambda rows, index: pl.BlockSpec((pl.Element(rows), pl.Element(k)), index)
    return pl.pallas_call(
        _repack_kernel,
        grid=(big // tc,),
        in_specs=[window(tc, src_row),
                  window(SSD_HEADS, lambda j: (_O_DT, 0)),
                  window(2 * GDN_HEADS, lambda j: (_O_BETA, 0))],
        out_specs=[pl.BlockSpec((k, tc), lambda j: (0, j)), pl.BlockSpec((k, LANES), lambda j: (0, 0))],
        out_shape=[jax.ShapeDtypeStruct((k, big), BF16), jax.ShapeDtypeStruct((k, LANES), BF16)],
        compiler_params=_params2(("arbitrary",)),
        name="repack_w_in",
    )(wt, wt, wt)


def _pad_lanes(v, lane0):
    out = jnp.zeros((1, LANES), F32)
    return out.at[0, lane0:lane0 + v.shape[0]].set(v.astype(F32))


def _layer(x, mod, ssd_h0, ssd_conv0, gdn_s0, gdn_conv0, p, q, bb, lb):
    b, l, d = x.shape
    mod6 = mod.reshape(b, 6, d)
    z, xbc, qkv, gate, small, ssd_conv, gdn_conv = _inproj_call(
        x, mod6, p["norm_mix_pre"], p["w_big"], p["w_small"], ssd_conv0, gdn_conv0,
        p["ssd_conv_w"], p["ssd_conv_b"], p["gdn_conv_w"], bb, lb)
    y, ssd_h = _ssd_call(xbc, z, small, ssd_h0.reshape(b, SSD_WIDTH, SSD_STATE),
                         p["ssd_dtb"], p["ssd_a"], p["ssd_d"], p["ssd_norm_w"], b, l, q)
    o, gdn_s = _gdn_call(qkv, gate, small, gdn_s0, p["gdn_dtb"], p["gdn_a"], p["gdn_norm_w"], b, l, q)
    out = _outffn_call(x, y, o, mod6, p["norm_mix_post"], p["norm_ffn_pre"], p["norm_ffn_post"],
                       p["w_out"], p["w_gu"], p["w_down"], bb, lb)
    return out, ssd_h.reshape(ssd_h0.shape), ssd_conv, gdn_s, gdn_conv


def _layer_params(l, w_ada, b_ada, norm_mix_pre, norm_mix_post, norm_ffn_pre, norm_ffn_post, w_in, ssd_conv_w,
                  ssd_conv_b, ssd_dt_bias, ssd_A_log, ssd_D, ssd_norm_w, gdn_conv_w, gdn_dt_bias, gdn_A_log,
                  gdn_norm_w, w_out, w_gate_up, w_down):
    w_big, w_small = _repack_w_in(w_in[l])
    row = lambda v: v.reshape(1, -1).astype(F32)
    return dict(
        w_ada=w_ada[l], b_ada=b_ada[l],
        norm_mix_pre=row(norm_mix_pre[l]), norm_mix_post=row(norm_mix_post[l]),
        norm_ffn_pre=row(norm_ffn_pre[l]), norm_ffn_post=row(norm_ffn_post[l]),
        w_big=w_big, w_small=w_small,
        ssd_conv_w=ssd_conv_w[l], ssd_conv_b=row(ssd_conv_b[l]),
        ssd_dtb=_pad_lanes(ssd_dt_bias[l], DT_LANE0),
        ssd_a=_pad_lanes(-jnp.exp(ssd_A_log[l].astype(F32)), DT_LANE0),
        ssd_d=row(jnp.repeat(ssd_D[l], SSD_HEADDIM)),
        ssd_norm_w=row(ssd_norm_w[l]),
        gdn_conv_w=gdn_conv_w[l],
        gdn_dtb=_pad_lanes(gdn_dt_bias[l], ALPHA_LANE0),
        gdn_a=_pad_lanes(-jnp.exp(gdn_A_log[l].astype(F32)), ALPHA_LANE0),
        gdn_norm_w=row(gdn_norm_w[l]),
        w_out=w_out[l].astype(BF16), w_gu=w_gate_up[l].astype(BF16), w_down=w_down[l].astype(BF16),
    )


def kernel(x_prompt, x_sample, c_prompt, c_sample, state_ssd, state_ssd_conv, state_gdn, state_gdn_conv, w_ada, b_ada, norm_mix_pre, norm_mix_post, norm_ffn_pre, norm_ffn_post, w_in, ssd_conv_w, ssd_conv_b, ssd_dt_bias, ssd_A_log, ssd_D, ssd_norm_w, gdn_conv_w, gdn_dt_bias, gdn_A_log, gdn_norm_w, w_out, w_gate_up, w_down):
    depth = w_in.shape[0]
    bp, lp, _ = x_prompt.shape
    bs, ls, _ = x_sample.shape
    yp, ys = x_prompt, x_sample
    outs = [[] for _ in range(8)]
    for l in range(depth):
        p = _layer_params(l, w_ada, b_ada, norm_mix_pre, norm_mix_post, norm_ffn_pre, norm_ffn_post, w_in,
                          ssd_conv_w, ssd_conv_b, ssd_dt_bias, ssd_A_log, ssd_D, ssd_norm_w, gdn_conv_w,
                          gdn_dt_bias, gdn_A_log, gdn_norm_w, w_out, w_gate_up, w_down)
        mod = _mod_call(jnp.concatenate([c_prompt, c_sample], axis=0), p["w_ada"], p["b_ada"])
        zeros = lambda a: jnp.zeros((bp,) + a.shape[2:], a.dtype)
        yp, a0, a1, a2, a3 = _layer(yp, mod[:bp], zeros(state_ssd), zeros(state_ssd_conv), zeros(state_gdn),
                                    zeros(state_gdn_conv), p, q=min(CHUNK, lp), bb=1, lb=min(ROW_TILE, lp))
        ys, b0, b1, b2, b3 = _layer(ys, mod[bp:], state_ssd[l], state_ssd_conv[l], state_gdn[l],
                                    state_gdn_conv[l], p, q=min(CHUNK, ls), bb=min(ROW_TILE // ls, bs), lb=ls)
        for lst, v in zip(outs, (a0, a1, a2, a3, b0, b1, b2, b3)):
            lst.append(v)
    return (yp, ys) + tuple(v[0][None] if depth == 1 else jnp.stack(v) for v in outs)
```
